```python
import math
import jax, jax.numpy as jnp
from jax import lax
import numpy as np

D_MODEL = 1024
BATCH = 8
SEQ = 4096
DEPTH = 2

N_EVEN = (DEPTH + 1) // 2
N_ODD = DEPTH // 2
DN_ALPHA = (2.0 * DEPTH) ** 0.25
DN_BETA = (8.0 * DEPTH) ** -0.25

RNN_WIDTH = D_MODEL // 2
RNN_HEADS = 8
RNN_HEAD_DIM = RNN_WIDTH // RNN_HEADS
CONV_WIDTH = 4
RG_C = 8.0

MLA_HEADS = 8
MLA_NOPE = 64
MLA_ROPE = 32
MLA_V = 64
MLA_Q_RANK = D_MODEL // 4
MLA_KV_RANK = D_MODEL // 8
MLA_WIDTH = MLA_HEADS * MLA_V
ROPE_THETA = 10000.0
Q_BLOCK = 128

AB_WIDTH = RNN_WIDTH + MLA_WIDTH
AB_IN = RNN_WIDTH + AB_WIDTH + MLA_Q_RANK + MLA_KV_RANK + MLA_ROPE

SSD_INNER = 2 * D_MODEL
SSD_HEAD_DIM = 64
SSD_HEADS = SSD_INNER // SSD_HEAD_DIM
SSD_GROUPS = 4
SSD_STATE = 128
SSD_CHUNK = 128
SSD_CONV_DIM = SSD_INNER + 2 * SSD_GROUPS * SSD_STATE
SSD_IN = SSD_INNER + SSD_CONV_DIM + SSD_HEADS

kernel_name = "hybrid_rglru_mla_ssd_deepnorm"


def _rmsnorm(x, g, eps=1e-6):
    xf = x.astype(jnp.float32)
    y = xf * lax.rsqrt(jnp.mean(xf * xf, axis=-1, keepdims=True) + eps)
    return (y * g.astype(jnp.float32)).astype(x.dtype)


def _layernorm(x, g, b, eps=1e-5):
    xf = x.astype(jnp.float32)
    mu = jnp.mean(xf, axis=-1, keepdims=True)
    xc = xf - mu
    var = jnp.mean(xc * xc, axis=-1, keepdims=True)
    y = xc * lax.rsqrt(var + eps) * g.astype(jnp.float32) + b.astype(jnp.float32)
    return y.astype(x.dtype)


def _causal_conv(x, w, b):
    k_taps = w.shape[0]
    seqlen = x.shape[1]
    xp = jnp.pad(x, ((0, 0), (k_taps - 1, 0), (0, 0)))
    return sum(xp[:, k:k + seqlen] * w[k] for k in range(k_taps)) + b


def _rope(x, cos, sin):
    half = x.shape[-1] // 2
    x1, x2 = x[..., :half], x[..., half:]
    return jnp.concatenate([x1 * cos - x2 * sin, x2 * cos + x1 * sin], axis=-1)


def _rg_lru(x, w_a, b_a, w_x, b_x, lam):
    bsz, seqlen, _ = x.shape
    xh = x.reshape(bsz, seqlen, RNN_HEADS, RNN_HEAD_DIM)
    r = jax.nn.sigmoid(jnp.einsum('bshi,hij->bshj', xh, w_a).reshape(bsz, seqlen, RNN_WIDTH) + b_a)
    i = jax.nn.sigmoid(jnp.einsum('bshi,hij->bshj', xh, w_x).reshape(bsz, seqlen, RNN_WIDTH) + b_x)
    log_a = (-RG_C * r.astype(jnp.float32)) * jax.nn.softplus(-lam.astype(jnp.float32))
    a = jnp.exp(log_a)
    mult = jnp.sqrt(-jnp.expm1(2.0 * log_a))
    u = mult * (i * x).astype(jnp.float32)

    def combine(c1, c2):
        a1, b1 = c1
        a2, b2 = c2
        return a1 * a2, a2 * b1 + b2

    _, h = lax.associative_scan(combine, (a, u), axis=1)
    return h.astype(x.dtype)


def _mla_attention(q_nope, q_rope, k_nope, k_rope, v):
    seqlen = q_nope.shape[1]
    scale = (MLA_NOPE + MLA_ROPE) ** -0.5
    outs = []
    for blk in range(seqlen // Q_BLOCK):
        q0 = blk * Q_BLOCK
        kend = q0 + Q_BLOCK
        s = (jnp.einsum('bqhd,bkhd->bhqk', q_nope[:, q0:kend], k_nope[:, :kend])
             + jnp.einsum('bqhr,bkr->bhqk', q_rope[:, q0:kend], k_rope[:, :kend]))
        s = s.astype(jnp.float32) * scale
        mask = jnp.arange(kend)[None, :] <= (q0 + jnp.arange(Q_BLOCK))[:, None]
        s = jnp.where(mask, s, -jnp.inf)
        p = jax.nn.softmax(s, axis=-1).astype(v.dtype)
        outs.append(jnp.einsum('bhqk,bkhd->bqhd', p, v[:, :kend]))
    return jnp.concatenate(outs, axis=1)


def _rglru_mla_layer(x, cos, sin, w_in, conv_w, conv_b, gate_a_w, gate_a_b, gate_x_w,
                     gate_x_b, lam, q_norm, kv_norm, w_uq, w_ukv, w_out):
    bsz, seqlen, _ = x.shape
    proj = jnp.einsum('bsd,de->bse', x, w_in)
    x_rnn, gate, c_q, c_kv, k_rope = jnp.split(
        proj, [RNN_WIDTH, RNN_WIDTH + AB_WIDTH, RNN_WIDTH + AB_WIDTH + MLA_Q_RANK,
               RNN_WIDTH + AB_WIDTH + MLA_Q_RANK + MLA_KV_RANK], axis=-1)
    x_rnn = _causal_conv(x_rnn, conv_w, conv_b)
    y_rnn = _rg_lru(x_rnn, gate_a_w, gate_a_b, gate_x_w, gate_x_b, lam)
    q = jnp.einsum('bsr,re->bse', _rmsnorm(c_q, q_norm), w_uq)
    q = q.reshape(bsz, seqlen, MLA_HEADS, MLA_NOPE + MLA_ROPE)
    q_nope = q[..., :MLA_NOPE]
    q_rope = _rope(q[..., MLA_NOPE:], cos[:, :, None], sin[:, :, None])
    kv = jnp.einsum('bsr,re->bse', _rmsnorm(c_kv, kv_norm), w_ukv)
    kv = kv.reshape(bsz, seqlen, MLA_HEADS, MLA_NOPE + MLA_V)
    k_nope, v = kv[..., :MLA_NOPE], kv[..., MLA_NOPE:]
    k_rope = _rope(k_rope, cos, sin)
    y_mla = _mla_attention(q_nope, q_rope, k_nope, k_rope, v).reshape(bsz, seqlen, MLA_WIDTH)
    y = jnp.concatenate([y_rnn, y_mla], axis=-1) * jax.nn.silu(gate)
    return jnp.einsum('bse,ed->bsd', y, w_out)


def _ssd_scan(x, dt, a_neg, bm, cm):
    bsz, seqlen, n_heads, p_dim = x.shape
    n_groups, n_state = bm.shape[2], bm.shape[3]
    hpg = n_heads // n_groups
    nc, L = seqlen // SSD_CHUNK, SSD_CHUNK
    xf = (x.astype(jnp.float32) * dt[..., None]).reshape(bsz, nc, L, n_groups, hpg, p_dim)
    a_dt = (dt * a_neg).reshape(bsz, nc, L, n_groups, hpg)
    bc = bm.astype(jnp.float32).reshape(bsz, nc, L, n_groups, n_state)
    cc = cm.astype(jnp.float32).reshape(bsz, nc, L, n_groups, n_state)
    cs = jnp.cumsum(a_dt, axis=2)
    cs_h = jnp.transpose(cs, (0, 1, 3, 4, 2))
    seg = cs_h[..., :, None] - cs_h[..., None, :]
    causal = jnp.tril(jnp.ones((L, L), dtype=bool))
    decay = jnp.where(causal, jnp.exp(jnp.where(causal, seg, 0.0)), 0.0)
    cb = jnp.einsum('bclgn,bcsgn->bcgls', cc, bc)
    y_diag = jnp.einsum('bcghls,bcsghp->bclghp', cb[:, :, :, None] * decay, xf)
    decay_states = jnp.exp(cs[:, :, -1:] - cs)
    states = jnp.einsum('bclgn,bclghp->bcghpn', bc, xf * decay_states[..., None])
    chunk_decay = jnp.exp(cs[:, :, -1])

    def step(h, inp):
        dec, st = inp
        return h * dec[..., None, None] + st, h

    h0 = jnp.zeros((bsz, n_groups, hpg, p_dim, n_state), jnp.float32)
    _, prev = lax.scan(step, h0, (jnp.moveaxis(chunk_decay, 1, 0), jnp.moveaxis(states, 1, 0)))
    prev = jnp.moveaxis(prev, 0, 1)
    y_off = jnp.einsum('bclgn,bcghpn->bclghp', cc, prev) * jnp.exp(cs)[..., None]
    return (y_diag + y_off).reshape(bsz, seqlen, n_heads, p_dim)


def _ssd_layer(x, w_in, conv_w, conv_b, dt_bias, a_log, d_skip, norm_w, w_out):
    bsz, seqlen, _ = x.shape
    proj = jnp.einsum('bsd,de->bse', x, w_in)
    z, xbc, dt = jnp.split(proj, [SSD_INNER, SSD_INNER + SSD_CONV_DIM], axis=-1)
    xbc = jax.nn.silu(_causal_conv(xbc, conv_w, conv_b))
    xs, bm, cm = jnp.split(xbc, [SSD_INNER, SSD_INNER + SSD_GROUPS * SSD_STATE], axis=-1)
    xs = xs.reshape(bsz, seqlen, SSD_HEADS, SSD_HEAD_DIM)
    bm = bm.reshape(bsz, seqlen, SSD_GROUPS, SSD_STATE)
    cm = cm.reshape(bsz, seqlen, SSD_GROUPS, SSD_STATE)
    dt = jax.nn.softplus(dt.astype(jnp.float32) + dt_bias.astype(jnp.float32))
    a_neg = -jnp.exp(a_log.astype(jnp.float32))
    y = _ssd_scan(xs, dt, a_neg, bm, cm) + d_skip.astype(jnp.float32)[:, None] * xs.astype(jnp.float32)
    y = y.reshape(bsz, seqlen, SSD_INNER) * jax.nn.silu(z.astype(jnp.float32))
    yg = y.reshape(bsz, seqlen, SSD_GROUPS, SSD_INNER // SSD_GROUPS)
    yg = yg * lax.rsqrt(jnp.mean(yg * yg, axis=-1, keepdims=True) + 1e-6)
    y = (yg.reshape(bsz, seqlen, SSD_INNER) * norm_w.astype(jnp.float32)).astype(x.dtype)
    return jnp.einsum('bse,ed->bsd', y, w_out)


def setup_inputs(seed: int = 0) -> dict:
    key = jax.random.key(seed)
    ks = jax.random.split(key, 32)
    f32 = jnp.float32
    nrm = lambda k, shape, s: jax.random.normal(k, shape, f32) * s

    x = jax.random.normal(ks[0], (BATCH, SEQ, D_MODEL), f32)
    offset = jax.random.randint(ks[1], (BATCH, 1), 0, 1024, dtype=jnp.int32)
    positions = offset + jnp.arange(SEQ, dtype=jnp.int32)[None, :]

    ab_w_in = nrm(ks[2], (N_EVEN, D_MODEL, AB_IN), D_MODEL ** -0.5)
    ab_conv_w = nrm(ks[3], (N_EVEN, CONV_WIDTH, RNN_WIDTH), CONV_WIDTH ** -0.5)
    ab_conv_b = nrm(ks[4], (N_EVEN, RNN_WIDTH), 0.02)
    ab_gate_a_w = nrm(ks[5], (N_EVEN, RNN_HEADS, RNN_HEAD_DIM, RNN_HEAD_DIM), RNN_HEAD_DIM ** -0.5)
    ab_gate_a_b = nrm(ks[6], (N_EVEN, RNN_WIDTH), 0.1)
    ab_gate_x_w = nrm(ks[7], (N_EVEN, RNN_HEADS, RNN_HEAD_DIM, RNN_HEAD_DIM), RNN_HEAD_DIM ** -0.5)
    ab_gate_x_b = nrm(ks[8], (N_EVEN, RNN_WIDTH), 0.1)
    u = jax.random.uniform(ks[9], (N_EVEN, RNN_WIDTH), f32, 0.9, 0.999)
    a0 = u ** (1.0 / RG_C)
    ab_lambda = jnp.log(a0) - jnp.log1p(-a0)
    mla_q_norm = 1.0 + nrm(ks[10], (N_EVEN, MLA_Q_RANK), 0.05)
    mla_kv_norm = 1.0 + nrm(ks[11], (N_EVEN, MLA_KV_RANK), 0.05)
    mla_w_uq = nrm(ks[12], (N_EVEN, MLA_Q_RANK, MLA_HEADS * (MLA_NOPE + MLA_ROPE)), MLA_Q_RANK ** -0.5)
    mla_w_ukv = nrm(ks[13], (N_EVEN, MLA_KV_RANK, MLA_HEADS * (MLA_NOPE + MLA_V)), MLA_KV_RANK ** -0.5)
    ab_w_out = nrm(ks[14], (N_EVEN, AB_WIDTH, D_MODEL), DN_BETA * math.sqrt(2.0 / (AB_WIDTH + D_MODEL)))
    ab_ln_g = 1.0 + nrm(ks[15], (N_EVEN, D_MODEL), 0.05)
    ab_ln_b = nrm(ks[16], (N_EVEN, D_MODEL), 0.02)

    ssd_w_in = nrm(ks[17], (N_ODD, D_MODEL, SSD_IN), D_MODEL ** -0.5)
    ssd_conv_w = nrm(ks[18], (N_ODD, CONV_WIDTH, SSD_CONV_DIM), CONV_WIDTH ** -0.5)
    ssd_conv_b = nrm(ks[19], (N_ODD, SSD_CONV_DIM), 0.02)
    dt0 = jnp.exp(jax.random.uniform(ks[20], (N_ODD, SSD_HEADS), f32, math.log(1e-3), math.log(1e-1)))
    ssd_dt_bias = dt0 + jnp.log(-jnp.expm1(-dt0))
    ssd_a_log = jnp.log(jax.random.uniform(ks[21], (N_ODD, SSD_HEADS), f32, 1.0, 16.0))
    ssd_d = 1.0 + nrm(ks[22], (N_ODD, SSD_HEADS), 0.1)
    ssd_norm = 1.0 + nrm(ks[23], (N_ODD, SSD_INNER), 0.05)
    ssd_w_out = nrm(ks[24], (N_ODD, SSD_INNER, D_MODEL), DN_BETA * math.sqrt(2.0 / (SSD_INNER + D_MODEL)))
    ssd_ln_g = 1.0 + nrm(ks[25], (N_ODD, D_MODEL), 0.05)
    ssd_ln_b = nrm(ks[26], (N_ODD, D_MODEL), 0.02)

    return {"x": x, "positions": positions,
            "ab_w_in": ab_w_in, "ab_conv_w": ab_conv_w, "ab_conv_b": ab_conv_b,
            "ab_gate_a_w": ab_gate_a_w, "ab_gate_a_b": ab_gate_a_b,
            "ab_gate_x_w": ab_gate_x_w, "ab_gate_x_b": ab_gate_x_b,
            "ab_lambda": ab_lambda, "mla_q_norm": mla_q_norm, "mla_kv_norm": mla_kv_norm,
            "mla_w_uq": mla_w_uq, "mla_w_ukv": mla_w_ukv, "ab_w_out": ab_w_out,
            "ab_ln_g": ab_ln_g, "ab_ln_b": ab_ln_b,
            "ssd_w_in": ssd_w_in, "ssd_conv_w": ssd_conv_w, "ssd_conv_b": ssd_conv_b,
            "ssd_dt_bias": ssd_dt_bias, "ssd_a_log": ssd_a_log, "ssd_d": ssd_d,
            "ssd_norm": ssd_norm, "ssd_w_out": ssd_w_out,
            "ssd_ln_g": ssd_ln_g, "ssd_ln_b": ssd_ln_b}


def reference(x, positions, ab_w_in, ab_conv_w, ab_conv_b, ab_gate_a_w, ab_gate_a_b,
              ab_gate_x_w, ab_gate_x_b, ab_lambda, mla_q_norm, mla_kv_norm, mla_w_uq,
              mla_w_ukv, ab_w_out, ab_ln_g, ab_ln_b, ssd_w_in, ssd_conv_w, ssd_conv_b,
              ssd_dt_bias, ssd_a_log, ssd_d, ssd_norm, ssd_w_out, ssd_ln_g, ssd_ln_b):
    inv_freq = ROPE_THETA ** (-jnp.arange(0, MLA_ROPE, 2, dtype=jnp.float32) / MLA_ROPE)
    ang = positions.astype(jnp.float32)[..., None] * inv_freq
    cos = jnp.cos(ang).astype(x.dtype)
    sin = jnp.sin(ang).astype(x.dtype)
    for layer in range(DEPTH):
        j = layer // 2
        if layer % 2 == 0:
            y = _rglru_mla_layer(x, cos, sin, ab_w_in[j], ab_conv_w[j], ab_conv_b[j],
                                 ab_gate_a_w[j], ab_gate_a_b[j], ab_gate_x_w[j], ab_gate_x_b[j],
                                 ab_lambda[j], mla_q_norm[j], mla_kv_norm[j], mla_w_uq[j],
                                 mla_w_ukv[j], ab_w_out[j])
            x = _layernorm(DN_ALPHA * x + y, ab_ln_g[j], ab_ln_b[j])
        else:
            y = _ssd_layer(x, ssd_w_in[j], ssd_conv_w[j], ssd_conv_b[j], ssd_dt_bias[j],
                           ssd_a_log[j], ssd_d[j], ssd_norm[j], ssd_w_out[j])
            x = _layernorm(DN_ALPHA * x + y, ssd_ln_g[j], ssd_ln_b[j])
    return x
```

```python
import functools
import math

import jax
import jax.numpy as jnp
from jax import lax
from jax.experimental import pallas as pl
from jax.experimental.pallas import tpu as pltpu

D_MODEL = 1024
DEPTH = 2
DN_ALPHA = (2.0 * DEPTH) ** 0.25

RNN_WIDTH = 512
RNN_HEADS = 8
RNN_HEAD_DIM = RNN_WIDTH // RNN_HEADS
CONV_WIDTH = 4
RG_C = 8.0

MLA_HEADS = 8
MLA_NOPE = 64
MLA_ROPE = 32
MLA_V = 64
MLA_Q_RANK = 256
MLA_KV_RANK = 128
MLA_WIDTH = MLA_HEADS * MLA_V
ROPE_THETA = 10000.0
AB_WIDTH = RNN_WIDTH + MLA_WIDTH

SSD_INNER = 2048
SSD_HEAD_DIM = 64
SSD_HEADS = 32
SSD_GROUPS = 4
SSD_STATE = 128
SSD_CHUNK = 128
SSD_CONV_DIM = SSD_INNER + 2 * SSD_GROUPS * SSD_STATE
SSD_GROUP_WIDTH = SSD_INNER // SSD_GROUPS

LANES = 128
HEAD_PAD = LANES
QK_PAD_WIDTH = MLA_HEADS * HEAD_PAD
VMEM_LIMIT = 56 * 1024 * 1024

BF16 = jnp.bfloat16
F32 = jnp.float32


def _sigmoid(x):
    return 1.0 / (1.0 + jnp.exp(-x))


def _silu(x):
    return x * _sigmoid(x)


def _softplus(x):
    return jnp.maximum(x, 0.0) + jnp.log1p(jnp.exp(-jnp.abs(x)))


def _dot(a, b):
    return jnp.dot(a, b, preferred_element_type=F32)


def _const_spec(shape):
    zeros = (0,) * len(shape)
    return pl.BlockSpec(shape, lambda *_: zeros)


def _params(semantics):
    return pltpu.CompilerParams(dimension_semantics=semantics,
                                vmem_limit_bytes=VMEM_LIMIT)


def _ab_in_kernel(x_ref, pos_ref, w_in_ref, freq_ref, qn_ref, kvn_ref, wq_ref, wqr_ref,
                  wk_ref, wv_ref, vones_ref,
                  xr_ref, gate_ref, q_ref, k_ref, v_ref):
    xb = x_ref[0].astype(BF16)
    proj = _dot(xb, w_in_ref[...])
    o_gate = RNN_WIDTH
    o_cq = o_gate + AB_WIDTH
    o_ckv = o_cq + MLA_Q_RANK
    o_kr = o_ckv + MLA_KV_RANK
    o_krr = o_kr + HEAD_PAD

    xr_ref[0] = proj[:, :o_gate]
    gate_ref[0] = _silu(proj[:, o_gate:o_cq]).astype(BF16)

    ang = pos_ref[0] * freq_ref[...]
    cos = jnp.cos(ang)
    sin = jnp.sin(ang)
    cos_h = jnp.concatenate([cos] * MLA_HEADS, axis=1)
    sin_h = jnp.concatenate([sin] * MLA_HEADS, axis=1)

    c_q = proj[:, o_cq:o_ckv]
    c_q = c_q * lax.rsqrt(jnp.mean(c_q * c_q, axis=-1, keepdims=True) + 1e-6) * qn_ref[...]
    c_qb = c_q.astype(BF16)
    q = _dot(c_qb, wq_ref[...]) * cos_h + _dot(c_qb, wqr_ref[...]) * sin_h
    scale = (MLA_NOPE + MLA_ROPE) ** -0.5
    q_ref[0] = (q * scale).astype(BF16)

    c_kv = proj[:, o_ckv:o_kr]
    c_kv = c_kv * lax.rsqrt(jnp.mean(c_kv * c_kv, axis=-1, keepdims=True) + 1e-6) * kvn_ref[...]
    c_kvb = c_kv.astype(BF16)
    k_rope = proj[:, o_kr:o_krr] * cos + proj[:, o_krr:o_krr + HEAD_PAD] * sin
    k = _dot(c_kvb, wk_ref[...]) + jnp.concatenate([k_rope] * MLA_HEADS, axis=1)
    k_ref[0] = k.astype(BF16)
    v = _dot(c_kvb, wv_ref[...]) + vones_ref[...]
    v_ref[0] = v.astype(BF16)


def _ab_in_call(x, pos_f, w_in_ext, freq, q_norm, kv_norm, wq, wqr, wk, wv, vones, tm):
    bsz, seqlen, _ = x.shape
    n_in = w_in_ext.shape[1]
    grid = (bsz, seqlen // tm)
    tok = lambda width: pl.BlockSpec((1, tm, width), lambda b, i: (b, i, 0))
    return pl.pallas_call(
        _ab_in_kernel,
        grid=grid,
        in_specs=[tok(D_MODEL), tok(1), _const_spec((D_MODEL, n_in)), _const_spec((1, HEAD_PAD)),
                  _const_spec((1, MLA_Q_RANK)), _const_spec((1, MLA_KV_RANK)),
                  _const_spec((MLA_Q_RANK, QK_PAD_WIDTH)), _const_spec((MLA_Q_RANK, QK_PAD_WIDTH)),
                  _const_spec((MLA_KV_RANK, QK_PAD_WIDTH)), _const_spec((MLA_KV_RANK, QK_PAD_WIDTH)),
                  _const_spec((1, QK_PAD_WIDTH))],
        out_specs=[tok(RNN_WIDTH), tok(AB_WIDTH), tok(QK_PAD_WIDTH), tok(QK_PAD_WIDTH),
                   tok(QK_PAD_WIDTH)],
        out_shape=[jax.ShapeDtypeStruct((bsz, seqlen, RNN_WIDTH), F32),
                   jax.ShapeDtypeStruct((bsz, seqlen, AB_WIDTH), BF16),
                   jax.ShapeDtypeStruct((bsz, seqlen, QK_PAD_WIDTH), BF16),
                   jax.ShapeDtypeStruct((bsz, seqlen, QK_PAD_WIDTH), BF16),
                   jax.ShapeDtypeStruct((bsz, seqlen, QK_PAD_WIDTH), BF16)],
        compiler_params=_params(("parallel", "parallel")),
        name="ab_in_proj",
    )(x, pos_f, w_in_ext, freq, q_norm, kv_norm, wq, wqr, wk, wv, vones)


def _rglru_kernel(xr_ref, conv_w_ref, conv_b_ref, w_gate_ref, b_gate_ref, lam_ref,
                  h_ref, tail_ref, carry_ref, a_ref, u_ref):
    bsz, ts, width = xr_ref.shape

    @pl.when(pl.program_id(0) == 0)
    def _():
        tail_ref[...] = jnp.zeros_like(tail_ref)
        carry_ref[...] = jnp.zeros_like(carry_ref)

    x_blk = pltpu.einshape("btc->tbc", xr_ref[...])
    x_ext = jnp.concatenate([tail_ref[...], x_blk], axis=0)
    tail_ref[...] = x_blk[ts - (CONV_WIDTH - 1):]
    xc = conv_b_ref[...][None]
    for k in range(CONV_WIDTH):
        xc = xc + x_ext[k:k + ts] * conv_w_ref[k:k + 1, :][None]

    xc2 = xc.reshape(ts * bsz, width)
    gates = _dot(xc2.astype(BF16), w_gate_ref[...]) + b_gate_ref[...]
    r = _sigmoid(gates[:, :width])
    i = _sigmoid(gates[:, width:])
    log_a = (-RG_C * r) * _softplus(-lam_ref[...])
    a = jnp.exp(log_a)
    mult = jnp.sqrt(-jnp.tanh(log_a) * (a * a + 1.0))
    u = mult * (i * xc2)
    a_ref[...] = a.reshape(ts, bsz, width)
    u_ref[...] = u.reshape(ts, bsz, width)

    def step(t, h):
        h = a_ref[t] * h + u_ref[t]
        u_ref[t] = h
        return h

    carry_ref[...] = lax.fori_loop(0, ts, step, carry_ref[...], unroll=8)
    h_ref[...] = pltpu.einshape("tbc->btc", u_ref[...])


def _rglru_call(xr, conv_w, conv_b, w_gate, b_gate, lam, ts):
    bsz, seqlen, width = xr.shape
    blk = pl.BlockSpec((bsz, ts, width), lambda i: (0, i, 0))
    return pl.pallas_call(
        _rglru_kernel,
        grid=(seqlen // ts,),
        in_specs=[blk, _const_spec((CONV_WIDTH, width)), _const_spec((1, width)),
                  _const_spec((width, 2 * width)), _const_spec((1, 2 * width)),
                  _const_spec((1, width))],
        out_specs=blk,
        out_shape=jax.ShapeDtypeStruct((bsz, seqlen, width), F32),
        scratch_shapes=[pltpu.VMEM((CONV_WIDTH - 1, bsz, width), F32),
                        pltpu.VMEM((bsz, width), F32),
                        pltpu.VMEM((ts, bsz, width), F32),
                        pltpu.VMEM((ts, bsz, width), F32)],
        compiler_params=_params(("arbitrary",)),
        name="rglru_scan",
    )(xr, conv_w, conv_b, w_gate, b_gate, lam)


def _mla_attn_kernel(q_ref, k_ref, v_ref, o_ref, *, tq):
    qi = pl.program_id(2)
    heads = q_ref.shape[2] // HEAD_PAD
    row = lax.broadcasted_iota(jnp.int32, (tq, tq), 0)
    col = lax.broadcasted_iota(jnp.int32, (tq, tq), 1)
    causal = col <= row

    def scores(q_h, j, h):
        k_blk = k_ref[0, pl.ds(pl.multiple_of(j * tq, tq), tq), h * HEAD_PAD:(h + 1) * HEAD_PAD]
        return lax.dot_general(q_h, k_blk, (((1,), (1,)), ((), ())),
                               preferred_element_type=F32)

    def update(carry, s, j, h):
        m, acc = carry
        m_new = jnp.maximum(m, jnp.max(s, axis=-1, keepdims=True))
        p = jnp.exp(s - m_new)
        alpha = jnp.exp(m - m_new)
        v_blk = v_ref[0, pl.ds(pl.multiple_of(j * tq, tq), tq), h * HEAD_PAD:(h + 1) * HEAD_PAD]
        acc = alpha * acc + _dot(p.astype(BF16), v_blk)
        return m_new, acc

    qs = [q_ref[0, :, h * HEAD_PAD:(h + 1) * HEAD_PAD] for h in range(heads)]

    def body(j, carries):
        return tuple(update(carries[h], scores(qs[h], j, h), j, h) for h in range(heads))

    init = tuple((jnp.full((tq, 1), -1e30, F32), jnp.zeros((tq, HEAD_PAD), F32))
                 for _ in range(heads))
    carries = lax.fori_loop(0, qi, body, init)

    outs = []
    for h in range(heads):
        s = jnp.where(causal, scores(qs[h], qi, h), -1e30)
        _, acc = update(carries[h], s, qi, h)
        outs.append(acc / acc[:, MLA_V:MLA_V + 1])
    lane = lax.broadcasted_iota(jnp.int32, (tq, HEAD_PAD), 1)
    for pair in range(heads // 2):
        o_pair = jnp.where(lane < MLA_V, outs[2 * pair],
                           pltpu.roll(outs[2 * pair + 1], MLA_V, axis=1))
        o_ref[0, :, pair * HEAD_PAD:(pair + 1) * HEAD_PAD] = o_pair.astype(o_ref.dtype)


def _mla_attn_call(q, k, v, tq, heads_per_step=2):
    bsz, seqlen, _ = q.shape
    width = heads_per_step * HEAD_PAD
    n_hsteps = MLA_HEADS // heads_per_step
    out_w = heads_per_step * MLA_V
    return pl.pallas_call(
        functools.partial(_mla_attn_kernel, tq=tq),
        grid=(bsz, n_hsteps, seqlen // tq),
        in_specs=[pl.BlockSpec((1, tq, width), lambda b, h, i: (b, i, h)),
                  pl.BlockSpec((1, seqlen, width), lambda b, h, i: (b, 0, h)),
                  pl.BlockSpec((1, seqlen, width), lambda b, h, i: (b, 0, h))],
        out_specs=pl.BlockSpec((1, tq, out_w), lambda b, h, i: (b, i, h)),
        out_shape=jax.ShapeDtypeStruct((bsz, seqlen, MLA_WIDTH), BF16),
        compiler_params=_params(("parallel", "parallel", "arbitrary")),
        name="mla_attention",
    )(q, k, v)


def _layernorm(z, g, b):
    mu = jnp.mean(z, axis=-1, keepdims=True)
    zc = z - mu
    var = jnp.mean(zc * zc, axis=-1, keepdims=True)
    return zc * lax.rsqrt(var + 1e-5) * g + b


def _ab_out_kernel(h_ref, ya_ref, gate_ref, x_ref, w_ref, g_ref, b_ref, o_ref):
    gate = gate_ref[0].astype(F32)
    y_rnn = (h_ref[0] * gate[:, :RNN_WIDTH]).astype(BF16)
    y_mla = (ya_ref[0].astype(F32) * gate[:, RNN_WIDTH:]).astype(BF16)
    y = _dot(y_rnn, w_ref[:RNN_WIDTH, :]) + _dot(y_mla, w_ref[RNN_WIDTH:, :])
    o_ref[0] = _layernorm(DN_ALPHA * x_ref[0] + y, g_ref[...], b_ref[...])


def _ab_out_call(h, y_mla, gate, x, w_out, ln_g, ln_b, tm):
    bsz, seqlen, _ = x.shape
    tok = lambda width: pl.BlockSpec((1, tm, width), lambda b, i: (b, i, 0))
    return pl.pallas_call(
        _ab_out_kernel,
        grid=(bsz, seqlen // tm),
        in_specs=[tok(RNN_WIDTH), tok(MLA_WIDTH), tok(AB_WIDTH), tok(D_MODEL),
                  _const_spec((AB_WIDTH, D_MODEL)), _const_spec((1, D_MODEL)),
                  _const_spec((1, D_MODEL))],
        out_specs=tok(D_MODEL),
        out_shape=jax.ShapeDtypeStruct((bsz, seqlen, D_MODEL), F32),
        compiler_params=_params(("parallel", "parallel")),
        name="ab_out_proj",
    )(h, y_mla, gate, x, w_out, ln_g, ln_b)


def _ssd_in_kernel(x_ref, wz_ref, wx_ref, wdt_ref, z_ref, xbc_ref, dt_ref):
    xb = x_ref[0].astype(BF16)
    z_ref[0] = _silu(_dot(xb, wz_ref[...])).astype(BF16)
    xbc_ref[0] = _dot(xb, wx_ref[...]).astype(BF16)
    dt_ref[0] = _dot(xb, wdt_ref[...])


def _ssd_in_call(x, wz, wx, wdt, tm):
    bsz, seqlen, _ = x.shape
    tok = lambda width: pl.BlockSpec((1, tm, width), lambda b, i: (b, i, 0))
    return pl.pallas_call(
        _ssd_in_kernel,
        grid=(bsz, seqlen // tm),
        in_specs=[tok(D_MODEL), _const_spec((D_MODEL, SSD_INNER)),
                  _const_spec((D_MODEL, SSD_CONV_DIM)), _const_spec((D_MODEL, LANES))],
        out_specs=[tok(SSD_INNER), tok(SSD_CONV_DIM), tok(LANES)],
        out_shape=[jax.ShapeDtypeStruct((bsz, seqlen, SSD_INNER), BF16),
                   jax.ShapeDtypeStruct((bsz, seqlen, SSD_CONV_DIM), BF16),
                   jax.ShapeDtypeStruct((bsz, seqlen, LANES), F32)],
        compiler_params=_params(("parallel", "parallel")),
        name="ssd_in_proj",
    )(x, wz, wx, wdt)


def _cumsum_rows(x):
    n = x.shape[0]
    row = lax.broadcasted_iota(jnp.int32, x.shape, 0)
    shift = 1
    while shift < n:
        x = x + jnp.where(row >= shift, pltpu.roll(x, shift, axis=0), 0.0)
        shift *= 2
    return x


def _ssd_chunk(xbc, dt_raw, state_ref, dt_bias, a_neg, d_skip_x):
    L = SSD_CHUNK
    gw = SSD_GROUPS * SSD_STATE
    xs = xbc[:, :SSD_INNER]
    xs_b = xs.astype(BF16)
    bm = xbc[:, SSD_INNER:SSD_INNER + gw]
    cm_b = xbc[:, SSD_INNER + gw:].astype(BF16)

    dt = _softplus(dt_raw + dt_bias)
    cs = _cumsum_rows(dt * a_neg)
    cs_t = cs.T
    dt_t = dt.T
    w_t = dt_t * jnp.exp(cs_t[:, L - 1:L] - cs_t)

    row = lax.broadcasted_iota(jnp.int32, (L, L), 0)
    col = lax.broadcasted_iota(jnp.int32, (L, L), 1)
    tril = col <= row
    lane = lax.broadcasted_iota(jnp.int32, (L, LANES), 1)
    low = lane < SSD_HEAD_DIM

    hpg = SSD_HEADS // SSD_GROUPS
    y_parts = []
    for g in range(SSD_GROUPS):
        b_g = bm[:, g * SSD_STATE:(g + 1) * SSD_STATE]
        c_g = cm_b[:, g * SSD_STATE:(g + 1) * SSD_STATE]
        cb = lax.dot_general(c_g, b_g.astype(BF16), (((1,), (1,)), ((), ())),
                             preferred_element_type=F32)
        cb = jnp.where(tril, cb, 0.0)
        b_t = b_g.T
        glanes = slice(g * SSD_GROUP_WIDTH, (g + 1) * SSD_GROUP_WIDTH)
        y_off = _dot(c_g, state_ref[:, glanes].astype(BF16))
        for pair in range(hpg // 2):
            h0 = g * hpg + 2 * pair
            plane = slice(h0 * SSD_HEAD_DIM, (h0 + 2) * SSD_HEAD_DIM)
            lhs_rows = []
            decay_out = []
            for h in (h0, h0 + 1):
                seg = cs[:, h:h + 1] - cs_t[h:h + 1, :]
                m_h = jnp.exp(jnp.minimum(seg, 0.0)) * cb * dt_t[h:h + 1, :]
                bw_h = b_t * w_t[h:h + 1, :]
                lhs_rows.append((m_h.astype(BF16), bw_h.astype(BF16)))
                decay_out.append(jnp.exp(jnp.broadcast_to(cs[:, h:h + 1], (L, LANES))))
            lhs = jnp.concatenate(
                [jnp.concatenate([lhs_rows[0][0], lhs_rows[1][0]], axis=1),
                 jnp.concatenate([lhs_rows[0][1], lhs_rows[1][1]], axis=1)], axis=0)
            x_pair = xs_b[:, plane]
            zero = jnp.zeros_like(x_pair)
            rhs = jnp.concatenate([jnp.where(low, x_pair, zero),
                                   jnp.where(low, zero, x_pair)], axis=0)
            res = _dot(lhs, rhs)
            dec = jnp.where(low, decay_out[0], decay_out[1])
            off = pair * 2 * SSD_HEAD_DIM
            y_pair = res[:L] + dec * y_off[:, off:off + 2 * SSD_HEAD_DIM]
            y_parts.append(y_pair)
            state_ref[:, plane] = state_ref[:, plane] * dec[L - 1:L, :] + res[L:]
    y = jnp.concatenate(y_parts, axis=1)
    return y + d_skip_x * xs


def _ssd_kernel(z_ref, xbc_ref, dt_ref, x_ref, conv_w_ref, conv_b_ref, dt_bias_ref, a_log_ref,
                d_skip_ref, norm_ref, w_out_ref, g_ref, b_ref, o_ref, state_ref, tail_ref,
                *, n_chunks):
    L = SSD_CHUNK

    @pl.when(pl.program_id(1) == 0)
    def _():
        state_ref[...] = jnp.zeros_like(state_ref)
        tail_ref[...] = jnp.zeros_like(tail_ref)

    a_neg = -jnp.exp(a_log_ref[...])
    for c in range(n_chunks):
        rows = slice(c * L, (c + 1) * L)
        x_blk = xbc_ref[0, rows, :].astype(F32)
        x_ext = jnp.concatenate([tail_ref[...], x_blk], axis=0)
        tail_ref[...] = x_blk[L - 8:]
        conv = conv_b_ref[...]
        for k in range(CONV_WIDTH):
            off = 8 - (CONV_WIDTH - 1) + k
            conv = conv + x_ext[off:off + L] * conv_w_ref[k:k + 1, :]
        xbc = _silu(conv)
        y = _ssd_chunk(xbc, dt_ref[0, rows, :], state_ref, dt_bias_ref[...], a_neg,
                       d_skip_ref[...])
        y = y * z_ref[0, rows, :].astype(F32)
        parts = []
        for g in range(SSD_GROUPS):
            yg = y[:, g * SSD_GROUP_WIDTH:(g + 1) * SSD_GROUP_WIDTH]
            parts.append(yg * lax.rsqrt(jnp.mean(yg * yg, axis=-1, keepdims=True) + 1e-6))
        yn = (jnp.concatenate(parts, axis=1) * norm_ref[...]).astype(BF16)
        out = _dot(yn, w_out_ref[...])
        o_ref[0, rows, :] = _layernorm(DN_ALPHA * x_ref[0, rows, :] + out, g_ref[...], b_ref[...])


def _ssd_call(z, xbc, dt, x, conv_w, conv_b, dt_bias, a_log, d_skip_x, norm_w, w_out, ln_g, ln_b,
              n_chunks):
    bsz, seqlen, _ = x.shape
    tm = n_chunks * SSD_CHUNK
    tok = lambda width: pl.BlockSpec((1, tm, width), lambda b, i: (b, i, 0))
    return pl.pallas_call(
        functools.partial(_ssd_kernel, n_chunks=n_chunks),
        grid=(bsz, seqlen // tm),
        in_specs=[tok(SSD_INNER), tok(SSD_CONV_DIM), tok(LANES), tok(D_MODEL),
                  _const_spec((CONV_WIDTH, SSD_CONV_DIM)), _const_spec((1, SSD_CONV_DIM)),
                  _const_spec((1, LANES)), _const_spec((1, LANES)), _const_spec((1, SSD_INNER)),
                  _const_spec((1, SSD_INNER)), _const_spec((SSD_INNER, D_MODEL)),
                  _const_spec((1, D_MODEL)), _const_spec((1, D_MODEL))],
        out_specs=tok(D_MODEL),
        out_shape=jax.ShapeDtypeStruct((bsz, seqlen, D_MODEL), F32),
        scratch_shapes=[pltpu.VMEM((SSD_STATE, SSD_INNER), F32),
                        pltpu.VMEM((8, SSD_CONV_DIM), F32)],
        compiler_params=_params(("parallel", "arbitrary")),
        name="ssd_scan_out",
    )(z, xbc, dt, x, conv_w, conv_b, dt_bias, a_log, d_skip_x, norm_w, w_out, ln_g, ln_b)


def _rot_cols(w):
    half = MLA_ROPE // 2
    return jnp.concatenate([-w[..., half:], w[..., :half]], axis=-1)


def _pad_heads(w_heads):
    r, h, c = w_heads.shape
    return jnp.pad(w_heads, ((0, 0), (0, 0), (0, HEAD_PAD - c))).reshape(r, h * HEAD_PAD)


def _block_diag(w):
    h, d, _ = w.shape
    eye = jnp.eye(h, dtype=w.dtype)
    return (eye[:, None, :, None] * w[:, :, None, :]).reshape(h * d, h * d)


def _layer0(x, pos_f, freq, w_in, conv_w, conv_b, gate_a_w, gate_a_b, gate_x_w, gate_x_b, lam,
            q_norm, kv_norm, w_uq, w_ukv, w_out, ln_g, ln_b, tm, ts, tq):
    o_kr = RNN_WIDTH + AB_WIDTH + MLA_Q_RANK + MLA_KV_RANK
    w_kr = w_in[:, o_kr:]
    pad_kr = lambda w: jnp.pad(w, ((0, 0), (MLA_NOPE, HEAD_PAD - MLA_NOPE - MLA_ROPE)))
    w_in_ext = jnp.concatenate([w_in[:, :o_kr], pad_kr(w_kr), pad_kr(_rot_cols(w_kr))],
                               axis=1).astype(BF16)

    uq = w_uq.reshape(MLA_Q_RANK, MLA_HEADS, MLA_NOPE + MLA_ROPE)
    uq_rot = jnp.concatenate([jnp.zeros_like(uq[..., :MLA_NOPE]), _rot_cols(uq[..., MLA_NOPE:])],
                             axis=-1)
    wq = _pad_heads(uq).astype(BF16)
    wqr = _pad_heads(uq_rot).astype(BF16)
    ukv = w_ukv.reshape(MLA_KV_RANK, MLA_HEADS, MLA_NOPE + MLA_V)
    wk = _pad_heads(ukv[..., :MLA_NOPE]).astype(BF16)
    wv = _pad_heads(ukv[..., MLA_NOPE:]).astype(BF16)
    vones = jnp.tile((jnp.arange(HEAD_PAD) >= MLA_V).astype(F32), MLA_HEADS)[None]

    xr, gate, q, k, v = _ab_in_call(x, pos_f, w_in_ext, freq, q_norm[None], kv_norm[None],
                                    wq, wqr, wk, wv, vones, tm)

    w_gate = jnp.concatenate([_block_diag(gate_a_w), _block_diag(gate_x_w)], axis=1).astype(BF16)
    b_gate = jnp.concatenate([gate_a_b, gate_x_b])[None]
    h = _rglru_call(xr, conv_w, conv_b[None], w_gate, b_gate, lam[None], ts)

    y_mla = _mla_attn_call(q, k, v, tq)
    return _ab_out_call(h, y_mla, gate, x, w_out.astype(BF16), ln_g[None], ln_b[None], tm)


def _layer1(x, w_in, conv_w, conv_b, dt_bias, a_log, d_skip, norm_w, w_out, ln_g, ln_b,
            tm, n_chunks):
    wz = w_in[:, :SSD_INNER].astype(BF16)
    wx = w_in[:, SSD_INNER:SSD_INNER + SSD_CONV_DIM].astype(BF16)
    pad_h = lambda a: jnp.pad(a, ((0, 0), (0, LANES - SSD_HEADS)))
    wdt = pad_h(w_in[:, SSD_INNER + SSD_CONV_DIM:]).astype(BF16)
    z, xbc, dt = _ssd_in_call(x, wz, wx, wdt, tm)
    d_skip_x = jnp.repeat(d_skip, SSD_HEAD_DIM)[None]
    return _ssd_call(z, xbc, dt, x, conv_w, conv_b[None], pad_h(dt_bias[None]), pad_h(a_log[None]),
                     d_skip_x, norm_w[None], w_out.astype(BF16), ln_g[None], ln_b[None], n_chunks)


def kernel(x, positions, ab_w_in, ab_conv_w, ab_conv_b, ab_gate_a_w, ab_gate_a_b, ab_gate_x_w,
           ab_gate_x_b, ab_lambda, mla_q_norm, mla_kv_norm, mla_w_uq, mla_w_ukv, ab_w_out,
           ab_ln_g, ab_ln_b, ssd_w_in, ssd_conv_w, ssd_conv_b, ssd_dt_bias, ssd_a_log, ssd_d,
           ssd_norm, ssd_w_out, ssd_ln_g, ssd_ln_b):
    seqlen = x.shape[1]
    tm = min(512, seqlen)
    ts = min(128, seqlen)
    tq = min(256, seqlen)
    n_chunks = min(2, seqlen // SSD_CHUNK)

    inv_freq = ROPE_THETA ** (-jnp.arange(0, MLA_ROPE, 2, dtype=F32) / MLA_ROPE)
    freq = jnp.concatenate([jnp.zeros((MLA_NOPE,), F32), inv_freq, inv_freq,
                            jnp.zeros((HEAD_PAD - MLA_NOPE - MLA_ROPE,), F32)])[None]
    pos_f = positions.astype(F32)[..., None]

    for layer in range(DEPTH):
        j = layer // 2
        if layer % 2 == 0:
            x = _layer0(x, pos_f, freq, ab_w_in[j], ab_conv_w[j], ab_conv_b[j], ab_gate_a_w[j],
                        ab_gate_a_b[j], ab_gate_x_w[j], ab_gate_x_b[j], ab_lambda[j],
                        mla_q_norm[j], mla_kv_norm[j], mla_w_uq[j], mla_w_ukv[j], ab_w_out[j],
                        ab_ln_g[j], ab_ln_b[j], tm, ts, tq)
        else:
            x = _layer1(x, ssd_w_in[j], ssd_conv_w[j], ssd_conv_b[j], ssd_dt_bias[j],
                        ssd_a_log[j], ssd_d[j], ssd_norm[j], ssd_w_out[j], ssd_ln_g[j],
                        ssd_ln_b[j], tm, n_chunks)
    return x
```

```python
import functools
import math

import jax
import jax.numpy as jnp
from jax import lax
from jax.experimental import pallas as pl
from jax.experimental.pallas import tpu as pltpu

D_MODEL = 1024
DEPTH = 2
DN_ALPHA = (2.0 * DEPTH) ** 0.25

RNN_WIDTH = 512
RNN_HEADS = 8
RNN_HEAD_DIM = RNN_WIDTH // RNN_HEADS
CONV_WIDTH = 4
RG_C = 8.0

MLA_HEADS = 8
MLA_NOPE = 64
MLA_ROPE = 32
MLA_V = 64
MLA_Q_RANK = 256
MLA_KV_RANK = 128
MLA_WIDTH = MLA_HEADS * MLA_V
ROPE_THETA = 10000.0
AB_WIDTH = RNN_WIDTH + MLA_WIDTH

SSD_INNER = 2048
SSD_HEAD_DIM = 64
SSD_HEADS = 32
SSD_GROUPS = 4
SSD_STATE = 128
SSD_CHUNK = 128
SSD_CONV_DIM = SSD_INNER + 2 * SSD_GROUPS * SSD_STATE
SSD_GROUP_WIDTH = SSD_INNER // SSD_GROUPS

LANES = 128
HEAD_PAD = LANES
QK_PAD_WIDTH = MLA_HEADS * HEAD_PAD
VMEM_LIMIT = 56 * 1024 * 1024

BF16 = jnp.bfloat16
F32 = jnp.float32


def _sigmoid(x):
    return 1.0 / (1.0 + jnp.exp(-x))


def _silu(x):
    return x * _sigmoid(x)


def _softplus(x):
    return jnp.maximum(x, 0.0) + jnp.log1p(jnp.exp(-jnp.abs(x)))


def _dot(a, b):
    return jnp.dot(a, b, preferred_element_type=F32)


def _const_spec(shape):
    zeros = (0,) * len(shape)
    return pl.BlockSpec(shape, lambda *_: zeros)


def _params(semantics):
    return pltpu.CompilerParams(dimension_semantics=semantics,
                                vmem_limit_bytes=VMEM_LIMIT)


def _ab_in_kernel(x_ref, pos_ref, w_in_ref, freq_ref, qn_ref, kvn_ref, wq_ref, wqr_ref,
                  wk_ref, wv_ref, vones_ref,
                  xr_ref, gate_ref, q_ref, k_ref, v_ref):
    xb = x_ref[0].astype(BF16)
    proj = _dot(xb, w_in_ref[...])
    o_gate = RNN_WIDTH
    o_cq = o_gate + AB_WIDTH
    o_ckv = o_cq + MLA_Q_RANK
    o_kr = o_ckv + MLA_KV_RANK
    o_krr = o_kr + HEAD_PAD

    xr_ref[0] = proj[:, :o_gate]
    gate_ref[0] = _silu(proj[:, o_gate:o_cq]).astype(BF16)

    ang = pos_ref[0] * freq_ref[...]
    cos = jnp.cos(ang)
    sin = jnp.sin(ang)
    cos_h = jnp.concatenate([cos] * MLA_HEADS, axis=1)
    sin_h = jnp.concatenate([sin] * MLA_HEADS, axis=1)

    c_q = proj[:, o_cq:o_ckv]
    c_q = c_q * lax.rsqrt(jnp.mean(c_q * c_q, axis=-1, keepdims=True) + 1e-6) * qn_ref[...]
    c_qb = c_q.astype(BF16)
    q = _dot(c_qb, wq_ref[...]) * cos_h + _dot(c_qb, wqr_ref[...]) * sin_h
    scale = (MLA_NOPE + MLA_ROPE) ** -0.5 * math.log2(math.e)
    q_ref[0] = (q * scale).astype(BF16)

    c_kv = proj[:, o_ckv:o_kr]
    c_kv = c_kv * lax.rsqrt(jnp.mean(c_kv * c_kv, axis=-1, keepdims=True) + 1e-6) * kvn_ref[...]
    c_kvb = c_kv.astype(BF16)
    k_rope = proj[:, o_kr:o_krr] * cos + proj[:, o_krr:o_krr + HEAD_PAD] * sin
    k = _dot(c_kvb, wk_ref[...]) + jnp.concatenate([k_rope] * MLA_HEADS, axis=1)
    k_ref[0] = k.astype(BF16)
    v = _dot(c_kvb, wv_ref[...]) + vones_ref[...]
    v_ref[0] = v.astype(BF16)


def _ab_in_call(x, pos_f, w_in_ext, freq, q_norm, kv_norm, wq, wqr, wk, wv, vones, tm):
    bsz, seqlen, _ = x.shape
    n_in = w_in_ext.shape[1]
    grid = (bsz, seqlen // tm)
    tok = lambda width: pl.BlockSpec((1, tm, width), lambda b, i: (b, i, 0))
    return pl.pallas_call(
        _ab_in_kernel,
        grid=grid,
        in_specs=[tok(D_MODEL), tok(1), _const_spec((D_MODEL, n_in)), _const_spec((1, HEAD_PAD)),
                  _const_spec((1, MLA_Q_RANK)), _const_spec((1, MLA_KV_RANK)),
                  _const_spec((MLA_Q_RANK, QK_PAD_WIDTH)), _const_spec((MLA_Q_RANK, QK_PAD_WIDTH)),
                  _const_spec((MLA_KV_RANK, QK_PAD_WIDTH)), _const_spec((MLA_KV_RANK, QK_PAD_WIDTH)),
                  _const_spec((1, QK_PAD_WIDTH))],
        out_specs=[tok(RNN_WIDTH), tok(AB_WIDTH), tok(QK_PAD_WIDTH), tok(QK_PAD_WIDTH),
                   tok(QK_PAD_WIDTH)],
        out_shape=[jax.ShapeDtypeStruct((bsz, seqlen, RNN_WIDTH), F32),
                   jax.ShapeDtypeStruct((bsz, seqlen, AB_WIDTH), BF16),
                   jax.ShapeDtypeStruct((bsz, seqlen, QK_PAD_WIDTH), BF16),
                   jax.ShapeDtypeStruct((bsz, seqlen, QK_PAD_WIDTH), BF16),
                   jax.ShapeDtypeStruct((bsz, seqlen, QK_PAD_WIDTH), BF16)],
        compiler_params=_params(("parallel", "parallel")),
        name="ab_in_proj",
    )(x, pos_f, w_in_ext, freq, q_norm, kv_norm, wq, wqr, wk, wv, vones)


def _rglru_kernel(xr_ref, conv_w_ref, conv_b_ref, w_gate_ref, b_gate_ref, lam_ref,
                  h_ref, tail_ref, carry_ref, a_ref, u_ref):
    bsz, ts, width = xr_ref.shape

    @pl.when(pl.program_id(0) == 0)
    def _():
        tail_ref[...] = jnp.zeros_like(tail_ref)
        carry_ref[...] = jnp.zeros_like(carry_ref)

    x_blk = pltpu.einshape("btc->tbc", xr_ref[...])
    x_ext = jnp.concatenate([tail_ref[...], x_blk], axis=0)
    tail_ref[...] = x_blk[ts - (CONV_WIDTH - 1):]
    xc = conv_b_ref[...][None]
    for k in range(CONV_WIDTH):
        xc = xc + x_ext[k:k + ts] * conv_w_ref[k:k + 1, :][None]

    xc2 = xc.reshape(ts * bsz, width)
    gates = _dot(xc2.astype(BF16), w_gate_ref[...]) + b_gate_ref[...]
    r = _sigmoid(gates[:, :width])
    i = _sigmoid(gates[:, width:])
    log_a = (-RG_C * r) * _softplus(-lam_ref[...])
    a = jnp.exp(log_a)
    mult = jnp.sqrt(-jnp.tanh(log_a) * (a * a + 1.0))
    u = mult * (i * xc2)
    a_ref[...] = a.reshape(ts, bsz, width)
    u_ref[...] = u.reshape(ts, bsz, width)

    def step(t, h):
        h = a_ref[t] * h + u_ref[t]
        u_ref[t] = h
        return h

    carry_ref[...] = lax.fori_loop(0, ts, step, carry_ref[...], unroll=8)
    h_ref[...] = pltpu.einshape("tbc->btc", u_ref[...])


def _rglru_call(xr, conv_w, conv_b, w_gate, b_gate, lam, ts):
    bsz, seqlen, width = xr.shape
    blk = pl.BlockSpec((bsz, ts, width), lambda i: (0, i, 0))
    return pl.pallas_call(
        _rglru_kernel,
        grid=(seqlen // ts,),
        in_specs=[blk, _const_spec((CONV_WIDTH, width)), _const_spec((1, width)),
                  _const_spec((width, 2 * width)), _const_spec((1, 2 * width)),
                  _const_spec((1, width))],
        out_specs=blk,
        out_shape=jax.ShapeDtypeStruct((bsz, seqlen, width), F32),
        scratch_shapes=[pltpu.VMEM((CONV_WIDTH - 1, bsz, width), F32),
                        pltpu.VMEM((bsz, width), F32),
                        pltpu.VMEM((ts, bsz, width), F32),
                        pltpu.VMEM((ts, bsz, width), F32)],
        compiler_params=_params(("arbitrary",)),
        name="rglru_scan",
    )(xr, conv_w, conv_b, w_gate, b_gate, lam)


ATTN_ROWS = 256


def _mla_attn_kernel(q_ref, k_ref, v_ref, o_ref, *, tq):
    qi = pl.program_id(2)
    heads = q_ref.shape[2] // HEAD_PAD
    halves = tq // ATTN_ROWS
    units = [(h, r) for h in range(heads) for r in range(halves)]
    qs = [q_ref[0, r * ATTN_ROWS:(r + 1) * ATTN_ROWS, h * HEAD_PAD:(h + 1) * HEAD_PAD]
          for h, r in units]

    def kv_rows(j):
        return pl.ds(pl.multiple_of(j * tq, tq), tq)

    def scores(j):
        out = []
        for h in range(heads):
            k_blk = k_ref[0, kv_rows(j), h * HEAD_PAD:(h + 1) * HEAD_PAD]
            for r in range(halves):
                out.append(lax.dot_general(qs[h * halves + r], k_blk, (((1,), (1,)), ((), ())),
                                           preferred_element_type=F32))
        return tuple(out)

    def update(m, acc, s, v_blk):
        m_new = jnp.maximum(m, jnp.max(s, axis=-1, keepdims=True))
        p = jnp.exp2(s - m_new)
        alpha = jnp.exp2(m - m_new)
        return m_new, alpha * acc + _dot(p.astype(BF16), v_blk)

    def update_all(ms, accs, ss, j):
        new_m, new_acc = [], []
        for h in range(heads):
            v_blk = v_ref[0, kv_rows(j), h * HEAD_PAD:(h + 1) * HEAD_PAD]
            for r in range(halves):
                u = h * halves + r
                m, acc = update(ms[u], accs[u], ss[u], v_blk)
                new_m.append(m)
                new_acc.append(acc)
        return tuple(new_m), tuple(new_acc)

    def body(j, carry):
        ms, accs, s_cur = carry
        s_next = scores(j + 1)
        ms, accs = update_all(ms, accs, s_cur, j)
        return ms, accs, s_next

    init = (tuple(jnp.full((ATTN_ROWS, 1), -1e30, F32) for _ in units),
            tuple(jnp.zeros((ATTN_ROWS, HEAD_PAD), F32) for _ in units),
            scores(0))
    ms, accs, s_diag = lax.fori_loop(0, qi, body, init)

    row = lax.broadcasted_iota(jnp.int32, (ATTN_ROWS, tq), 0)
    col = lax.broadcasted_iota(jnp.int32, (ATTN_ROWS, tq), 1)
    s_diag = tuple(jnp.where(col <= row + r * ATTN_ROWS, s_diag[u], -1e30)
                   for u, (h, r) in enumerate(units))
    _, accs = update_all(ms, accs, s_diag, qi)

    outs = [acc / acc[:, MLA_V:MLA_V + 1] for acc in accs]
    lane = lax.broadcasted_iota(jnp.int32, (ATTN_ROWS, HEAD_PAD), 1)
    for pair in range(heads // 2):
        for r in range(halves):
            o_pair = jnp.where(lane < MLA_V, outs[2 * pair * halves + r],
                               pltpu.roll(outs[(2 * pair + 1) * halves + r], MLA_V, axis=1))
            o_ref[0, r * ATTN_ROWS:(r + 1) * ATTN_ROWS,
                  pair * HEAD_PAD:(pair + 1) * HEAD_PAD] = o_pair.astype(o_ref.dtype)


def _mla_attn_call(q, k, v, tq, heads_per_step=2):
    bsz, seqlen, _ = q.shape
    width = heads_per_step * HEAD_PAD
    n_hsteps = MLA_HEADS // heads_per_step
    out_w = heads_per_step * MLA_V
    return pl.pallas_call(
        functools.partial(_mla_attn_kernel, tq=tq),
        grid=(bsz, n_hsteps, seqlen // tq),
        in_specs=[pl.BlockSpec((1, tq, width), lambda b, h, i: (b, i, h)),
                  pl.BlockSpec((1, seqlen, width), lambda b, h, i: (b, 0, h)),
                  pl.BlockSpec((1, seqlen, width), lambda b, h, i: (b, 0, h))],
        out_specs=pl.BlockSpec((1, tq, out_w), lambda b, h, i: (b, i, h)),
        out_shape=jax.ShapeDtypeStruct((bsz, seqlen, MLA_WIDTH), BF16),
        compiler_params=_params(("parallel", "parallel", "arbitrary")),
        name="mla_attention",
    )(q, k, v)


def _layernorm(z, g, b):
    mu = jnp.mean(z, axis=-1, keepdims=True)
    zc = z - mu
    var = jnp.mean(zc * zc, axis=-1, keepdims=True)
    return zc * lax.rsqrt(var + 1e-5) * g + b


def _ab_out_kernel(h_ref, ya_ref, gate_ref, x_ref, w_ref, g_ref, b_ref, o_ref):
    gate = gate_ref[0].astype(F32)
    y_rnn = (h_ref[0] * gate[:, :RNN_WIDTH]).astype(BF16)
    y_mla = (ya_ref[0].astype(F32) * gate[:, RNN_WIDTH:]).astype(BF16)
    y = _dot(y_rnn, w_ref[:RNN_WIDTH, :]) + _dot(y_mla, w_ref[RNN_WIDTH:, :])
    o_ref[0] = _layernorm(DN_ALPHA * x_ref[0] + y, g_ref[...], b_ref[...])


def _ab_out_call(h, y_mla, gate, x, w_out, ln_g, ln_b, tm):
    bsz, seqlen, _ = x.shape
    tok = lambda width: pl.BlockSpec((1, tm, width), lambda b, i: (b, i, 0))
    return pl.pallas_call(
        _ab_out_kernel,
        grid=(bsz, seqlen // tm),
        in_specs=[tok(RNN_WIDTH), tok(MLA_WIDTH), tok(AB_WIDTH), tok(D_MODEL),
                  _const_spec((AB_WIDTH, D_MODEL)), _const_spec((1, D_MODEL)),
                  _const_spec((1, D_MODEL))],
        out_specs=tok(D_MODEL),
        out_shape=jax.ShapeDtypeStruct((bsz, seqlen, D_MODEL), F32),
        compiler_params=_params(("parallel", "parallel")),
        name="ab_out_proj",
    )(h, y_mla, gate, x, w_out, ln_g, ln_b)


def _ssd_in_kernel(x_ref, wz_ref, wx_ref, wdt_ref, z_ref, xbc_ref, dt_ref):
    xb = x_ref[0].astype(BF16)
    z_ref[0] = _silu(_dot(xb, wz_ref[...])).astype(BF16)
    xbc_ref[0] = _dot(xb, wx_ref[...]).astype(BF16)
    dt_ref[0] = _dot(xb, wdt_ref[...])


def _ssd_in_call(x, wz, wx, wdt, tm):
    bsz, seqlen, _ = x.shape
    tok = lambda width: pl.BlockSpec((1, tm, width), lambda b, i: (b, i, 0))
    return pl.pallas_call(
        _ssd_in_kernel,
        grid=(bsz, seqlen // tm),
        in_specs=[tok(D_MODEL), _const_spec((D_MODEL, SSD_INNER)),
                  _const_spec((D_MODEL, SSD_CONV_DIM)), _const_spec((D_MODEL, LANES))],
        out_specs=[tok(SSD_INNER), tok(SSD_CONV_DIM), tok(LANES)],
        out_shape=[jax.ShapeDtypeStruct((bsz, seqlen, SSD_INNER), BF16),
                   jax.ShapeDtypeStruct((bsz, seqlen, SSD_CONV_DIM), BF16),
                   jax.ShapeDtypeStruct((bsz, seqlen, LANES), F32)],
        compiler_params=_params(("parallel", "parallel")),
        name="ssd_in_proj",
    )(x, wz, wx, wdt)


def _cumsum_rows(x):
    n = x.shape[0]
    row = lax.broadcasted_iota(jnp.int32, x.shape, 0)
    shift = 1
    while shift < n:
        x = x + jnp.where(row >= shift, pltpu.roll(x, shift, axis=0), 0.0)
        shift *= 2
    return x


def _ssd_chunk(xbc, dt_raw, state_ref, dt_bias, a_neg, d_skip_x):
    L = SSD_CHUNK
    gw = SSD_GROUPS * SSD_STATE
    xs = xbc[:, :SSD_INNER]
    xs_b = xs.astype(BF16)
    bm = xbc[:, SSD_INNER:SSD_INNER + gw]
    cm_b = xbc[:, SSD_INNER + gw:].astype(BF16)

    dt = _softplus(dt_raw + dt_bias)
    cs = _cumsum_rows(dt * a_neg)
    cs_t = cs.T
    dt_t = dt.T
    w_t = dt_t * jnp.exp(cs_t[:, L - 1:L] - cs_t)

    row = lax.broadcasted_iota(jnp.int32, (L, L), 0)
    col = lax.broadcasted_iota(jnp.int32, (L, L), 1)
    tril = col <= row
    lane = lax.broadcasted_iota(jnp.int32, (L, LANES), 1)
    low = lane < SSD_HEAD_DIM

    hpg = SSD_HEADS // SSD_GROUPS
    y_parts = []
    for g in range(SSD_GROUPS):
        b_g = bm[:, g * SSD_STATE:(g + 1) * SSD_STATE]
        c_g = cm_b[:, g * SSD_STATE:(g + 1) * SSD_STATE]
        cb = lax.dot_general(c_g, b_g.astype(BF16), (((1,), (1,)), ((), ())),
                             preferred_element_type=F32)
        cb = jnp.where(tril, cb, 0.0)
        b_t = b_g.T
        glanes = slice(g * SSD_GROUP_WIDTH, (g + 1) * SSD_GROUP_WIDTH)
        y_off = _dot(c_g, state_ref[:, glanes].astype(BF16))
        for pair in range(hpg // 2):
            h0 = g * hpg + 2 * pair
            plane = slice(h0 * SSD_HEAD_DIM, (h0 + 2) * SSD_HEAD_DIM)
            lhs_rows = []
            decay_out = []
            for h in (h0, h0 + 1):
                seg = cs[:, h:h + 1] - cs_t[h:h + 1, :]
                m_h = jnp.exp(jnp.minimum(seg, 0.0)) * cb * dt_t[h:h + 1, :]
                bw_h = b_t * w_t[h:h + 1, :]
                lhs_rows.append((m_h.astype(BF16), bw_h.astype(BF16)))
                decay_out.append(jnp.exp(jnp.broadcast_to(cs[:, h:h + 1], (L, LANES))))
            lhs = jnp.concatenate(
                [jnp.concatenate([lhs_rows[0][0], lhs_rows[1][0]], axis=1),
                 jnp.concatenate([lhs_rows[0][1], lhs_rows[1][1]], axis=1)], axis=0)
            x_pair = xs_b[:, plane]
            zero = jnp.zeros_like(x_pair)
            rhs = jnp.concatenate([jnp.where(low, x_pair, zero),
                                   jnp.where(low, zero, x_pair)], axis=0)
            res = _dot(lhs, rhs)
            dec = jnp.where(low, decay_out[0], decay_out[1])
            off = pair * 2 * SSD_HEAD_DIM
            y_pair = res[:L] + dec * y_off[:, off:off + 2 * SSD_HEAD_DIM]
            y_parts.append(y_pair)
            state_ref[:, plane] = state_ref[:, plane] * dec[L - 1:L, :] + res[L:]
    y = jnp.concatenate(y_parts, axis=1)
    return y + d_skip_x * xs


def _ssd_kernel(z_ref, xbc_ref, dt_ref, x_ref, conv_w_ref, conv_b_ref, dt_bias_ref, a_log_ref,
                d_skip_ref, norm_ref, w_out_ref, g_ref, b_ref, o_ref, state_ref, tail_ref,
                *, n_chunks):
    L = SSD_CHUNK

    @pl.when(pl.program_id(1) == 0)
    def _():
        state_ref[...] = jnp.zeros_like(state_ref)
        tail_ref[...] = jnp.zeros_like(tail_ref)

    a_neg = -jnp.exp(a_log_ref[...])
    for c in range(n_chunks):
        rows = slice(c * L, (c + 1) * L)
        x_blk = xbc_ref[0, rows, :].astype(F32)
        x_ext = jnp.concatenate([tail_ref[...], x_blk], axis=0)
        tail_ref[...] = x_blk[L - 8:]
        conv = conv_b_ref[...]
        for k in range(CONV_WIDTH):
            off = 8 - (CONV_WIDTH - 1) + k
            conv = conv + x_ext[off:off + L] * conv_w_ref[k:k + 1, :]
        xbc = _silu(conv)
        y = _ssd_chunk(xbc, dt_ref[0, rows, :], state_ref, dt_bias_ref[...], a_neg,
                       d_skip_ref[...])
        y = y * z_ref[0, rows, :].astype(F32)
        parts = []
        for g in range(SSD_GROUPS):
            yg = y[:, g * SSD_GROUP_WIDTH:(g + 1) * SSD_GROUP_WIDTH]
            parts.append(yg * lax.rsqrt(jnp.mean(yg * yg, axis=-1, keepdims=True) + 1e-6))
        yn = (jnp.concatenate(parts, axis=1) * norm_ref[...]).astype(BF16)
        out = _dot(yn, w_out_ref[...])
        o_ref[0, rows, :] = _layernorm(DN_ALPHA * x_ref[0, rows, :] + out, g_ref[...], b_ref[...])


def _ssd_call(z, xbc, dt, x, conv_w, conv_b, dt_bias, a_log, d_skip_x, norm_w, w_out, ln_g, ln_b,
              n_chunks):
    bsz, seqlen, _ = x.shape
    tm = n_chunks * SSD_CHUNK
    tok = lambda width: pl.BlockSpec((1, tm, width), lambda b, i: (b, i, 0))
    return pl.pallas_call(
        functools.partial(_ssd_kernel, n_chunks=n_chunks),
        grid=(bsz, seqlen // tm),
        in_specs=[tok(SSD_INNER), tok(SSD_CONV_DIM), tok(LANES), tok(D_MODEL),
                  _const_spec((CONV_WIDTH, SSD_CONV_DIM)), _const_spec((1, SSD_CONV_DIM)),
                  _const_spec((1, LANES)), _const_spec((1, LANES)), _const_spec((1, SSD_INNER)),
                  _const_spec((1, SSD_INNER)), _const_spec((SSD_INNER, D_MODEL)),
                  _const_spec((1, D_MODEL)), _const_spec((1, D_MODEL))],
        out_specs=tok(D_MODEL),
        out_shape=jax.ShapeDtypeStruct((bsz, seqlen, D_MODEL), F32),
        scratch_shapes=[pltpu.VMEM((SSD_STATE, SSD_INNER), F32),
                        pltpu.VMEM((8, SSD_CONV_DIM), F32)],
        compiler_params=_params(("parallel", "arbitrary")),
        name="ssd_scan_out",
    )(z, xbc, dt, x, conv_w, conv_b, dt_bias, a_log, d_skip_x, norm_w, w_out, ln_g, ln_b)


def _rot_cols(w):
    half = MLA_ROPE // 2
    return jnp.concatenate([-w[..., half:], w[..., :half]], axis=-1)


def _pad_heads(w_heads):
    r, h, c = w_heads.shape
    return jnp.pad(w_heads, ((0, 0), (0, 0), (0, HEAD_PAD - c))).reshape(r, h * HEAD_PAD)


def _block_diag(w):
    h, d, _ = w.shape
    eye = jnp.eye(h, dtype=w.dtype)
    return (eye[:, None, :, None] * w[:, :, None, :]).reshape(h * d, h * d)


def _layer0(x, pos_f, freq, w_in, conv_w, conv_b, gate_a_w, gate_a_b, gate_x_w, gate_x_b, lam,
            q_norm, kv_norm, w_uq, w_ukv, w_out, ln_g, ln_b, tm, ts, tq):
    o_kr = RNN_WIDTH + AB_WIDTH + MLA_Q_RANK + MLA_KV_RANK
    w_kr = w_in[:, o_kr:]
    pad_kr = lambda w: jnp.pad(w, ((0, 0), (MLA_NOPE, HEAD_PAD - MLA_NOPE - MLA_ROPE)))
    w_in_ext = jnp.concatenate([w_in[:, :o_kr], pad_kr(w_kr), pad_kr(_rot_cols(w_kr))],
                               axis=1).astype(BF16)

    uq = w_uq.reshape(MLA_Q_RANK, MLA_HEADS, MLA_NOPE + MLA_ROPE)
    uq_rot = jnp.concatenate([jnp.zeros_like(uq[..., :MLA_NOPE]), _rot_cols(uq[..., MLA_NOPE:])],
                             axis=-1)
    wq = _pad_heads(uq).astype(BF16)
    wqr = _pad_heads(uq_rot).astype(BF16)
    ukv = w_ukv.reshape(MLA_KV_RANK, MLA_HEADS, MLA_NOPE + MLA_V)
    wk = _pad_heads(ukv[..., :MLA_NOPE]).astype(BF16)
    wv = _pad_heads(ukv[..., MLA_NOPE:]).astype(BF16)
    vones = jnp.tile((jnp.arange(HEAD_PAD) >= MLA_V).astype(F32), MLA_HEADS)[None]

    xr, gate, q, k, v = _ab_in_call(x, pos_f, w_in_ext, freq, q_norm[None], kv_norm[None],
                                    wq, wqr, wk, wv, vones, tm)

    w_gate = jnp.concatenate([_block_diag(gate_a_w), _block_diag(gate_x_w)], axis=1).astype(BF16)
    b_gate = jnp.concatenate([gate_a_b, gate_x_b])[None]
    h = _rglru_call(xr, conv_w, conv_b[None], w_gate, b_gate, lam[None], ts)

    y_mla = _mla_attn_call(q, k, v, tq)
    return _ab_out_call(h, y_mla, gate, x, w_out.astype(BF16), ln_g[None], ln_b[None], tm)


def _layer1(x, w_in, conv_w, conv_b, dt_bias, a_log, d_skip, norm_w, w_out, ln_g, ln_b,
            tm, n_chunks):
    wz = w_in[:, :SSD_INNER].astype(BF16)
    wx = w_in[:, SSD_INNER:SSD_INNER + SSD_CONV_DIM].astype(BF16)
    pad_h = lambda a: jnp.pad(a, ((0, 0), (0, LANES - SSD_HEADS)))
    wdt = pad_h(w_in[:, SSD_INNER + SSD_CONV_DIM:]).astype(BF16)
    z, xbc, dt = _ssd_in_call(x, wz, wx, wdt, tm)
    d_skip_x = jnp.repeat(d_skip, SSD_HEAD_DIM)[None]
    return _ssd_call(z, xbc, dt, x, conv_w, conv_b[None], pad_h(dt_bias[None]), pad_h(a_log[None]),
                     d_skip_x, norm_w[None], w_out.astype(BF16), ln_g[None], ln_b[None], n_chunks)


def kernel(x, positions, ab_w_in, ab_conv_w, ab_conv_b, ab_gate_a_w, ab_gate_a_b, ab_gate_x_w,
           ab_gate_x_b, ab_lambda, mla_q_norm, mla_kv_norm, mla_w_uq, mla_w_ukv, ab_w_out,
           ab_ln_g, ab_ln_b, ssd_w_in, ssd_conv_w, ssd_conv_b, ssd_dt_bias, ssd_a_log, ssd_d,
           ssd_norm, ssd_w_out, ssd_ln_g, ssd_ln_b):
    seqlen = x.shape[1]
    tm = min(512, seqlen)
    ts = min(128, seqlen)
    tq = min(512, seqlen)
    n_chunks = min(2, seqlen // SSD_CHUNK)

    inv_freq = ROPE_THETA ** (-jnp.arange(0, MLA_ROPE, 2, dtype=F32) / MLA_ROPE)
    freq = jnp.concatenate([jnp.zeros((MLA_NOPE,), F32), inv_freq, inv_freq,
                            jnp.zeros((HEAD_PAD - MLA_NOPE - MLA_ROPE,), F32)])[None]
    pos_f = positions.astype(F32)[..., None]

    for layer in range(DEPTH):
        j = layer // 2
        if layer % 2 == 0:
            x = _layer0(x, pos_f, freq, ab_w_in[j], ab_conv_w[j], ab_conv_b[j], ab_gate_a_w[j],
                        ab_gate_a_b[j], ab_gate_x_w[j], ab_gate_x_b[j], ab_lambda[j],
                        mla_q_norm[j], mla_kv_norm[j], mla_w_uq[j], mla_w_ukv[j], ab_w_out[j],
                        ab_ln_g[j], ab_ln_b[j], tm, ts, tq)
        else:
            x = _layer1(x, ssd_w_in[j], ssd_conv_w[j], ssd_conv_b[j], ssd_dt_bias[j],
                        ssd_a_log[j], ssd_d[j], ssd_norm[j], ssd_w_out[j], ssd_ln_g[j],
                        ssd_ln_b[j], tm, n_chunks)
    return x
```

```python
import functools
import math

import jax
import jax.numpy as jnp
from jax import lax
from jax.experimental import pallas as pl
from jax.experimental.pallas import tpu as pltpu

D_MODEL = 1024
DEPTH = 2
DN_ALPHA = (2.0 * DEPTH) ** 0.25

RNN_WIDTH = 512
RNN_HEADS = 8
RNN_HEAD_DIM = RNN_WIDTH // RNN_HEADS
CONV_WIDTH = 4
RG_C = 8.0

MLA_HEADS = 8
MLA_NOPE = 64
MLA_ROPE = 32
MLA_V = 64
MLA_Q_RANK = 256
MLA_KV_RANK = 128
MLA_WIDTH = MLA_HEADS * MLA_V
ROPE_THETA = 10000.0
AB_WIDTH = RNN_WIDTH + MLA_WIDTH

SSD_INNER = 2048
SSD_HEAD_DIM = 64
SSD_HEADS = 32
SSD_GROUPS = 4
SSD_STATE = 128
SSD_CHUNK = 128
SSD_CONV_DIM = SSD_INNER + 2 * SSD_GROUPS * SSD_STATE
SSD_GROUP_WIDTH = SSD_INNER // SSD_GROUPS

LANES = 128
HEAD_PAD = LANES
QK_PAD_WIDTH = MLA_HEADS * HEAD_PAD
VMEM_LIMIT = 56 * 1024 * 1024

BF16 = jnp.bfloat16
F32 = jnp.float32


def _sigmoid(x):
    return 1.0 / (1.0 + jnp.exp(-x))


def _silu(x):
    return x * _sigmoid(x)


def _log1p(y):
    u = 1.0 + y
    return jnp.where(u == 1.0, y, jnp.log(u) * (y / (u - 1.0)))


def _softplus(x):
    return jnp.maximum(x, 0.0) + _log1p(jnp.exp(-jnp.abs(x)))


def _dot(a, b):
    return jnp.dot(a, b, preferred_element_type=F32)


def _const_spec(shape):
    zeros = (0,) * len(shape)
    return pl.BlockSpec(shape, lambda *_: zeros)


def _params(semantics):
    return pltpu.CompilerParams(dimension_semantics=semantics,
                                vmem_limit_bytes=VMEM_LIMIT)


def _ab_in_kernel(x_ref, pos_ref, w_in_ref, freq_ref, qn_ref, kvn_ref, wq_ref, wqr_ref,
                  wk_ref, wv_ref, vones_ref,
                  xr_ref, gate_ref, q_ref, k_ref, v_ref):
    xb = x_ref[0].astype(BF16)
    proj = _dot(xb, w_in_ref[...])
    o_gate = RNN_WIDTH
    o_cq = o_gate + AB_WIDTH
    o_ckv = o_cq + MLA_Q_RANK
    o_kr = o_ckv + MLA_KV_RANK
    o_krr = o_kr + HEAD_PAD

    xr_ref[0] = proj[:, :o_gate]
    gate_ref[0] = _silu(proj[:, o_gate:o_cq]).astype(BF16)

    ang = pos_ref[0] * freq_ref[...]
    cos = jnp.cos(ang)
    sin = jnp.sin(ang)
    cos_h = jnp.concatenate([cos] * MLA_HEADS, axis=1)
    sin_h = jnp.concatenate([sin] * MLA_HEADS, axis=1)

    c_q = proj[:, o_cq:o_ckv]
    c_q = c_q * lax.rsqrt(jnp.mean(c_q * c_q, axis=-1, keepdims=True) + 1e-6) * qn_ref[...]
    c_qb = c_q.astype(BF16)
    q = _dot(c_qb, wq_ref[...]) * cos_h + _dot(c_qb, wqr_ref[...]) * sin_h
    scale = (MLA_NOPE + MLA_ROPE) ** -0.5 * math.log2(math.e)
    q_ref[0] = (q * scale).astype(BF16)

    c_kv = proj[:, o_ckv:o_kr]
    c_kv = c_kv * lax.rsqrt(jnp.mean(c_kv * c_kv, axis=-1, keepdims=True) + 1e-6) * kvn_ref[...]
    c_kvb = c_kv.astype(BF16)
    k_rope = proj[:, o_kr:o_krr] * cos + proj[:, o_krr:o_krr + HEAD_PAD] * sin
    k = _dot(c_kvb, wk_ref[...]) + jnp.concatenate([k_rope] * MLA_HEADS, axis=1)
    k_ref[0] = k.astype(BF16)
    v = _dot(c_kvb, wv_ref[...]) + vones_ref[...]
    v_ref[0] = v.astype(BF16)


def _ab_in_call(x, pos_f, w_in_ext, freq, q_norm, kv_norm, wq, wqr, wk, wv, vones, tm):
    bsz, seqlen, _ = x.shape
    n_in = w_in_ext.shape[1]
    grid = (bsz, seqlen // tm)
    tok = lambda width: pl.BlockSpec((1, tm, width), lambda b, i: (b, i, 0))
    return pl.pallas_call(
        _ab_in_kernel,
        grid=grid,
        in_specs=[tok(D_MODEL), tok(1), _const_spec((D_MODEL, n_in)), _const_spec((1, HEAD_PAD)),
                  _const_spec((1, MLA_Q_RANK)), _const_spec((1, MLA_KV_RANK)),
                  _const_spec((MLA_Q_RANK, QK_PAD_WIDTH)), _const_spec((MLA_Q_RANK, QK_PAD_WIDTH)),
                  _const_spec((MLA_KV_RANK, QK_PAD_WIDTH)), _const_spec((MLA_KV_RANK, QK_PAD_WIDTH)),
                  _const_spec((1, QK_PAD_WIDTH))],
        out_specs=[tok(RNN_WIDTH), tok(AB_WIDTH), tok(QK_PAD_WIDTH), tok(QK_PAD_WIDTH),
                   tok(QK_PAD_WIDTH)],
        out_shape=[jax.ShapeDtypeStruct((bsz, seqlen, RNN_WIDTH), F32),
                   jax.ShapeDtypeStruct((bsz, seqlen, AB_WIDTH), BF16),
                   jax.ShapeDtypeStruct((bsz, seqlen, QK_PAD_WIDTH), BF16),
                   jax.ShapeDtypeStruct((bsz, seqlen, QK_PAD_WIDTH), BF16),
                   jax.ShapeDtypeStruct((bsz, seqlen, QK_PAD_WIDTH), BF16)],
        compiler_params=_params(("parallel", "parallel")),
        name="ab_in_proj",
    )(x, pos_f, w_in_ext, freq, q_norm, kv_norm, wq, wqr, wk, wv, vones)


def _rglru_kernel(xr_ref, conv_w_ref, conv_b_ref, w_gate_ref, b_gate_ref, lam_ref,
                  h_ref, tail_ref, carry_ref, a_ref, u_ref):
    bsz, ts, width = xr_ref.shape

    @pl.when(pl.program_id(0) == 0)
    def _():
        tail_ref[...] = jnp.zeros_like(tail_ref)
        carry_ref[...] = jnp.zeros_like(carry_ref)

    x_blk = pltpu.einshape("btc->tbc", xr_ref[...])
    x_ext = jnp.concatenate([tail_ref[...], x_blk], axis=0)
    tail_ref[...] = x_blk[ts - (CONV_WIDTH - 1):]
    xc = conv_b_ref[...][None]
    for k in range(CONV_WIDTH):
        xc = xc + x_ext[k:k + ts] * conv_w_ref[k:k + 1, :][None]

    xc2 = xc.reshape(ts * bsz, width)
    gates = _dot(xc2.astype(BF16), w_gate_ref[...]) + b_gate_ref[...]
    r = _sigmoid(gates[:, :width])
    i = _sigmoid(gates[:, width:])
    log_a = (-RG_C * r) * _softplus(-lam_ref[...])
    a = jnp.exp(log_a)
    mult = jnp.sqrt(-jnp.tanh(log_a) * (a * a + 1.0))
    u = mult * (i * xc2)
    a_ref[...] = a.reshape(ts, bsz, width)
    u_ref[...] = u.reshape(ts, bsz, width)

    def step(t, h):
        h = a_ref[t] * h + u_ref[t]
        u_ref[t] = h
        return h

    carry_ref[...] = lax.fori_loop(0, ts, step, carry_ref[...], unroll=8)
    h_ref[...] = pltpu.einshape("tbc->btc", u_ref[...])


def _rglru_call(xr, conv_w, conv_b, w_gate, b_gate, lam, ts):
    bsz, seqlen, width = xr.shape
    blk = pl.BlockSpec((bsz, ts, width), lambda i: (0, i, 0))
    return pl.pallas_call(
        _rglru_kernel,
        grid=(seqlen // ts,),
        in_specs=[blk, _const_spec((CONV_WIDTH, width)), _const_spec((1, width)),
                  _const_spec((width, 2 * width)), _const_spec((1, 2 * width)),
                  _const_spec((1, width))],
        out_specs=blk,
        out_shape=jax.ShapeDtypeStruct((bsz, seqlen, width), F32),
        scratch_shapes=[pltpu.VMEM((CONV_WIDTH - 1, bsz, width), F32),
                        pltpu.VMEM((bsz, width), F32),
                        pltpu.VMEM((ts, bsz, width), F32),
                        pltpu.VMEM((ts, bsz, width), F32)],
        compiler_params=_params(("arbitrary",)),
        name="rglru_scan",
    )(xr, conv_w, conv_b, w_gate, b_gate, lam)


ATTN_ROWS = 256


def _mla_attn_kernel(q_ref, k_ref, v_ref, o_ref, s0_ref, s1_ref, m_ref, acc_ref, *, tq):
    qi = pl.program_id(2)
    heads = q_ref.shape[2] // HEAD_PAD
    halves = tq // ATTN_ROWS
    units = [(h, r) for h in range(heads) for r in range(halves)]

    def kv_rows(j):
        return pl.ds(pl.multiple_of(j * tq, tq), tq)

    def scores_into(dst_ref, j):
        for h in range(heads):
            k_blk = k_ref[0, kv_rows(j), h * HEAD_PAD:(h + 1) * HEAD_PAD]
            for r in range(halves):
                q_u = q_ref[0, r * ATTN_ROWS:(r + 1) * ATTN_ROWS, h * HEAD_PAD:(h + 1) * HEAD_PAD]
                dst_ref[h * halves + r] = lax.dot_general(
                    q_u, k_blk, (((1,), (1,)), ((), ())), preferred_element_type=F32)

    def update_from(src_ref, j, diagonal):
        if diagonal:
            row = lax.broadcasted_iota(jnp.int32, (ATTN_ROWS, tq), 0)
            col = lax.broadcasted_iota(jnp.int32, (ATTN_ROWS, tq), 1)
        for h in range(heads):
            v_blk = v_ref[0, kv_rows(j), h * HEAD_PAD:(h + 1) * HEAD_PAD]
            for r in range(halves):
                u = h * halves + r
                if diagonal:
                    src_ref[u] = jnp.where(col <= row + r * ATTN_ROWS, src_ref[u], -1e30)
                m = m_ref[u]
                m_new = jnp.maximum(m, jnp.max(src_ref[u], axis=-1, keepdims=True))
                p = jnp.exp2(src_ref[u] - m_new)
                acc_ref[u] = jnp.exp2(m - m_new) * acc_ref[u] + _dot(p.astype(BF16), v_blk)
                m_ref[u] = m_new

    def finish(src_ref):
        update_from(src_ref, qi, True)
        lane = lax.broadcasted_iota(jnp.int32, (ATTN_ROWS, HEAD_PAD), 1)
        for pair in range(heads // 2):
            for r in range(halves):
                a0 = acc_ref[2 * pair * halves + r]
                a1 = acc_ref[(2 * pair + 1) * halves + r]
                o0 = a0 / a0[:, MLA_V:MLA_V + 1]
                o1 = a1 / a1[:, MLA_V:MLA_V + 1]
                o_pair = jnp.where(lane < MLA_V, o0, pltpu.roll(o1, MLA_V, axis=1))
                o_ref[0, r * ATTN_ROWS:(r + 1) * ATTN_ROWS,
                      pair * HEAD_PAD:(pair + 1) * HEAD_PAD] = o_pair.astype(o_ref.dtype)

    m_ref[...] = jnp.full(m_ref.shape, -1e30, F32)
    acc_ref[...] = jnp.zeros(acc_ref.shape, F32)
    scores_into(s0_ref, 0)

    def pair_body(t, carry):
        j = 2 * t
        scores_into(s1_ref, j + 1)
        update_from(s0_ref, j, False)
        scores_into(s0_ref, j + 2)
        update_from(s1_ref, j + 1, False)
        return carry

    lax.fori_loop(0, qi // 2, pair_body, 0)

    @pl.when(qi % 2 == 0)
    def _():
        finish(s0_ref)

    @pl.when(qi % 2 == 1)
    def _():
        scores_into(s1_ref, qi)
        update_from(s0_ref, qi - 1, False)
        finish(s1_ref)


def _mla_attn_call(q, k, v, tq, heads_per_step=2):
    bsz, seqlen, _ = q.shape
    width = heads_per_step * HEAD_PAD
    n_hsteps = MLA_HEADS // heads_per_step
    out_w = heads_per_step * MLA_V
    n_units = heads_per_step * (tq // ATTN_ROWS)
    return pl.pallas_call(
        functools.partial(_mla_attn_kernel, tq=tq),
        grid=(bsz, n_hsteps, seqlen // tq),
        in_specs=[pl.BlockSpec((1, tq, width), lambda b, h, i: (b, i, h)),
                  pl.BlockSpec((1, seqlen, width), lambda b, h, i: (b, 0, h)),
                  pl.BlockSpec((1, seqlen, width), lambda b, h, i: (b, 0, h))],
        out_specs=pl.BlockSpec((1, tq, out_w), lambda b, h, i: (b, i, h)),
        out_shape=jax.ShapeDtypeStruct((bsz, seqlen, MLA_WIDTH), BF16),
        scratch_shapes=[pltpu.VMEM((n_units, ATTN_ROWS, tq), F32),
                        pltpu.VMEM((n_units, ATTN_ROWS, tq), F32),
                        pltpu.VMEM((n_units, ATTN_ROWS, 1), F32),
                        pltpu.VMEM((n_units, ATTN_ROWS, HEAD_PAD), F32)],
        compiler_params=_params(("parallel", "parallel", "arbitrary")),
        name="mla_attention",
    )(q, k, v)


def _layernorm(z, g, b):
    mu = jnp.mean(z, axis=-1, keepdims=True)
    zc = z - mu
    var = jnp.mean(zc * zc, axis=-1, keepdims=True)
    return zc * lax.rsqrt(var + 1e-5) * g + b


def _ab_out_kernel(h_ref, ya_ref, gate_ref, x_ref, w_ref, g_ref, b_ref, o_ref):
    gate = gate_ref[0].astype(F32)
    y_rnn = (h_ref[0] * gate[:, :RNN_WIDTH]).astype(BF16)
    y_mla = (ya_ref[0].astype(F32) * gate[:, RNN_WIDTH:]).astype(BF16)
    y = _dot(y_rnn, w_ref[:RNN_WIDTH, :]) + _dot(y_mla, w_ref[RNN_WIDTH:, :])
    o_ref[0] = _layernorm(DN_ALPHA * x_ref[0] + y, g_ref[...], b_ref[...])


def _ab_out_call(h, y_mla, gate, x, w_out, ln_g, ln_b, tm):
    bsz, seqlen, _ = x.shape
    tok = lambda width: pl.BlockSpec((1, tm, width), lambda b, i: (b, i, 0))
    return pl.pallas_call(
        _ab_out_kernel,
        grid=(bsz, seqlen // tm),
        in_specs=[tok(RNN_WIDTH), tok(MLA_WIDTH), tok(AB_WIDTH), tok(D_MODEL),
                  _const_spec((AB_WIDTH, D_MODEL)), _const_spec((1, D_MODEL)),
                  _const_spec((1, D_MODEL))],
        out_specs=tok(D_MODEL),
        out_shape=jax.ShapeDtypeStruct((bsz, seqlen, D_MODEL), F32),
        compiler_params=_params(("parallel", "parallel")),
        name="ab_out_proj",
    )(h, y_mla, gate, x, w_out, ln_g, ln_b)


SSD_IN_COLS = 512
SUBLANES = 8


def _causal_conv_rows(ext, w, b):
    rows, width = ext.shape
    tiles = rows // SUBLANES
    x3 = ext.reshape(tiles, SUBLANES, width)
    first = lax.broadcasted_iota(jnp.int32, x3.shape, 1) == 0

    def shift_down(a):
        r = pltpu.roll(a, 1, axis=1)
        r_prev = jnp.concatenate([r[:1], r[:-1]], axis=0)
        return jnp.where(first, r_prev, r)

    acc = x3 * w[0:1, :][None]
    for k in range(1, CONV_WIDTH):
        acc = shift_down(acc) + x3 * w[k:k + 1, :][None]
    return (acc[1:] + b[None]).reshape(rows - SUBLANES, width)


def _ssd_in_kernel(x_ref, wz_ref, wx_ref, wdt_ref, dt_bias_ref, conv_w_ref, conv_b_ref,
                   z_ref, xbc_ref, dt_ref, tail_ref):
    tm = x_ref.shape[1]

    @pl.when(pl.program_id(1) == 0)
    def _():
        tail_ref[...] = jnp.zeros_like(tail_ref)

    xb = x_ref[0].astype(BF16)
    dt_ref[0] = _softplus(_dot(xb, wdt_ref[...]) + dt_bias_ref[...])
    for c in range(SSD_INNER // SSD_IN_COLS):
        cols = slice(c * SSD_IN_COLS, (c + 1) * SSD_IN_COLS)
        z_ref[0, :, cols] = _silu(_dot(xb, wz_ref[:, cols])).astype(BF16)
    for c in range(SSD_CONV_DIM // SSD_IN_COLS):
        cols = slice(c * SSD_IN_COLS, (c + 1) * SSD_IN_COLS)
        proj = _dot(xb, wx_ref[:, cols])
        ext = jnp.concatenate([tail_ref[:, cols], proj], axis=0)
        tail_ref[:, cols] = proj[tm - 8:]
        conv = _causal_conv_rows(ext, conv_w_ref[:, cols], conv_b_ref[:, cols])
        xbc_ref[0, :, cols] = _silu(conv).astype(BF16)


def _ssd_in_call(x, wz, wx, wdt, dt_bias, conv_w, conv_b, tm):
    bsz, seqlen, _ = x.shape
    tok = lambda width: pl.BlockSpec((1, tm, width), lambda b, i: (b, i, 0))
    return pl.pallas_call(
        _ssd_in_kernel,
        grid=(bsz, seqlen // tm),
        in_specs=[tok(D_MODEL), _const_spec((D_MODEL, SSD_INNER)),
                  _const_spec((D_MODEL, SSD_CONV_DIM)), _const_spec((D_MODEL, LANES)),
                  _const_spec((1, LANES)), _const_spec((CONV_WIDTH, SSD_CONV_DIM)),
                  _const_spec((1, SSD_CONV_DIM))],
        out_specs=[tok(SSD_INNER), tok(SSD_CONV_DIM), tok(LANES)],
        out_shape=[jax.ShapeDtypeStruct((bsz, seqlen, SSD_INNER), BF16),
                   jax.ShapeDtypeStruct((bsz, seqlen, SSD_CONV_DIM), BF16),
                   jax.ShapeDtypeStruct((bsz, seqlen, LANES), F32)],
        scratch_shapes=[pltpu.VMEM((8, SSD_CONV_DIM), F32)],
        compiler_params=_params(("parallel", "arbitrary")),
        name="ssd_in_proj",
    )(x, wz, wx, wdt, dt_bias, conv_w, conv_b)


def _cumsum_rows(x):
    n = x.shape[0]
    row = lax.broadcasted_iota(jnp.int32, x.shape, 0)
    shift = 1
    while shift < n:
        x = x + jnp.where(row >= shift, pltpu.roll(x, shift, axis=0), 0.0)
        shift *= 2
    return x


LOG2E = math.log2(math.e)


def _ssd_chunk(xbc_ref, rows, dt, state_ref, a_neg2, d_skip_x):
    L = SSD_CHUNK
    gw = SSD_GROUPS * SSD_STATE

    cs = _cumsum_rows(dt * a_neg2)
    cs_t = cs.T
    dt_t = dt.T
    w_t = dt_t * jnp.exp2(cs_t[:, L - 1:L] - cs_t)

    row = lax.broadcasted_iota(jnp.int32, (L, L), 0)
    col = lax.broadcasted_iota(jnp.int32, (L, L), 1)
    tril = col <= row
    lane = lax.broadcasted_iota(jnp.int32, (L, LANES), 1)
    low = lane < SSD_HEAD_DIM
    keep_lo = low.astype(F32).astype(BF16)
    keep_hi = 1.0 - keep_lo

    hpg = SSD_HEADS // SSD_GROUPS
    y_parts = []
    for g in range(SSD_GROUPS):
        b_g = xbc_ref[0, rows, SSD_INNER + g * SSD_STATE:SSD_INNER + (g + 1) * SSD_STATE]
        c_g = xbc_ref[0, rows, SSD_INNER + gw + g * SSD_STATE:SSD_INNER + gw + (g + 1) * SSD_STATE]
        cb = lax.dot_general(c_g, b_g, (((1,), (1,)), ((), ())),
                             preferred_element_type=F32)
        cb = jnp.where(tril, cb, 0.0)
        b_t = b_g.astype(F32).T
        glanes = slice(g * SSD_GROUP_WIDTH, (g + 1) * SSD_GROUP_WIDTH)
        y_off = _dot(c_g, state_ref[:, glanes].astype(BF16))
        for pair in range(hpg // 2):
            h0 = g * hpg + 2 * pair
            plane = slice(h0 * SSD_HEAD_DIM, (h0 + 2) * SSD_HEAD_DIM)
            tops, bots, decays = [], [], []
            for h in (h0, h0 + 1):
                cs_col = jnp.broadcast_to(cs[:, h:h + 1], (L, LANES))
                seg = cs_col - cs_t[h:h + 1, :]
                m_h = jnp.exp2(jnp.minimum(seg, 0.0)) * (cb * dt_t[h:h + 1, :])
                tops.append(m_h.astype(BF16))
                bots.append((b_t * w_t[h:h + 1, :]).astype(BF16))
                decays.append(jnp.exp2(cs_col))
            lhs = jnp.concatenate([jnp.concatenate(tops, axis=1),
                                   jnp.concatenate(bots, axis=1)], axis=0)
            x_pair = xbc_ref[0, rows, plane]
            rhs = jnp.concatenate([x_pair * keep_lo, x_pair * keep_hi], axis=0)
            res = _dot(lhs, rhs)
            dec = jnp.where(low, decays[0], decays[1])
            off = pair * 2 * SSD_HEAD_DIM
            y_pair = (res[:L] + dec * y_off[:, off:off + 2 * SSD_HEAD_DIM]
                      + d_skip_x[:, plane] * x_pair.astype(F32))
            y_parts.append(y_pair)
            state_ref[:, plane] = state_ref[:, plane] * dec[L - 1:L, :] + res[L:]
    return jnp.concatenate(y_parts, axis=1)


def _ssd_kernel(z_ref, xbc_ref, dt_ref, x_ref, a_log_ref, d_skip_ref, norm_ref, w_out_ref,
                g_ref, b_ref, o_ref, state_ref, *, n_chunks):
    L = SSD_CHUNK

    @pl.when(pl.program_id(1) == 0)
    def _():
        state_ref[...] = jnp.zeros_like(state_ref)

    a_neg2 = -jnp.exp(a_log_ref[...]) * LOG2E
    for c in range(n_chunks):
        rows = slice(c * L, (c + 1) * L)
        y = _ssd_chunk(xbc_ref, rows, dt_ref[0, rows, :], state_ref, a_neg2, d_skip_ref[...])
        y = y * z_ref[0, rows, :].astype(F32)
        parts = []
        for g in range(SSD_GROUPS):
            yg = y[:, g * SSD_GROUP_WIDTH:(g + 1) * SSD_GROUP_WIDTH]
            parts.append(yg * lax.rsqrt(jnp.mean(yg * yg, axis=-1, keepdims=True) + 1e-6))
        yn = (jnp.concatenate(parts, axis=1) * norm_ref[...]).astype(BF16)
        out = _dot(yn, w_out_ref[...])
        o_ref[0, rows, :] = _layernorm(DN_ALPHA * x_ref[0, rows, :] + out, g_ref[...], b_ref[...])


def _ssd_call(z, xbc, dt, x, a_log, d_skip_x, norm_w, w_out, ln_g, ln_b, n_chunks):
    bsz, seqlen, _ = x.shape
    tm = n_chunks * SSD_CHUNK
    tok = lambda width: pl.BlockSpec((1, tm, width), lambda b, i: (b, i, 0))
    return pl.pallas_call(
        functools.partial(_ssd_kernel, n_chunks=n_chunks),
        grid=(bsz, seqlen // tm),
        in_specs=[tok(SSD_INNER), tok(SSD_CONV_DIM), tok(LANES), tok(D_MODEL),
                  _const_spec((1, LANES)), _const_spec((1, SSD_INNER)),
                  _const_spec((1, SSD_INNER)), _const_spec((SSD_INNER, D_MODEL)),
                  _const_spec((1, D_MODEL)), _const_spec((1, D_MODEL))],
        out_specs=tok(D_MODEL),
        out_shape=jax.ShapeDtypeStruct((bsz, seqlen, D_MODEL), F32),
        scratch_shapes=[pltpu.VMEM((SSD_STATE, SSD_INNER), F32)],
        compiler_params=_params(("parallel", "arbitrary")),
        name="ssd_scan_out",
    )(z, xbc, dt, x, a_log, d_skip_x, norm_w, w_out, ln_g, ln_b)


def _rot_cols(w):
    half = MLA_ROPE // 2
    return jnp.concatenate([-w[..., half:], w[..., :half]], axis=-1)


def _pad_heads(w_heads):
    r, h, c = w_heads.shape
    return jnp.pad(w_heads, ((0, 0), (0, 0), (0, HEAD_PAD - c))).reshape(r, h * HEAD_PAD)


def _block_diag(w):
    h, d, _ = w.shape
    eye = jnp.eye(h, dtype=w.dtype)
    return (eye[:, None, :, None] * w[:, :, None, :]).reshape(h * d, h * d)


def _layer0(x, pos_f, freq, w_in, conv_w, conv_b, gate_a_w, gate_a_b, gate_x_w, gate_x_b, lam,
            q_norm, kv_norm, w_uq, w_ukv, w_out, ln_g, ln_b, tm, ts, tq):
    o_kr = RNN_WIDTH + AB_WIDTH + MLA_Q_RANK + MLA_KV_RANK
    w_kr = w_in[:, o_kr:]
    pad_kr = lambda w: jnp.pad(w, ((0, 0), (MLA_NOPE, HEAD_PAD - MLA_NOPE - MLA_ROPE)))
    w_in_ext = jnp.concatenate([w_in[:, :o_kr], pad_kr(w_kr), pad_kr(_rot_cols(w_kr))],
                               axis=1).astype(BF16)

    uq = w_uq.reshape(MLA_Q_RANK, MLA_HEADS, MLA_NOPE + MLA_ROPE)
    uq_rot = jnp.concatenate([jnp.zeros_like(uq[..., :MLA_NOPE]), _rot_cols(uq[..., MLA_NOPE:])],
                             axis=-1)
    wq = _pad_heads(uq).astype(BF16)
    wqr = _pad_heads(uq_rot).astype(BF16)
    ukv = w_ukv.reshape(MLA_KV_RANK, MLA_HEADS, MLA_NOPE + MLA_V)
    wk = _pad_heads(ukv[..., :MLA_NOPE]).astype(BF16)
    wv = _pad_heads(ukv[..., MLA_NOPE:]).astype(BF16)
    vones = jnp.tile((jnp.arange(HEAD_PAD) >= MLA_V).astype(F32), MLA_HEADS)[None]

    xr, gate, q, k, v = _ab_in_call(x, pos_f, w_in_ext, freq, q_norm[None], kv_norm[None],
                                    wq, wqr, wk, wv, vones, tm)

    w_gate = jnp.concatenate([_block_diag(gate_a_w), _block_diag(gate_x_w)], axis=1).astype(BF16)
    b_gate = jnp.concatenate([gate_a_b, gate_x_b])[None]
    h = _rglru_call(xr, conv_w, conv_b[None], w_gate, b_gate, lam[None], ts)

    y_mla = _mla_attn_call(q, k, v, tq)
    return _ab_out_call(h, y_mla, gate, x, w_out.astype(BF16), ln_g[None], ln_b[None], tm)


def _layer1(x, w_in, conv_w, conv_b, dt_bias, a_log, d_skip, norm_w, w_out, ln_g, ln_b,
            tm, n_chunks):
    wz = w_in[:, :SSD_INNER].astype(BF16)
    wx = w_in[:, SSD_INNER:SSD_INNER + SSD_CONV_DIM].astype(BF16)
    pad_h = lambda a: jnp.pad(a, ((0, 0), (0, LANES - SSD_HEADS)))
    wdt = pad_h(w_in[:, SSD_INNER + SSD_CONV_DIM:]).astype(BF16)
    z, xbc, dt = _ssd_in_call(x, wz, wx, wdt, pad_h(dt_bias[None]), conv_w, conv_b[None], tm)
    d_skip_x = jnp.repeat(d_skip, SSD_HEAD_DIM)[None]
    return _ssd_call(z, xbc, dt, x, pad_h(a_log[None]), d_skip_x, norm_w[None],
                     w_out.astype(BF16), ln_g[None], ln_b[None], n_chunks)


def kernel(x, positions, ab_w_in, ab_conv_w, ab_conv_b, ab_gate_a_w, ab_gate_a_b, ab_gate_x_w,
           ab_gate_x_b, ab_lambda, mla_q_norm, mla_kv_norm, mla_w_uq, mla_w_ukv, ab_w_out,
           ab_ln_g, ab_ln_b, ssd_w_in, ssd_conv_w, ssd_conv_b, ssd_dt_bias, ssd_a_log, ssd_d,
           ssd_norm, ssd_w_out, ssd_ln_g, ssd_ln_b):
    seqlen = x.shape[1]
    tm = min(512, seqlen)
    ts = min(128, seqlen)
    tq = min(512, seqlen)
    n_chunks = min(2, seqlen // SSD_CHUNK)

    inv_freq = ROPE_THETA ** (-jnp.arange(0, MLA_ROPE, 2, dtype=F32) / MLA_ROPE)
    freq = jnp.concatenate([jnp.zeros((MLA_NOPE,), F32), inv_freq, inv_freq,
                            jnp.zeros((HEAD_PAD - MLA_NOPE - MLA_ROPE,), F32)])[None]
    pos_f = positions.astype(F32)[..., None]

    for layer in range(DEPTH):
        j = layer // 2
        if layer % 2 == 0:
            x = _layer0(x, pos_f, freq, ab_w_in[j], ab_conv_w[j], ab_conv_b[j], ab_gate_a_w[j],
                        ab_gate_a_b[j], ab_gate_x_w[j], ab_gate_x_b[j], ab_lambda[j],
                        mla_q_norm[j], mla_kv_norm[j], mla_w_uq[j], mla_w_ukv[j], ab_w_out[j],
                        ab_ln_g[j], ab_ln_b[j], tm, ts, tq)
        else:
            x = _layer1(x, ssd_w_in[j], ssd_conv_w[j], ssd_conv_b[j], ssd_dt_bias[j],
                        ssd_a_log[j], ssd_d[j], ssd_norm[j], ssd_w_out[j], ssd_ln_g[j],
                        ssd_ln_b[j], tm, n_chunks)
    return x
```

```python
import functools
import math

import jax
import jax.numpy as jnp
from jax import lax
from jax.experimental import pallas as pl
from jax.experimental.pallas import tpu as pltpu

D_MODEL = 1024
DEPTH = 2
DN_ALPHA = (2.0 * DEPTH) ** 0.25

RNN_WIDTH = 512
RNN_HEADS = 8
RNN_HEAD_DIM = RNN_WIDTH // RNN_HEADS
CONV_WIDTH = 4
RG_C = 8.0

MLA_HEADS = 8
MLA_NOPE = 64
MLA_ROPE = 32
MLA_V = 64
MLA_Q_RANK = 256
MLA_KV_RANK = 128
MLA_WIDTH = MLA_HEADS * MLA_V
ROPE_THETA = 10000.0
AB_WIDTH = RNN_WIDTH + MLA_WIDTH

SSD_INNER = 2048
SSD_HEAD_DIM = 64
SSD_HEADS = 32
SSD_GROUPS = 4
SSD_STATE = 128
SSD_CHUNK = 128
SSD_CONV_DIM = SSD_INNER + 2 * SSD_GROUPS * SSD_STATE
SSD_GROUP_WIDTH = SSD_INNER // SSD_GROUPS

LANES = 128
HEAD_PAD = LANES
QK_PAD_WIDTH = MLA_HEADS * HEAD_PAD
VMEM_LIMIT = 56 * 1024 * 1024

BF16 = jnp.bfloat16
F32 = jnp.float32


def _sigmoid(x):
    return 1.0 / (1.0 + jnp.exp(-x))


def _silu(x):
    return x * _sigmoid(x)


def _log1p(y):
    u = 1.0 + y
    return jnp.where(u == 1.0, y, jnp.log(u) * (y / (u - 1.0)))


def _softplus(x):
    return jnp.maximum(x, 0.0) + _log1p(jnp.exp(-jnp.abs(x)))


def _dot(a, b):
    return jnp.dot(a, b, preferred_element_type=F32)


def _const_spec(shape):
    zeros = (0,) * len(shape)
    return pl.BlockSpec(shape, lambda *_: zeros)


def _params(semantics):
    return pltpu.CompilerParams(dimension_semantics=semantics,
                                vmem_limit_bytes=VMEM_LIMIT)


def _ab_in_kernel(x_ref, pos_ref, w_in_ref, freq_ref, qn_ref, kvn_ref, wq_ref, wqr_ref,
                  wk_ref, wv_ref, vones_ref,
                  xr_ref, gate_ref, q_ref, k_ref, v_ref):
    xb = x_ref[0].astype(BF16)
    proj = _dot(xb, w_in_ref[...])
    o_gate = RNN_WIDTH
    o_cq = o_gate + AB_WIDTH
    o_ckv = o_cq + MLA_Q_RANK
    o_kr = o_ckv + MLA_KV_RANK
    o_krr = o_kr + HEAD_PAD

    xr_ref[0] = proj[:, :o_gate]
    gate_ref[0] = _silu(proj[:, o_gate:o_cq]).astype(BF16)

    ang = pos_ref[0] * freq_ref[...]
    cos = jnp.cos(ang)
    sin = jnp.sin(ang)
    cos_h = jnp.concatenate([cos] * MLA_HEADS, axis=1)
    sin_h = jnp.concatenate([sin] * MLA_HEADS, axis=1)

    c_q = proj[:, o_cq:o_ckv]
    c_q = c_q * lax.rsqrt(jnp.mean(c_q * c_q, axis=-1, keepdims=True) + 1e-6) * qn_ref[...]
    c_qb = c_q.astype(BF16)
    q = _dot(c_qb, wq_ref[...]) * cos_h + _dot(c_qb, wqr_ref[...]) * sin_h
    scale = (MLA_NOPE + MLA_ROPE) ** -0.5 * math.log2(math.e)
    q_ref[0] = (q * scale).astype(BF16)

    c_kv = proj[:, o_ckv:o_kr]
    c_kv = c_kv * lax.rsqrt(jnp.mean(c_kv * c_kv, axis=-1, keepdims=True) + 1e-6) * kvn_ref[...]
    c_kvb = c_kv.astype(BF16)
    k_rope = proj[:, o_kr:o_krr] * cos + proj[:, o_krr:o_krr + HEAD_PAD] * sin
    k = _dot(c_kvb, wk_ref[...]) + jnp.concatenate([k_rope] * MLA_HEADS, axis=1)
    k_ref[0] = k.astype(BF16)
    v = _dot(c_kvb, wv_ref[...]) + vones_ref[...]
    v_ref[0] = v.astype(BF16)


def _ab_in_call(x, pos_f, w_in_ext, freq, q_norm, kv_norm, wq, wqr, wk, wv, vones, tm):
    bsz, seqlen, _ = x.shape
    n_in = w_in_ext.shape[1]
    grid = (bsz, seqlen // tm)
    tok = lambda width: pl.BlockSpec((1, tm, width), lambda b, i: (b, i, 0))
    return pl.pallas_call(
        _ab_in_kernel,
        grid=grid,
        in_specs=[tok(D_MODEL), tok(1), _const_spec((D_MODEL, n_in)), _const_spec((1, HEAD_PAD)),
                  _const_spec((1, MLA_Q_RANK)), _const_spec((1, MLA_KV_RANK)),
                  _const_spec((MLA_Q_RANK, QK_PAD_WIDTH)), _const_spec((MLA_Q_RANK, QK_PAD_WIDTH)),
                  _const_spec((MLA_KV_RANK, QK_PAD_WIDTH)), _const_spec((MLA_KV_RANK, QK_PAD_WIDTH)),
                  _const_spec((1, QK_PAD_WIDTH))],
        out_specs=[tok(RNN_WIDTH), tok(AB_WIDTH), tok(QK_PAD_WIDTH), tok(QK_PAD_WIDTH),
                   tok(QK_PAD_WIDTH)],
        out_shape=[jax.ShapeDtypeStruct((bsz, seqlen, RNN_WIDTH), F32),
                   jax.ShapeDtypeStruct((bsz, seqlen, AB_WIDTH), BF16),
                   jax.ShapeDtypeStruct((bsz, seqlen, QK_PAD_WIDTH), BF16),
                   jax.ShapeDtypeStruct((bsz, seqlen, QK_PAD_WIDTH), BF16),
                   jax.ShapeDtypeStruct((bsz, seqlen, QK_PAD_WIDTH), BF16)],
        compiler_params=_params(("parallel", "parallel")),
        name="ab_in_proj",
    )(x, pos_f, w_in_ext, freq, q_norm, kv_norm, wq, wqr, wk, wv, vones)


def _rglru_kernel(xr_ref, conv_w_ref, conv_b_ref, w_gate_ref, b_gate_ref, lam_ref,
                  h_ref, tail_ref, carry_ref, a_ref, u_ref):
    bsz, ts, width = xr_ref.shape

    @pl.when(pl.program_id(0) == 0)
    def _():
        tail_ref[...] = jnp.zeros_like(tail_ref)
        carry_ref[...] = jnp.zeros_like(carry_ref)

    x_blk = pltpu.einshape("btc->tbc", xr_ref[...])
    x_ext = jnp.concatenate([tail_ref[...], x_blk], axis=0)
    tail_ref[...] = x_blk[ts - (CONV_WIDTH - 1):]
    xc = conv_b_ref[...][None]
    for k in range(CONV_WIDTH):
        xc = xc + x_ext[k:k + ts] * conv_w_ref[k:k + 1, :][None]

    xc2 = xc.reshape(ts * bsz, width)
    gates = _dot(xc2.astype(BF16), w_gate_ref[...]) + b_gate_ref[...]
    r = _sigmoid(gates[:, :width])
    i = _sigmoid(gates[:, width:])
    log_a = (-RG_C * r) * _softplus(-lam_ref[...])
    a = jnp.exp(log_a)
    mult = jnp.sqrt(-jnp.tanh(log_a) * (a * a + 1.0))
    u = mult * (i * xc2)
    a_ref[...] = a.reshape(ts, bsz, width)
    u_ref[...] = u.reshape(ts, bsz, width)

    def step(t, h):
        h = a_ref[t] * h + u_ref[t]
        u_ref[t] = h
        return h

    carry_ref[...] = lax.fori_loop(0, ts, step, carry_ref[...], unroll=8)
    h_ref[...] = pltpu.einshape("tbc->btc", u_ref[...])


def _rglru_call(xr, conv_w, conv_b, w_gate, b_gate, lam, ts):
    bsz, seqlen, width = xr.shape
    blk = pl.BlockSpec((bsz, ts, width), lambda i: (0, i, 0))
    return pl.pallas_call(
        _rglru_kernel,
        grid=(seqlen // ts,),
        in_specs=[blk, _const_spec((CONV_WIDTH, width)), _const_spec((1, width)),
                  _const_spec((width, 2 * width)), _const_spec((1, 2 * width)),
                  _const_spec((1, width))],
        out_specs=blk,
        out_shape=jax.ShapeDtypeStruct((bsz, seqlen, width), F32),
        scratch_shapes=[pltpu.VMEM((CONV_WIDTH - 1, bsz, width), F32),
                        pltpu.VMEM((bsz, width), F32),
                        pltpu.VMEM((ts, bsz, width), F32),
                        pltpu.VMEM((ts, bsz, width), F32)],
        compiler_params=_params(("arbitrary",)),
        name="rglru_scan",
    )(xr, conv_w, conv_b, w_gate, b_gate, lam)


ATTN_ROWS = 256


def _mla_attn_tiles(q_ref, k_ref, v_ref, o_ref, s_refs, tiles, tq):
    heads = q_ref.shape[2] // HEAD_PAD
    halves = tq // ATTN_ROWS
    units = [(h, r) for h in range(heads) for r in range(halves)]
    steps = [(qi, j) for qi in tiles for j in range(qi + 1)]

    def width(qi, j, r):
        return (r + 1) * ATTN_ROWS if j == qi else tq

    def scores_into(dst_ref, qi, j):
        for u, (h, r) in enumerate(units):
            w = width(qi, j, r)
            q_u = q_ref[0, qi * tq + r * ATTN_ROWS:qi * tq + (r + 1) * ATTN_ROWS,
                        h * HEAD_PAD:(h + 1) * HEAD_PAD]
            k_blk = k_ref[0, j * tq:j * tq + w, h * HEAD_PAD:(h + 1) * HEAD_PAD]
            dst_ref[u, :, :w] = lax.dot_general(q_u, k_blk, (((1,), (1,)), ((), ())),
                                                preferred_element_type=F32)

    scores_into(s_refs[0], *steps[0])
    ms, accs = None, None
    for i, (qi, j) in enumerate(steps):
        src_ref = s_refs[i % 2]
        if i + 1 < len(steps):
            scores_into(s_refs[(i + 1) % 2], *steps[i + 1])
        if j == 0:
            ms = [jnp.full((ATTN_ROWS, 1), -1e30, F32) for _ in units]
            accs = [jnp.zeros((ATTN_ROWS, HEAD_PAD), F32) for _ in units]
        for u, (h, r) in enumerate(units):
            w = width(qi, j, r)
            if j == qi:
                col = lax.broadcasted_iota(jnp.int32, (ATTN_ROWS, ATTN_ROWS), 1)
                row = lax.broadcasted_iota(jnp.int32, (ATTN_ROWS, ATTN_ROWS), 0)
                edge = slice(w - ATTN_ROWS, w)
                src_ref[u, :, edge] = jnp.where(col <= row, src_ref[u, :, edge], -1e30)
            v_blk = v_ref[0, j * tq:j * tq + w, h * HEAD_PAD:(h + 1) * HEAD_PAD]
            m_new = jnp.maximum(ms[u], jnp.max(src_ref[u, :, :w], axis=-1, keepdims=True))
            p = jnp.exp2((src_ref[u, :, :w] - m_new).astype(BF16))
            accs[u] = jnp.exp2(ms[u] - m_new) * accs[u] + _dot(p, v_blk)
            ms[u] = m_new
        if j == qi:
            lane = lax.broadcasted_iota(jnp.int32, (ATTN_ROWS, HEAD_PAD), 1)
            outs = [acc / acc[:, MLA_V:MLA_V + 1] for acc in accs]
            for pair in range(heads // 2):
                for r in range(halves):
                    o_pair = jnp.where(lane < MLA_V, outs[2 * pair * halves + r],
                                       pltpu.roll(outs[(2 * pair + 1) * halves + r], MLA_V, axis=1))
                    o_ref[0, qi * tq + r * ATTN_ROWS:qi * tq + (r + 1) * ATTN_ROWS,
                          pair * HEAD_PAD:(pair + 1) * HEAD_PAD] = o_pair.astype(o_ref.dtype)


def _mla_attn_kernel(q_ref, k_ref, v_ref, o_ref, s0_ref, s1_ref, *, tq, groups):
    g = pl.program_id(2)
    for idx, tiles in enumerate(groups):
        @pl.when(g == idx)
        def _(tiles=tiles):
            _mla_attn_tiles(q_ref, k_ref, v_ref, o_ref, (s0_ref, s1_ref), tiles, tq)


def _mla_attn_call(q, k, v, tq, heads_per_step=2):
    bsz, seqlen, _ = q.shape
    width = heads_per_step * HEAD_PAD
    n_hsteps = MLA_HEADS // heads_per_step
    out_w = heads_per_step * MLA_V
    n_units = heads_per_step * (tq // ATTN_ROWS)
    n_q = seqlen // tq
    groups = tuple(tuple(sorted({i, n_q - 1 - i})) for i in range((n_q + 1) // 2))
    resident = lambda w: pl.BlockSpec((1, seqlen, w), lambda b, h, g: (b, 0, h))
    return pl.pallas_call(
        functools.partial(_mla_attn_kernel, tq=tq, groups=groups),
        grid=(bsz, n_hsteps, len(groups)),
        in_specs=[resident(width), resident(width), resident(width)],
        out_specs=resident(out_w),
        out_shape=jax.ShapeDtypeStruct((bsz, seqlen, MLA_WIDTH), BF16),
        scratch_shapes=[pltpu.VMEM((n_units, ATTN_ROWS, tq), F32),
                        pltpu.VMEM((n_units, ATTN_ROWS, tq), F32)],
        compiler_params=_params(("parallel", "parallel", "arbitrary")),
        name="mla_attention",
    )(q, k, v)


def _layernorm(z, g, b):
    mu = jnp.mean(z, axis=-1, keepdims=True)
    zc = z - mu
    var = jnp.mean(zc * zc, axis=-1, keepdims=True)
    return zc * lax.rsqrt(var + 1e-5) * g + b


def _ab_out_kernel(h_ref, ya_ref, gate_ref, x_ref, w_ref, g_ref, b_ref, o_ref):
    gate = gate_ref[0].astype(F32)
    y_rnn = (h_ref[0] * gate[:, :RNN_WIDTH]).astype(BF16)
    y_mla = (ya_ref[0].astype(F32) * gate[:, RNN_WIDTH:]).astype(BF16)
    y = _dot(y_rnn, w_ref[:RNN_WIDTH, :]) + _dot(y_mla, w_ref[RNN_WIDTH:, :])
    o_ref[0] = _layernorm(DN_ALPHA * x_ref[0] + y, g_ref[...], b_ref[...])


def _ab_out_call(h, y_mla, gate, x, w_out, ln_g, ln_b, tm):
    bsz, seqlen, _ = x.shape
    tok = lambda width: pl.BlockSpec((1, tm, width), lambda b, i: (b, i, 0))
    return pl.pallas_call(
        _ab_out_kernel,
        grid=(bsz, seqlen // tm),
        in_specs=[tok(RNN_WIDTH), tok(MLA_WIDTH), tok(AB_WIDTH), tok(D_MODEL),
                  _const_spec((AB_WIDTH, D_MODEL)), _const_spec((1, D_MODEL)),
                  _const_spec((1, D_MODEL))],
        out_specs=tok(D_MODEL),
        out_shape=jax.ShapeDtypeStruct((bsz, seqlen, D_MODEL), F32),
        compiler_params=_params(("parallel", "parallel")),
        name="ab_out_proj",
    )(h, y_mla, gate, x, w_out, ln_g, ln_b)


SSD_IN_COLS = 512
SUBLANES = 8


def _causal_conv_rows(ext, w, b):
    rows, width = ext.shape
    tiles = rows // SUBLANES
    x3 = ext.reshape(tiles, SUBLANES, width)
    first = lax.broadcasted_iota(jnp.int32, x3.shape, 1) == 0

    def shift_down(a):
        r = pltpu.roll(a, 1, axis=1)
        r_prev = jnp.concatenate([r[:1], r[:-1]], axis=0)
        return jnp.where(first, r_prev, r)

    acc = x3 * w[0:1, :][None]
    for k in range(1, CONV_WIDTH):
        acc = shift_down(acc) + x3 * w[k:k + 1, :][None]
    return (acc[1:] + b[None]).reshape(rows - SUBLANES, width)


def _ssd_in_kernel(x_ref, wz_ref, wx_ref, wdt_ref, dt_bias_ref, conv_w_ref, conv_b_ref,
                   z_ref, xbc_ref, dt_ref, tail_ref):
    tm = x_ref.shape[1]

    @pl.when(pl.program_id(1) == 0)
    def _():
        tail_ref[...] = jnp.zeros_like(tail_ref)

    xb = x_ref[0].astype(BF16)
    dt_ref[0] = _softplus(_dot(xb, wdt_ref[...]) + dt_bias_ref[...])
    for c in range(SSD_INNER // SSD_IN_COLS):
        cols = slice(c * SSD_IN_COLS, (c + 1) * SSD_IN_COLS)
        z_ref[0, :, cols] = _silu(_dot(xb, wz_ref[:, cols])).astype(BF16)
    for c in range(SSD_CONV_DIM // SSD_IN_COLS):
        cols = slice(c * SSD_IN_COLS, (c + 1) * SSD_IN_COLS)
        proj = _dot(xb, wx_ref[:, cols])
        ext = jnp.concatenate([tail_ref[:, cols], proj], axis=0)
        tail_ref[:, cols] = proj[tm - 8:]
        conv = _causal_conv_rows(ext, conv_w_ref[:, cols], conv_b_ref[:, cols])
        xbc_ref[0, :, cols] = _silu(conv).astype(BF16)


def _ssd_in_call(x, wz, wx, wdt, dt_bias, conv_w, conv_b, tm):
    bsz, seqlen, _ = x.shape
    tok = lambda width: pl.BlockSpec((1, tm, width), lambda b, i: (b, i, 0))
    return pl.pallas_call(
        _ssd_in_kernel,
        grid=(bsz, seqlen // tm),
        in_specs=[tok(D_MODEL), _const_spec((D_MODEL, SSD_INNER)),
                  _const_spec((D_MODEL, SSD_CONV_DIM)), _const_spec((D_MODEL, LANES)),
                  _const_spec((1, LANES)), _const_spec((CONV_WIDTH, SSD_CONV_DIM)),
                  _const_spec((1, SSD_CONV_DIM))],
        out_specs=[tok(SSD_INNER), tok(SSD_CONV_DIM), tok(LANES)],
        out_shape=[jax.ShapeDtypeStruct((bsz, seqlen, SSD_INNER), BF16),
                   jax.ShapeDtypeStruct((bsz, seqlen, SSD_CONV_DIM), BF16),
                   jax.ShapeDtypeStruct((bsz, seqlen, LANES), F32)],
        scratch_shapes=[pltpu.VMEM((8, SSD_CONV_DIM), F32)],
        compiler_params=_params(("parallel", "arbitrary")),
        name="ssd_in_proj",
    )(x, wz, wx, wdt, dt_bias, conv_w, conv_b)


def _cumsum_rows(x):
    n = x.shape[0]
    row = lax.broadcasted_iota(jnp.int32, x.shape, 0)
    shift = 1
    while shift < n:
        x = x + jnp.where(row >= shift, pltpu.roll(x, shift, axis=0), 0.0)
        shift *= 2
    return x


LOG2E = math.log2(math.e)


def _ssd_chunk(xbc_ref, rows, dt, state_ref, a_neg2, d_skip_x):
    L = SSD_CHUNK
    gw = SSD_GROUPS * SSD_STATE

    cs = _cumsum_rows(dt * a_neg2)
    cs_t = cs.T
    dt_t = dt.T
    w_t = dt_t * jnp.exp2(cs_t[:, L - 1:L] - cs_t)

    row = lax.broadcasted_iota(jnp.int32, (L, L), 0)
    col = lax.broadcasted_iota(jnp.int32, (L, L), 1)
    tril = col <= row
    lane = lax.broadcasted_iota(jnp.int32, (L, LANES), 1)
    low = lane < SSD_HEAD_DIM
    keep_lo = low.astype(F32).astype(BF16)
    keep_hi = 1.0 - keep_lo

    hpg = SSD_HEADS // SSD_GROUPS
    y_parts = []
    for g in range(SSD_GROUPS):
        b_g = xbc_ref[0, rows, SSD_INNER + g * SSD_STATE:SSD_INNER + (g + 1) * SSD_STATE]
        c_g = xbc_ref[0, rows, SSD_INNER + gw + g * SSD_STATE:SSD_INNER + gw + (g + 1) * SSD_STATE]
        cb = lax.dot_general(c_g, b_g, (((1,), (1,)), ((), ())),
                             preferred_element_type=F32)
        cb = jnp.where(tril, cb, 0.0)
        b_t = b_g.astype(F32).T
        glanes = slice(g * SSD_GROUP_WIDTH, (g + 1) * SSD_GROUP_WIDTH)
        y_off = _dot(c_g, state_ref[:, glanes].astype(BF16))
        for pair in range(hpg // 2):
            h0 = g * hpg + 2 * pair
            plane = slice(h0 * SSD_HEAD_DIM, (h0 + 2) * SSD_HEAD_DIM)
            tops, bots, decays = [], [], []
            for h in (h0, h0 + 1):
                cs_col = jnp.broadcast_to(cs[:, h:h + 1], (L, LANES))
                seg = cs_col - cs_t[h:h + 1, :]
                m_h = jnp.exp2(jnp.minimum(seg, 0.0)) * (cb * dt_t[h:h + 1, :])
                tops.append(m_h.astype(BF16))
                bots.append((b_t * w_t[h:h + 1, :]).astype(BF16))
                decays.append(jnp.exp2(cs_col))
            lhs = jnp.concatenate([jnp.concatenate(tops, axis=1),
                                   jnp.concatenate(bots, axis=1)], axis=0)
            x_pair = xbc_ref[0, rows, plane]
            rhs = jnp.concatenate([x_pair * keep_lo, x_pair * keep_hi], axis=0)
            res = _dot(lhs, rhs)
            dec = jnp.where(low, decays[0], decays[1])
            off = pair * 2 * SSD_HEAD_DIM
            y_pair = (res[:L] + dec * y_off[:, off:off + 2 * SSD_HEAD_DIM]
                      + d_skip_x[:, plane] * x_pair.astype(F32))
            y_parts.append(y_pair)
            state_ref[:, plane] = state_ref[:, plane] * dec[L - 1:L, :] + res[L:]
    return jnp.concatenate(y_parts, axis=1)


def _ssd_kernel(z_ref, xbc_ref, dt_ref, x_ref, a_log_ref, d_skip_ref, norm_ref, w_out_ref,
                g_ref, b_ref, o_ref, state_ref, *, n_chunks):
    L = SSD_CHUNK

    @pl.when(pl.program_id(1) == 0)
    def _():
        state_ref[...] = jnp.zeros_like(state_ref)

    a_neg2 = -jnp.exp(a_log_ref[...]) * LOG2E
    for c in range(n_chunks):
        rows = slice(c * L, (c + 1) * L)
        y = _ssd_chunk(xbc_ref, rows, dt_ref[0, rows, :], state_ref, a_neg2, d_skip_ref[...])
        y = y * z_ref[0, rows, :].astype(F32)
        parts = []
        for g in range(SSD_GROUPS):
            yg = y[:, g * SSD_GROUP_WIDTH:(g + 1) * SSD_GROUP_WIDTH]
            parts.append(yg * lax.rsqrt(jnp.mean(yg * yg, axis=-1, keepdims=True) + 1e-6))
        yn = (jnp.concatenate(parts, axis=1) * norm_ref[...]).astype(BF16)
        out = _dot(yn, w_out_ref[...])
        o_ref[0, rows, :] = _layernorm(DN_ALPHA * x_ref[0, rows, :] + out, g_ref[...], b_ref[...])


def _ssd_call(z, xbc, dt, x, a_log, d_skip_x, norm_w, w_out, ln_g, ln_b, n_chunks):
    bsz, seqlen, _ = x.shape
    tm = n_chunks * SSD_CHUNK
    tok = lambda width: pl.BlockSpec((1, tm, width), lambda b, i: (b, i, 0))
    return pl.pallas_call(
        functools.partial(_ssd_kernel, n_chunks=n_chunks),
        grid=(bsz, seqlen // tm),
        in_specs=[tok(SSD_INNER), tok(SSD_CONV_DIM), tok(LANES), tok(D_MODEL),
                  _const_spec((1, LANES)), _const_spec((1, SSD_INNER)),
                  _const_spec((1, SSD_INNER)), _const_spec((SSD_INNER, D_MODEL)),
                  _const_spec((1, D_MODEL)), _const_spec((1, D_MODEL))],
        out_specs=tok(D_MODEL),
        out_shape=jax.ShapeDtypeStruct((bsz, seqlen, D_MODEL), F32),
        scratch_shapes=[pltpu.VMEM((SSD_STATE, SSD_INNER), F32)],
        compiler_params=_params(("parallel", "arbitrary")),
        name="ssd_scan_out",
    )(z, xbc, dt, x, a_log, d_skip_x, norm_w, w_out, ln_g, ln_b)


def _rot_cols(w):
    half = MLA_ROPE // 2
    return jnp.concatenate([-w[..., half:], w[..., :half]], axis=-1)


def _pad_heads(w_heads):
    r, h, c = w_heads.shape
    return jnp.pad(w_heads, ((0, 0), (0, 0), (0, HEAD_PAD - c))).reshape(r, h * HEAD_PAD)


def _block_diag(w):
    h, d, _ = w.shape
    eye = jnp.eye(h, dtype=w.dtype)
    return (eye[:, None, :, None] * w[:, :, None, :]).reshape(h * d, h * d)


def _layer0(x, pos_f, freq, w_in, conv_w, conv_b, gate_a_w, gate_a_b, gate_x_w, gate_x_b, lam,
            q_norm, kv_norm, w_uq, w_ukv, w_out, ln_g, ln_b, tm, ts, tq):
    o_kr = RNN_WIDTH + AB_WIDTH + MLA_Q_RANK + MLA_KV_RANK
    w_kr = w_in[:, o_kr:]
    pad_kr = lambda w: jnp.pad(w, ((0, 0), (MLA_NOPE, HEAD_PAD - MLA_NOPE - MLA_ROPE)))
    w_in_ext = jnp.concatenate([w_in[:, :o_kr], pad_kr(w_kr), pad_kr(_rot_cols(w_kr))],
                               axis=1).astype(BF16)

    uq = w_uq.reshape(MLA_Q_RANK, MLA_HEADS, MLA_NOPE + MLA_ROPE)
    uq_rot = jnp.concatenate([jnp.zeros_like(uq[..., :MLA_NOPE]), _rot_cols(uq[..., MLA_NOPE:])],
                             axis=-1)
    wq = _pad_heads(uq).astype(BF16)
    wqr = _pad_heads(uq_rot).astype(BF16)
    ukv = w_ukv.reshape(MLA_KV_RANK, MLA_HEADS, MLA_NOPE + MLA_V)
    wk = _pad_heads(ukv[..., :MLA_NOPE]).astype(BF16)
    wv = _pad_heads(ukv[..., MLA_NOPE:]).astype(BF16)
    vones = jnp.tile((jnp.arange(HEAD_PAD) >= MLA_V).astype(F32), MLA_HEADS)[None]

    xr, gate, q, k, v = _ab_in_call(x, pos_f, w_in_ext, freq, q_norm[None], kv_norm[None],
                                    wq, wqr, wk, wv, vones, tm)

    w_gate = jnp.concatenate([_block_diag(gate_a_w), _block_diag(gate_x_w)], axis=1).astype(BF16)
    b_gate = jnp.concatenate([gate_a_b, gate_x_b])[None]
    h = _rglru_call(xr, conv_w, conv_b[None], w_gate, b_gate, lam[None], ts)

    y_mla = _mla_attn_call(q, k, v, tq)
    return _ab_out_call(h, y_mla, gate, x, w_out.astype(BF16), ln_g[None], ln_b[None], tm)


def _layer1(x, w_in, conv_w, conv_b, dt_bias, a_log, d_skip, norm_w, w_out, ln_g, ln_b,
            tm, n_chunks):
    wz = w_in[:, :SSD_INNER].astype(BF16)
    wx = w_in[:, SSD_INNER:SSD_INNER + SSD_CONV_DIM].astype(BF16)
    pad_h = lambda a: jnp.pad(a, ((0, 0), (0, LANES - SSD_HEADS)))
    wdt = pad_h(w_in[:, SSD_INNER + SSD_CONV_DIM:]).astype(BF16)
    z, xbc, dt = _ssd_in_call(x, wz, wx, wdt, pad_h(dt_bias[None]), conv_w, conv_b[None], tm)
    d_skip_x = jnp.repeat(d_skip, SSD_HEAD_DIM)[None]
    return _ssd_call(z, xbc, dt, x, pad_h(a_log[None]), d_skip_x, norm_w[None],
                     w_out.astype(BF16), ln_g[None], ln_b[None], n_chunks)


def kernel(x, positions, ab_w_in, ab_conv_w, ab_conv_b, ab_gate_a_w, ab_gate_a_b, ab_gate_x_w,
           ab_gate_x_b, ab_lambda, mla_q_norm, mla_kv_norm, mla_w_uq, mla_w_ukv, ab_w_out,
           ab_ln_g, ab_ln_b, ssd_w_in, ssd_conv_w, ssd_conv_b, ssd_dt_bias, ssd_a_log, ssd_d,
           ssd_norm, ssd_w_out, ssd_ln_g, ssd_ln_b):
    seqlen = x.shape[1]
    tm = min(512, seqlen)
    ts = min(128, seqlen)
    tq = min(512, seqlen)
    n_chunks = min(2, seqlen // SSD_CHUNK)

    inv_freq = ROPE_THETA ** (-jnp.arange(0, MLA_ROPE, 2, dtype=F32) / MLA_ROPE)
    freq = jnp.concatenate([jnp.zeros((MLA_NOPE,), F32), inv_freq, inv_freq,
                            jnp.zeros((HEAD_PAD - MLA_NOPE - MLA_ROPE,), F32)])[None]
    pos_f = positions.astype(F32)[..., None]

    for layer in range(DEPTH):
        j = layer // 2
        if layer % 2 == 0:
            x = _layer0(x, pos_f, freq, ab_w_in[j], ab_conv_w[j], ab_conv_b[j], ab_gate_a_w[j],
                        ab_gate_a_b[j], ab_gate_x_w[j], ab_gate_x_b[j], ab_lambda[j],
                        mla_q_norm[j], mla_kv_norm[j], mla_w_uq[j], mla_w_ukv[j], ab_w_out[j],
                        ab_ln_g[j], ab_ln_b[j], tm, ts, tq)
        else:
            x = _layer1(x, ssd_w_in[j], ssd_conv_w[j], ssd_conv_b[j], ssd_dt_bias[j],
                        ssd_a_log[j], ssd_d[j], ssd_norm[j], ssd_w_out[j], ssd_ln_g[j],
                        ssd_ln_b[j], tm, n_chunks)
    return x
```

```python
import functools
import math

import jax
import jax.numpy as jnp
from jax import lax
from jax.experimental import pallas as pl
from jax.experimental.pallas import tpu as pltpu

D_MODEL = 1024
DEPTH = 2
DN_ALPHA = (2.0 * DEPTH) ** 0.25

RNN_WIDTH = 512
RNN_HEADS = 8
RNN_HEAD_DIM = RNN_WIDTH // RNN_HEADS
CONV_WIDTH = 4
RG_C = 8.0

MLA_HEADS = 8
MLA_NOPE = 64
MLA_ROPE = 32
MLA_V = 64
MLA_Q_RANK = 256
MLA_KV_RANK = 128
MLA_WIDTH = MLA_HEADS * MLA_V
ROPE_THETA = 10000.0
AB_WIDTH = RNN_WIDTH + MLA_WIDTH

SSD_INNER = 2048
SSD_HEAD_DIM = 64
SSD_HEADS = 32
SSD_GROUPS = 4
SSD_STATE = 128
SSD_CHUNK = 128
SSD_CONV_DIM = SSD_INNER + 2 * SSD_GROUPS * SSD_STATE
SSD_GROUP_WIDTH = SSD_INNER // SSD_GROUPS

LANES = 128
HEAD_PAD = LANES
QK_PAD_WIDTH = MLA_HEADS * HEAD_PAD
VMEM_LIMIT = 56 * 1024 * 1024

BF16 = jnp.bfloat16
F32 = jnp.float32


def _sigmoid(x):
    return 1.0 / (1.0 + jnp.exp(-x))


def _silu(x):
    return x * _sigmoid(x)


def _log1p(y):
    u = 1.0 + y
    return jnp.where(u == 1.0, y, jnp.log(u) * (y / (u - 1.0)))


def _softplus(x):
    return jnp.maximum(x, 0.0) + _log1p(jnp.exp(-jnp.abs(x)))


def _dot(a, b):
    return jnp.dot(a, b, preferred_element_type=F32)


def _const_spec(shape):
    zeros = (0,) * len(shape)
    return pl.BlockSpec(shape, lambda *_: zeros)


def _params(semantics, flags=None):
    return pltpu.CompilerParams(dimension_semantics=semantics,
                                vmem_limit_bytes=VMEM_LIMIT, flags=flags)


def _ab_in_kernel(x_ref, pos_ref, w_in_ref, freq_ref, qn_ref, kvn_ref, wq_ref,
                  wk_ref, wv_ref, vones_ref,
                  xr_ref, gate_ref, q_ref, k_ref, v_ref):
    xb = x_ref[0].astype(BF16)
    proj = _dot(xb, w_in_ref[...])
    o_gate = RNN_WIDTH
    o_cq = o_gate + AB_WIDTH
    o_ckv = o_cq + MLA_Q_RANK
    o_kr = o_ckv + MLA_KV_RANK

    xr_ref[0] = proj[:, :o_gate]
    gate_ref[0] = _silu(proj[:, o_gate:o_cq]).astype(BF16)

    ang = pos_ref[0] * freq_ref[...]
    cos = jnp.cos(ang)
    sin = jnp.sin(ang)
    to_rope_lanes = HEAD_PAD - MLA_ROPE

    c_q = proj[:, o_cq:o_ckv]
    c_q = c_q * lax.rsqrt(jnp.mean(c_q * c_q, axis=-1, keepdims=True) + 1e-6) * qn_ref[...]
    q_all = _dot(c_q.astype(BF16), wq_ref[...])
    scale = (MLA_NOPE + MLA_ROPE) ** -0.5 * math.log2(math.e)
    for h in range(MLA_HEADS):
        q_h = q_all[:, h * HEAD_PAD:(h + 1) * HEAD_PAD]
        q_h = q_h * cos + pltpu.roll(q_h, to_rope_lanes, axis=1) * sin
        q_ref[0, :, h * HEAD_PAD:(h + 1) * HEAD_PAD] = (q_h * scale).astype(BF16)

    c_kv = proj[:, o_ckv:o_kr]
    c_kv = c_kv * lax.rsqrt(jnp.mean(c_kv * c_kv, axis=-1, keepdims=True) + 1e-6) * kvn_ref[...]
    c_kvb = c_kv.astype(BF16)
    kr = proj[:, o_kr:o_kr + HEAD_PAD]
    lane = lax.broadcasted_iota(jnp.int32, kr.shape, 1)
    k_rope = jnp.where(lane < MLA_NOPE + MLA_ROPE,
                       kr * cos + pltpu.roll(kr, to_rope_lanes, axis=1) * sin, 0.0)
    k = _dot(c_kvb, wk_ref[...]) + jnp.concatenate([k_rope] * MLA_HEADS, axis=1)
    k_ref[0] = k.astype(BF16)
    v = _dot(c_kvb, wv_ref[...]) + vones_ref[...]
    v_ref[0] = v.astype(BF16)


def _ab_in_call(x, pos_f, w_in_ext, freq, q_norm, kv_norm, wq, wk, wv, vones, tm):
    bsz, seqlen, _ = x.shape
    n_in = w_in_ext.shape[1]
    grid = (bsz, seqlen // tm)
    tok = lambda width: pl.BlockSpec((1, tm, width), lambda b, i: (b, i, 0))
    return pl.pallas_call(
        _ab_in_kernel,
        grid=grid,
        in_specs=[tok(D_MODEL), tok(1), _const_spec((D_MODEL, n_in)), _const_spec((1, HEAD_PAD)),
                  _const_spec((1, MLA_Q_RANK)), _const_spec((1, MLA_KV_RANK)),
                  _const_spec((MLA_Q_RANK, QK_PAD_WIDTH)),
                  _const_spec((MLA_KV_RANK, QK_PAD_WIDTH)), _const_spec((MLA_KV_RANK, QK_PAD_WIDTH)),
                  _const_spec((1, QK_PAD_WIDTH))],
        out_specs=[tok(RNN_WIDTH), tok(AB_WIDTH), tok(QK_PAD_WIDTH), tok(QK_PAD_WIDTH),
                   tok(QK_PAD_WIDTH)],
        out_shape=[jax.ShapeDtypeStruct((bsz, seqlen, RNN_WIDTH), F32),
                   jax.ShapeDtypeStruct((bsz, seqlen, AB_WIDTH), BF16),
                   jax.ShapeDtypeStruct((bsz, seqlen, QK_PAD_WIDTH), BF16),
                   jax.ShapeDtypeStruct((bsz, seqlen, QK_PAD_WIDTH), BF16),
                   jax.ShapeDtypeStruct((bsz, seqlen, QK_PAD_WIDTH), BF16)],
        compiler_params=_params(("parallel", "parallel")),
        name="ab_in_proj",
    )(x, pos_f, w_in_ext, freq, q_norm, kv_norm, wq, wk, wv, vones)


def _rglru_kernel(xr_ref, conv_w_ref, conv_b_ref, w_gate_ref, b_gate_ref, lam_ref,
                  h_ref, tail_ref, carry_ref, a_ref, u_ref):
    bsz, ts, width = xr_ref.shape

    @pl.when(pl.program_id(0) == 0)
    def _():
        tail_ref[...] = jnp.zeros_like(tail_ref)
        carry_ref[...] = jnp.zeros_like(carry_ref)

    x_blk = jnp.swapaxes(xr_ref[...], 0, 1)
    x_ext = jnp.concatenate([tail_ref[...], x_blk], axis=0)
    tail_ref[...] = x_blk[ts - (CONV_WIDTH - 1):]
    xc = conv_b_ref[...][None]
    for k in range(CONV_WIDTH):
        xc = xc + x_ext[k:k + ts] * conv_w_ref[k:k + 1, :][None]

    xc2 = xc.reshape(ts * bsz, width)
    gates = _dot(xc2.astype(BF16), w_gate_ref[...]) + b_gate_ref[...]
    r = _sigmoid(gates[:, :width])
    i = _sigmoid(gates[:, width:])
    log_a = (-RG_C * r) * _softplus(-lam_ref[...])
    a = jnp.exp(log_a)
    mult = jnp.sqrt(-jnp.tanh(log_a) * (a * a + 1.0))
    u = mult * (i * xc2)
    a_ref[...] = a.reshape(ts, bsz, width)
    u_ref[...] = u.reshape(ts, bsz, width)

    def step(t, h):
        h = a_ref[t] * h + u_ref[t]
        u_ref[t] = h
        return h

    carry_ref[...] = lax.fori_loop(0, ts, step, carry_ref[...], unroll=8)
    h_ref[...] = jnp.swapaxes(u_ref[...], 0, 1)


def _rglru_call(xr, conv_w, conv_b, w_gate, b_gate, lam, ts):
    bsz, seqlen, width = xr.shape
    blk = pl.BlockSpec((bsz, ts, width), lambda i: (0, i, 0))
    return pl.pallas_call(
        _rglru_kernel,
        grid=(seqlen // ts,),
        in_specs=[blk, _const_spec((CONV_WIDTH, width)), _const_spec((1, width)),
                  _const_spec((width, 2 * width)), _const_spec((1, 2 * width)),
                  _const_spec((1, width))],
        out_specs=blk,
        out_shape=jax.ShapeDtypeStruct((bsz, seqlen, width), F32),
        scratch_shapes=[pltpu.VMEM((CONV_WIDTH - 1, bsz, width), F32),
                        pltpu.VMEM((bsz, width), F32),
                        pltpu.VMEM((ts, bsz, width), F32),
                        pltpu.VMEM((ts, bsz, width), F32)],
        compiler_params=_params(("arbitrary",)),
        name="rglru_scan",
    )(xr, conv_w, conv_b, w_gate, b_gate, lam)


ATTN_ROWS = 256


def _mla_attn_tiles(q_ref, k_ref, v_ref, o_ref, s_refs, tiles, tq):
    heads = q_ref.shape[2] // HEAD_PAD
    halves = tq // ATTN_ROWS
    units = [(h, r) for h in range(heads) for r in range(halves)]
    steps = [(qi, j) for qi in tiles for j in range(qi + 1)]

    def width(qi, j, r):
        return (r + 1) * ATTN_ROWS if j == qi else tq

    def scores_into(dst_ref, qi, j):
        for u, (h, r) in enumerate(units):
            w = width(qi, j, r)
            q_u = q_ref[0, qi * tq + r * ATTN_ROWS:qi * tq + (r + 1) * ATTN_ROWS,
                        h * HEAD_PAD:(h + 1) * HEAD_PAD]
            k_blk = k_ref[0, j * tq:j * tq + w, h * HEAD_PAD:(h + 1) * HEAD_PAD]
            dst_ref[u, :, :w] = lax.dot_general(q_u, k_blk, (((1,), (1,)), ((), ())),
                                                preferred_element_type=F32)

    scores_into(s_refs[0], *steps[0])
    ms, accs = None, None
    for i, (qi, j) in enumerate(steps):
        src_ref = s_refs[i % 2]
        if i + 1 < len(steps):
            scores_into(s_refs[(i + 1) % 2], *steps[i + 1])
        if j == 0:
            ms = [jnp.full((ATTN_ROWS, 1), -1e30, F32) for _ in units]
            accs = [jnp.zeros((ATTN_ROWS, HEAD_PAD), F32) for _ in units]
        for u, (h, r) in enumerate(units):
            w = width(qi, j, r)
            if j == qi:
                col = lax.broadcasted_iota(jnp.int32, (ATTN_ROWS, ATTN_ROWS), 1)
                row = lax.broadcasted_iota(jnp.int32, (ATTN_ROWS, ATTN_ROWS), 0)
                edge = slice(w - ATTN_ROWS, w)
                src_ref[u, :, edge] = jnp.where(col <= row, src_ref[u, :, edge], -1e30)
            v_blk = v_ref[0, j * tq:j * tq + w, h * HEAD_PAD:(h + 1) * HEAD_PAD]
            m_new = jnp.maximum(ms[u], jnp.max(src_ref[u, :, :w], axis=-1, keepdims=True))
            p = jnp.exp2((src_ref[u, :, :w] - m_new).astype(BF16))
            accs[u] = jnp.exp2(ms[u] - m_new) * accs[u] + _dot(p, v_blk)
            ms[u] = m_new
        if j == qi:
            lane = lax.broadcasted_iota(jnp.int32, (ATTN_ROWS, HEAD_PAD), 1)
            outs = [acc / acc[:, MLA_V:MLA_V + 1] for acc in accs]
            for pair in range(heads // 2):
                for r in range(halves):
                    o_pair = jnp.where(lane < MLA_V, outs[2 * pair * halves + r],
                                       pltpu.roll(outs[(2 * pair + 1) * halves + r], MLA_V, axis=1))
                    o_ref[0, qi * tq + r * ATTN_ROWS:qi * tq + (r + 1) * ATTN_ROWS,
                          pair * HEAD_PAD:(pair + 1) * HEAD_PAD] = o_pair.astype(o_ref.dtype)


def _mla_attn_kernel(q_ref, k_ref, v_ref, o_ref, s0_ref, s1_ref, *, tq, groups):
    g = pl.program_id(2)
    for idx, tiles in enumerate(groups):
        @pl.when(g == idx)
        def _(tiles=tiles):
            _mla_attn_tiles(q_ref, k_ref, v_ref, o_ref, (s0_ref, s1_ref), tiles, tq)


def _mla_attn_call(q, k, v, tq, heads_per_step=2):
    bsz, seqlen, _ = q.shape
    width = heads_per_step * HEAD_PAD
    n_hsteps = MLA_HEADS // heads_per_step
    out_w = heads_per_step * MLA_V
    n_units = heads_per_step * (tq // ATTN_ROWS)
    n_q = seqlen // tq
    groups = tuple(tuple(sorted({i, n_q - 1 - i})) for i in range((n_q + 1) // 2))
    resident = lambda w: pl.BlockSpec((1, seqlen, w), lambda b, h, g: (b, 0, h))
    return pl.pallas_call(
        functools.partial(_mla_attn_kernel, tq=tq, groups=groups),
        grid=(bsz, n_hsteps, len(groups)),
        in_specs=[resident(width), resident(width), resident(width)],
        out_specs=resident(out_w),
        out_shape=jax.ShapeDtypeStruct((bsz, seqlen, MLA_WIDTH), BF16),
        scratch_shapes=[pltpu.VMEM((n_units, ATTN_ROWS, tq), F32),
                        pltpu.VMEM((n_units, ATTN_ROWS, tq), F32)],
        compiler_params=_params(("parallel", "parallel", "arbitrary")),
        name="mla_attention",
    )(q, k, v)


def _layernorm(z, g, b):
    mu = jnp.mean(z, axis=-1, keepdims=True)
    zc = z - mu
    var = jnp.mean(zc * zc, axis=-1, keepdims=True)
    return zc * lax.rsqrt(var + 1e-5) * g + b


def _ab_out_kernel(h_ref, ya_ref, gate_ref, x_ref, w_ref, g_ref, b_ref, o_ref):
    gate = gate_ref[0].astype(F32)
    y_rnn = (h_ref[0] * gate[:, :RNN_WIDTH]).astype(BF16)
    y_mla = (ya_ref[0].astype(F32) * gate[:, RNN_WIDTH:]).astype(BF16)
    y = _dot(y_rnn, w_ref[:RNN_WIDTH, :]) + _dot(y_mla, w_ref[RNN_WIDTH:, :])
    o_ref[0] = _layernorm(DN_ALPHA * x_ref[0] + y, g_ref[...], b_ref[...])


def _ab_out_call(h, y_mla, gate, x, w_out, ln_g, ln_b, tm):
    bsz, seqlen, _ = x.shape
    tok = lambda width: pl.BlockSpec((1, tm, width), lambda b, i: (b, i, 0))
    return pl.pallas_call(
        _ab_out_kernel,
        grid=(bsz, seqlen // tm),
        in_specs=[tok(RNN_WIDTH), tok(MLA_WIDTH), tok(AB_WIDTH), tok(D_MODEL),
                  _const_spec((AB_WIDTH, D_MODEL)), _const_spec((1, D_MODEL)),
                  _const_spec((1, D_MODEL))],
        out_specs=tok(D_MODEL),
        out_shape=jax.ShapeDtypeStruct((bsz, seqlen, D_MODEL), F32),
        compiler_params=_params(("parallel", "parallel")),
        name="ab_out_proj",
    )(h, y_mla, gate, x, w_out, ln_g, ln_b)


SSD_IN_COLS = 512
SUBLANES = 8


BF16_ROWS = 16


def _conv_silu_store(proj, tail, w, b, out_ref, cols):
    n_rows = proj.shape[0]
    taps = [w[k:k + 1, :] for k in range(CONV_WIDTH)]
    first = lax.broadcasted_iota(jnp.int32, tail.shape, 0) == 0
    prev = None
    done = []
    for i in range(-1, n_rows // SUBLANES):
        x = tail if i < 0 else proj[i * SUBLANES:(i + 1) * SUBLANES]
        acc = x * taps[0]
        rolled = []
        for k in range(1, CONV_WIDTH):
            r = pltpu.roll(acc, 1, axis=0)
            rolled.append(r)
            acc = (r if prev is None else jnp.where(first, prev[k - 1], r)) + x * taps[k]
        prev = rolled
        if i >= 0:
            done.append(acc + b)
        if len(done) == BF16_ROWS // SUBLANES:
            top = (i + 1) * SUBLANES
            out_ref[0, top - BF16_ROWS:top, cols] = _silu(
                jnp.concatenate(done, axis=0)).astype(BF16)
            done = []


SSD_PROJ_WIDTH = SSD_INNER + SSD_CONV_DIM + LANES


def _ssd_in_stage(step, x_ref, w_ref, dt_bias_ref, conv_w_ref, conv_b_ref,
                  z_ref, xbc_ref, dt_ref, new_ref, old_ref):
    tm = x_ref.shape[1]
    x_off = SSD_INNER
    tails = []
    for c in range(SSD_CONV_DIM // SSD_IN_COLS):
        cols = slice(x_off + c * SSD_IN_COLS, x_off + (c + 1) * SSD_IN_COLS)
        tails.append(jnp.where(step == 1, 0.0, new_ref[tm - SUBLANES:, cols]))

    xb = x_ref[0].astype(BF16)
    for c in range(0, SSD_PROJ_WIDTH, SSD_IN_COLS):
        cols = slice(c, min(c + SSD_IN_COLS, SSD_PROJ_WIDTH))
        new_ref[:, cols] = _dot(xb, w_ref[:, cols])

    for c in range(SSD_INNER // SSD_IN_COLS):
        cols = slice(c * SSD_IN_COLS, (c + 1) * SSD_IN_COLS)
        z_ref[0, :, cols] = _silu(old_ref[:, cols]).astype(BF16)
    for c in range(SSD_CONV_DIM // SSD_IN_COLS):
        cols = slice(c * SSD_IN_COLS, (c + 1) * SSD_IN_COLS)
        p_cols = slice(x_off + c * SSD_IN_COLS, x_off + (c + 1) * SSD_IN_COLS)
        _conv_silu_store(old_ref[:, p_cols], tails[c], conv_w_ref[:, cols], conv_b_ref[:, cols],
                         xbc_ref, cols)
    dt_ref[0] = _softplus(old_ref[:, SSD_INNER + SSD_CONV_DIM:] + dt_bias_ref[...])


def _ssd_in_kernel(x_ref, w_ref, dt_bias_ref, conv_w_ref, conv_b_ref,
                   z_ref, xbc_ref, dt_ref, p0_ref, p1_ref):
    step = pl.program_id(1)

    @pl.when(step == 0)
    def _():
        p1_ref[...] = jnp.zeros_like(p1_ref)
        p0_ref[p0_ref.shape[0] - SUBLANES:, :] = jnp.zeros((SUBLANES, p0_ref.shape[1]), F32)

    @pl.when(step % 2 == 0)
    def _():
        _ssd_in_stage(step, x_ref, w_ref, dt_bias_ref, conv_w_ref, conv_b_ref,
                      z_ref, xbc_ref, dt_ref, p0_ref, p1_ref)

    @pl.when(step % 2 == 1)
    def _():
        _ssd_in_stage(step, x_ref, w_ref, dt_bias_ref, conv_w_ref, conv_b_ref,
                      z_ref, xbc_ref, dt_ref, p1_ref, p0_ref)


def _ssd_in_call(x, w_all, dt_bias, conv_w, conv_b, tm):
    bsz, seqlen, _ = x.shape
    n_tiles = seqlen // tm
    tile_in = pl.BlockSpec((1, tm, D_MODEL), lambda b, i: (b, jnp.minimum(i, n_tiles - 1), 0))
    tile_out = lambda width: pl.BlockSpec((1, tm, width),
                                          lambda b, i: (b, jnp.maximum(i - 1, 0), 0))
    return pl.pallas_call(
        _ssd_in_kernel,
        grid=(bsz, n_tiles + 1),
        in_specs=[tile_in, _const_spec((D_MODEL, SSD_PROJ_WIDTH)),
                  _const_spec((1, LANES)), _const_spec((CONV_WIDTH, SSD_CONV_DIM)),
                  _const_spec((1, SSD_CONV_DIM))],
        out_specs=[tile_out(SSD_INNER), tile_out(SSD_CONV_DIM), tile_out(LANES)],
        out_shape=[jax.ShapeDtypeStruct((bsz, seqlen, SSD_INNER), BF16),
                   jax.ShapeDtypeStruct((bsz, seqlen, SSD_CONV_DIM), BF16),
                   jax.ShapeDtypeStruct((bsz, seqlen, LANES), F32)],
        scratch_shapes=[pltpu.VMEM((tm, SSD_PROJ_WIDTH), F32),
                        pltpu.VMEM((tm, SSD_PROJ_WIDTH), F32)],
        compiler_params=_params(("parallel", "arbitrary")),
        name="ssd_in_proj",
    )(x, w_all, dt_bias, conv_w, conv_b)


def _cumsum_rows(x):
    n = x.shape[0]
    row = lax.broadcasted_iota(jnp.int32, x.shape, 0)
    shift = 1
    while shift < n:
        x = x + jnp.where(row >= shift, pltpu.roll(x, shift, axis=0), 0.0)
        shift *= 2
    return x


LOG2E = math.log2(math.e)


def _ssd_chunk(xbc_ref, rows, dt, state_ref, a_neg2, d_skip_x):
    L = SSD_CHUNK
    gw = SSD_GROUPS * SSD_STATE

    cs = _cumsum_rows(dt * a_neg2)
    cs_t = cs.T
    dt_t = dt.T
    w_t = dt_t * jnp.exp2(cs_t[:, L - 1:L] - cs_t)

    row = lax.broadcasted_iota(jnp.int32, (L, L), 0)
    col = lax.broadcasted_iota(jnp.int32, (L, L), 1)
    tril = col <= row
    lane = lax.broadcasted_iota(jnp.int32, (L, LANES), 1)
    low = lane < SSD_HEAD_DIM
    keep_lo = low.astype(F32).astype(BF16)
    keep_hi = 1.0 - keep_lo

    hpg = SSD_HEADS // SSD_GROUPS
    y_parts = []
    for g in range(SSD_GROUPS):
        b_g = xbc_ref[0, rows, SSD_INNER + g * SSD_STATE:SSD_INNER + (g + 1) * SSD_STATE]
        c_g = xbc_ref[0, rows, SSD_INNER + gw + g * SSD_STATE:SSD_INNER + gw + (g + 1) * SSD_STATE]
        cb = lax.dot_general(c_g, b_g, (((1,), (1,)), ((), ())),
                             preferred_element_type=F32)
        cb = jnp.where(tril, cb, 0.0)
        b_t = b_g.astype(F32).T
        glanes = slice(g * SSD_GROUP_WIDTH, (g + 1) * SSD_GROUP_WIDTH)
        y_off = _dot(c_g, state_ref[:, glanes].astype(BF16))
        for pair in range(hpg // 2):
            h0 = g * hpg + 2 * pair
            plane = slice(h0 * SSD_HEAD_DIM, (h0 + 2) * SSD_HEAD_DIM)
            tops, bots, decays = [], [], []
            for h in (h0, h0 + 1):
                cs_col = jnp.broadcast_to(cs[:, h:h + 1], (L, LANES))
                seg = cs_col - cs_t[h:h + 1, :]
                m_h = jnp.exp2(jnp.minimum(seg, 0.0)) * (cb * dt_t[h:h + 1, :])
                tops.append(m_h.astype(BF16))
                bots.append((b_t * w_t[h:h + 1, :]).astype(BF16))
                decays.append(jnp.exp2(cs_col))
            lhs = jnp.concatenate([jnp.concatenate(tops, axis=1),
                                   jnp.concatenate(bots, axis=1)], axis=0)
            x_pair = xbc_ref[0, rows, plane]
            rhs = jnp.concatenate([x_pair * keep_lo, x_pair * keep_hi], axis=0)
            res = _dot(lhs, rhs)
            dec = jnp.where(low, decays[0], decays[1])
            off = pair * 2 * SSD_HEAD_DIM
            y_pair = (res[:L] + dec * y_off[:, off:off + 2 * SSD_HEAD_DIM]
                      + d_skip_x[:, plane] * x_pair.astype(F32))
            y_parts.append(y_pair)
            state_ref[:, plane] = state_ref[:, plane] * dec[L - 1:L, :] + res[L:]
    return jnp.concatenate(y_parts, axis=1)


def _ssd_kernel(z_ref, xbc_ref, dt_ref, x_ref, a_log_ref, d_skip_ref, norm_ref, w_out_ref,
                g_ref, b_ref, o_ref, state_ref, *, n_chunks):
    L = SSD_CHUNK

    @pl.when(pl.program_id(1) == 0)
    def _():
        state_ref[...] = jnp.zeros_like(state_ref)

    a_neg2 = -jnp.exp(a_log_ref[...]) * LOG2E
    for c in range(n_chunks):
        rows = slice(c * L, (c + 1) * L)
        y = _ssd_chunk(xbc_ref, rows, dt_ref[0, rows, :], state_ref, a_neg2, d_skip_ref[...])
        y = y * z_ref[0, rows, :].astype(F32)
        parts = []
        for g in range(SSD_GROUPS):
            yg = y[:, g * SSD_GROUP_WIDTH:(g + 1) * SSD_GROUP_WIDTH]
            parts.append(yg * lax.rsqrt(jnp.mean(yg * yg, axis=-1, keepdims=True) + 1e-6))
        yn = (jnp.concatenate(parts, axis=1) * norm_ref[...]).astype(BF16)
        out = _dot(yn, w_out_ref[...])
        o_ref[0, rows, :] = _layernorm(DN_ALPHA * x_ref[0, rows, :] + out, g_ref[...], b_ref[...])


def _ssd_call(z, xbc, dt, x, a_log, d_skip_x, norm_w, w_out, ln_g, ln_b, n_chunks):
    bsz, seqlen, _ = x.shape
    tm = n_chunks * SSD_CHUNK
    tok = lambda width: pl.BlockSpec((1, tm, width), lambda b, i: (b, i, 0))
    return pl.pallas_call(
        functools.partial(_ssd_kernel, n_chunks=n_chunks),
        grid=(bsz, seqlen // tm),
        in_specs=[tok(SSD_INNER), tok(SSD_CONV_DIM), tok(LANES), tok(D_MODEL),
                  _const_spec((1, LANES)), _const_spec((1, SSD_INNER)),
                  _const_spec((1, SSD_INNER)), _const_spec((SSD_INNER, D_MODEL)),
                  _const_spec((1, D_MODEL)), _const_spec((1, D_MODEL))],
        out_specs=tok(D_MODEL),
        out_shape=jax.ShapeDtypeStruct((bsz, seqlen, D_MODEL), F32),
        scratch_shapes=[pltpu.VMEM((SSD_STATE, SSD_INNER), F32)],
        compiler_params=_params(("parallel", "arbitrary")),
        name="ssd_scan_out",
    )(z, xbc, dt, x, a_log, d_skip_x, norm_w, w_out, ln_g, ln_b)


def _rot_cols(w):
    half = MLA_ROPE // 2
    return jnp.concatenate([-w[..., half:], w[..., :half]], axis=-1)


def _pad_heads(w_heads):
    r, h, c = w_heads.shape
    return jnp.pad(w_heads, ((0, 0), (0, 0), (0, HEAD_PAD - c))).reshape(r, h * HEAD_PAD)


def _block_diag(w):
    h, d, _ = w.shape
    eye = jnp.eye(h, dtype=w.dtype)
    return (eye[:, None, :, None] * w[:, :, None, :]).reshape(h * d, h * d)


def _layer0(x, pos_f, freq, w_in, conv_w, conv_b, gate_a_w, gate_a_b, gate_x_w, gate_x_b, lam,
            q_norm, kv_norm, w_uq, w_ukv, w_out, ln_g, ln_b, tm, ts, tq):
    o_kr = RNN_WIDTH + AB_WIDTH + MLA_Q_RANK + MLA_KV_RANK
    w_kr = w_in[:, o_kr:]
    w_in_ext = jnp.concatenate([w_in[:, :o_kr], jnp.zeros_like(w_in[:, :MLA_NOPE]), w_kr,
                                _rot_cols(w_kr)], axis=1).astype(BF16)

    uq = w_uq.reshape(MLA_Q_RANK, MLA_HEADS, MLA_NOPE + MLA_ROPE)
    wq = _pad_heads(jnp.concatenate([uq, _rot_cols(uq[..., MLA_NOPE:])], axis=-1)).astype(BF16)
    ukv = w_ukv.reshape(MLA_KV_RANK, MLA_HEADS, MLA_NOPE + MLA_V)
    wk = _pad_heads(ukv[..., :MLA_NOPE]).astype(BF16)
    wv = _pad_heads(ukv[..., MLA_NOPE:]).astype(BF16)
    vones = jnp.tile((jnp.arange(HEAD_PAD) >= MLA_V).astype(F32), MLA_HEADS)[None]

    xr, gate, q, k, v = _ab_in_call(x, pos_f, w_in_ext, freq, q_norm[None], kv_norm[None],
                                    wq, wk, wv, vones, tm)

    w_gate = jnp.concatenate([_block_diag(gate_a_w), _block_diag(gate_x_w)], axis=1).astype(BF16)
    b_gate = jnp.concatenate([gate_a_b, gate_x_b])[None]
    h = _rglru_call(xr, conv_w, conv_b[None], w_gate, b_gate, lam[None], ts)

    y_mla = _mla_attn_call(q, k, v, tq)
    return _ab_out_call(h, y_mla, gate, x, w_out.astype(BF16), ln_g[None], ln_b[None], tm)


def _layer1(x, w_in, conv_w, conv_b, dt_bias, a_log, d_skip, norm_w, w_out, ln_g, ln_b,
            tm, n_chunks):
    pad_h = lambda a: jnp.pad(a, ((0, 0), (0, LANES - SSD_HEADS)))
    w_all = pad_h(w_in).astype(BF16)
    z, xbc, dt = _ssd_in_call(x, w_all, pad_h(dt_bias[None]), conv_w, conv_b[None], tm)
    d_skip_x = jnp.repeat(d_skip, SSD_HEAD_DIM)[None]
    return _ssd_call(z, xbc, dt, x, pad_h(a_log[None]), d_skip_x, norm_w[None],
                     w_out.astype(BF16), ln_g[None], ln_b[None], n_chunks)


def kernel(x, positions, ab_w_in, ab_conv_w, ab_conv_b, ab_gate_a_w, ab_gate_a_b, ab_gate_x_w,
           ab_gate_x_b, ab_lambda, mla_q_norm, mla_kv_norm, mla_w_uq, mla_w_ukv, ab_w_out,
           ab_ln_g, ab_ln_b, ssd_w_in, ssd_conv_w, ssd_conv_b, ssd_dt_bias, ssd_a_log, ssd_d,
           ssd_norm, ssd_w_out, ssd_ln_g, ssd_ln_b):
    seqlen = x.shape[1]
    tm = min(512, seqlen)
    tm_ssd = min(256, seqlen)
    ts = min(128, seqlen)
    tq = min(512, seqlen)
    n_chunks = min(2, seqlen // SSD_CHUNK)

    inv_freq = ROPE_THETA ** (-jnp.arange(0, MLA_ROPE, 2, dtype=F32) / MLA_ROPE)
    freq = jnp.concatenate([jnp.zeros((MLA_NOPE,), F32), inv_freq, inv_freq,
                            jnp.zeros((HEAD_PAD - MLA_NOPE - MLA_ROPE,), F32)])[None]
    pos_f = positions.astype(F32)[..., None]

    for layer in range(DEPTH):
        j = layer // 2
        if layer % 2 == 0:
            x = _layer0(x, pos_f, freq, ab_w_in[j], ab_conv_w[j], ab_conv_b[j], ab_gate_a_w[j],
                        ab_gate_a_b[j], ab_gate_x_w[j], ab_gate_x_b[j], ab_lambda[j],
                        mla_q_norm[j], mla_kv_norm[j], mla_w_uq[j], mla_w_ukv[j], ab_w_out[j],
                        ab_ln_g[j], ab_ln_b[j], tm, ts, tq)
        else:
            x = _layer1(x, ssd_w_in[j], ssd_conv_w[j], ssd_conv_b[j], ssd_dt_bias[j],
                        ssd_a_log[j], ssd_d[j], ssd_norm[j], ssd_w_out[j], ssd_ln_g[j],
                        ssd_ln_b[j], tm_ssd, n_chunks)
    return x
```

```python
import functools
import math

import jax
import jax.numpy as jnp
from jax import lax
from jax.experimental import pallas as pl
from jax.experimental.pallas import tpu as pltpu

D_MODEL = 1024
DEPTH = 2
DN_ALPHA = (2.0 * DEPTH) ** 0.25

RNN_WIDTH = 512
RNN_HEADS = 8
RNN_HEAD_DIM = RNN_WIDTH // RNN_HEADS
CONV_WIDTH = 4
RG_C = 8.0

MLA_HEADS = 8
MLA_NOPE = 64
MLA_ROPE = 32
MLA_V = 64
MLA_Q_RANK = 256
MLA_KV_RANK = 128
MLA_WIDTH = MLA_HEADS * MLA_V
ROPE_THETA = 10000.0
AB_WIDTH = RNN_WIDTH + MLA_WIDTH

SSD_INNER = 2048
SSD_HEAD_DIM = 64
SSD_HEADS = 32
SSD_GROUPS = 4
SSD_STATE = 128
SSD_CHUNK = 128
SSD_CONV_DIM = SSD_INNER + 2 * SSD_GROUPS * SSD_STATE
SSD_GROUP_WIDTH = SSD_INNER // SSD_GROUPS

LANES = 128
HEAD_PAD = LANES
QK_PAD_WIDTH = MLA_HEADS * HEAD_PAD
VMEM_LIMIT = 56 * 1024 * 1024

BF16 = jnp.bfloat16
F32 = jnp.float32


def _sigmoid(x):
    return 1.0 / (1.0 + jnp.exp(-x))


def _silu(x):
    return x * _sigmoid(x)


def _log1p(y):
    u = 1.0 + y
    return jnp.where(u == 1.0, y, jnp.log(u) * (y / (u - 1.0)))


def _softplus(x):
    return jnp.maximum(x, 0.0) + _log1p(jnp.exp(-jnp.abs(x)))


def _dot(a, b):
    return jnp.dot(a, b, preferred_element_type=F32)


def _const_spec(shape):
    zeros = (0,) * len(shape)
    return pl.BlockSpec(shape, lambda *_: zeros)


def _params(semantics, flags=None):
    return pltpu.CompilerParams(dimension_semantics=semantics,
                                vmem_limit_bytes=VMEM_LIMIT, flags=flags)


def _ab_in_kernel(x_ref, pos_ref, w_in_ref, freq_ref, qn_ref, kvn_ref, wq_ref,
                  wk_ref, wv_ref, vones_ref,
                  xr_ref, gate_ref, q_ref, k_ref, v_ref):
    xb = x_ref[0].astype(BF16)
    proj = _dot(xb, w_in_ref[...])
    o_gate = RNN_WIDTH
    o_cq = o_gate + AB_WIDTH
    o_ckv = o_cq + MLA_Q_RANK
    o_kr = o_ckv + MLA_KV_RANK

    xr_ref[0] = proj[:, :o_gate]
    gate_ref[0] = _silu(proj[:, o_gate:o_cq]).astype(BF16)

    ang = pos_ref[0] * freq_ref[...]
    cos = jnp.cos(ang)
    sin = jnp.sin(ang)
    to_rope_lanes = HEAD_PAD - MLA_ROPE

    c_q = proj[:, o_cq:o_ckv]
    c_q = c_q * lax.rsqrt(jnp.mean(c_q * c_q, axis=-1, keepdims=True) + 1e-6) * qn_ref[...]
    q_all = _dot(c_q.astype(BF16), wq_ref[...])
    scale = (MLA_NOPE + MLA_ROPE) ** -0.5 * math.log2(math.e)
    for h in range(MLA_HEADS):
        q_h = q_all[:, h * HEAD_PAD:(h + 1) * HEAD_PAD]
        q_h = q_h * cos + pltpu.roll(q_h, to_rope_lanes, axis=1) * sin
        q_ref[0, :, h * HEAD_PAD:(h + 1) * HEAD_PAD] = (q_h * scale).astype(BF16)

    c_kv = proj[:, o_ckv:o_kr]
    c_kv = c_kv * lax.rsqrt(jnp.mean(c_kv * c_kv, axis=-1, keepdims=True) + 1e-6) * kvn_ref[...]
    c_kvb = c_kv.astype(BF16)
    kr = proj[:, o_kr:o_kr + HEAD_PAD]
    lane = lax.broadcasted_iota(jnp.int32, kr.shape, 1)
    k_rope = jnp.where(lane < MLA_NOPE + MLA_ROPE,
                       kr * cos + pltpu.roll(kr, to_rope_lanes, axis=1) * sin, 0.0)
    k = _dot(c_kvb, wk_ref[...]) + jnp.concatenate([k_rope] * MLA_HEADS, axis=1)
    k_ref[0] = k.astype(BF16)
    v = _dot(c_kvb, wv_ref[...]) + vones_ref[...]
    v_ref[0] = v.astype(BF16)


def _ab_in_call(x, pos_f, w_in_ext, freq, q_norm, kv_norm, wq, wk, wv, vones, tm):
    bsz, seqlen, _ = x.shape
    n_in = w_in_ext.shape[1]
    grid = (bsz, seqlen // tm)
    tok = lambda width: pl.BlockSpec((1, tm, width), lambda b, i: (b, i, 0))
    return pl.pallas_call(
        _ab_in_kernel,
        grid=grid,
        in_specs=[tok(D_MODEL), tok(1), _const_spec((D_MODEL, n_in)), _const_spec((1, HEAD_PAD)),
                  _const_spec((1, MLA_Q_RANK)), _const_spec((1, MLA_KV_RANK)),
                  _const_spec((MLA_Q_RANK, QK_PAD_WIDTH)),
                  _const_spec((MLA_KV_RANK, QK_PAD_WIDTH)), _const_spec((MLA_KV_RANK, QK_PAD_WIDTH)),
                  _const_spec((1, QK_PAD_WIDTH))],
        out_specs=[tok(RNN_WIDTH), tok(AB_WIDTH), tok(QK_PAD_WIDTH), tok(QK_PAD_WIDTH),
                   tok(QK_PAD_WIDTH)],
        out_shape=[jax.ShapeDtypeStruct((bsz, seqlen, RNN_WIDTH), F32),
                   jax.ShapeDtypeStruct((bsz, seqlen, AB_WIDTH), BF16),
                   jax.ShapeDtypeStruct((bsz, seqlen, QK_PAD_WIDTH), BF16),
                   jax.ShapeDtypeStruct((bsz, seqlen, QK_PAD_WIDTH), BF16),
                   jax.ShapeDtypeStruct((bsz, seqlen, QK_PAD_WIDTH), BF16)],
        compiler_params=_params(("parallel", "parallel")),
        name="ab_in_proj",
    )(x, pos_f, w_in_ext, freq, q_norm, kv_norm, wq, wk, wv, vones)


def _rglru_kernel(xr_ref, conv_w_ref, conv_b_ref, w_gate_ref, b_gate_ref, lam_ref,
                  h_ref, tail_ref, carry_ref, a_ref, u_ref):
    bsz, ts, width = xr_ref.shape

    @pl.when(pl.program_id(0) == 0)
    def _():
        tail_ref[...] = jnp.zeros_like(tail_ref)
        carry_ref[...] = jnp.zeros_like(carry_ref)

    x_blk = jnp.swapaxes(xr_ref[...], 0, 1)
    x_ext = jnp.concatenate([tail_ref[...], x_blk], axis=0)
    tail_ref[...] = x_blk[ts - (CONV_WIDTH - 1):]
    xc = conv_b_ref[...][None]
    for k in range(CONV_WIDTH):
        xc = xc + x_ext[k:k + ts] * conv_w_ref[k:k + 1, :][None]

    xc2 = xc.reshape(ts * bsz, width)
    gates = _dot(xc2.astype(BF16), w_gate_ref[...]) + b_gate_ref[...]
    r = _sigmoid(gates[:, :width])
    i = _sigmoid(gates[:, width:])
    log_a = (-RG_C * r) * _softplus(-lam_ref[...])
    a = jnp.exp(log_a)
    mult = jnp.sqrt(-jnp.tanh(log_a) * (a * a + 1.0))
    u = mult * (i * xc2)
    a_ref[...] = a.reshape(ts, bsz, width)
    u_ref[...] = u.reshape(ts, bsz, width)

    def step(t, h):
        h = a_ref[t] * h + u_ref[t]
        u_ref[t] = h
        return h

    carry_ref[...] = lax.fori_loop(0, ts, step, carry_ref[...], unroll=8)
    h_ref[...] = jnp.swapaxes(u_ref[...], 0, 1)


def _rglru_call(xr, conv_w, conv_b, w_gate, b_gate, lam, ts):
    bsz, seqlen, width = xr.shape
    blk = pl.BlockSpec((bsz, ts, width), lambda i: (0, i, 0))
    return pl.pallas_call(
        _rglru_kernel,
        grid=(seqlen // ts,),
        in_specs=[blk, _const_spec((CONV_WIDTH, width)), _const_spec((1, width)),
                  _const_spec((width, 2 * width)), _const_spec((1, 2 * width)),
                  _const_spec((1, width))],
        out_specs=blk,
        out_shape=jax.ShapeDtypeStruct((bsz, seqlen, width), F32),
        scratch_shapes=[pltpu.VMEM((CONV_WIDTH - 1, bsz, width), F32),
                        pltpu.VMEM((bsz, width), F32),
                        pltpu.VMEM((ts, bsz, width), F32),
                        pltpu.VMEM((ts, bsz, width), F32)],
        compiler_params=_params(("arbitrary",)),
        name="rglru_scan",
    )(xr, conv_w, conv_b, w_gate, b_gate, lam)


ATTN_ROWS = 256


def _mla_attn_tiles(q_ref, k_ref, v_ref, o_ref, s_refs, tiles, tq):
    heads = q_ref.shape[2] // HEAD_PAD
    halves = tq // ATTN_ROWS
    units = [(h, r) for h in range(heads) for r in range(halves)]
    steps = [(qi, j) for qi in tiles for j in range(qi + 1)]

    def width(qi, j, r):
        return (r + 1) * ATTN_ROWS if j == qi else tq

    def scores_into(dst_ref, qi, j):
        for u, (h, r) in enumerate(units):
            w = width(qi, j, r)
            q_u = q_ref[0, qi * tq + r * ATTN_ROWS:qi * tq + (r + 1) * ATTN_ROWS,
                        h * HEAD_PAD:(h + 1) * HEAD_PAD]
            k_blk = k_ref[0, j * tq:j * tq + w, h * HEAD_PAD:(h + 1) * HEAD_PAD]
            dst_ref[u, :, :w] = lax.dot_general(q_u, k_blk, (((1,), (1,)), ((), ())),
                                                preferred_element_type=F32)

    scores_into(s_refs[0], *steps[0])
    ms, accs = None, None
    for i, (qi, j) in enumerate(steps):
        src_ref = s_refs[i % 2]
        if i + 1 < len(steps):
            scores_into(s_refs[(i + 1) % 2], *steps[i + 1])
        if j == 0:
            ms = [jnp.full((ATTN_ROWS, 1), -1e30, F32) for _ in units]
            accs = [jnp.zeros((ATTN_ROWS, HEAD_PAD), F32) for _ in units]
        for u, (h, r) in enumerate(units):
            w = width(qi, j, r)
            if j == qi:
                col = lax.broadcasted_iota(jnp.int32, (ATTN_ROWS, ATTN_ROWS), 1)
                row = lax.broadcasted_iota(jnp.int32, (ATTN_ROWS, ATTN_ROWS), 0)
                edge = slice(w - ATTN_ROWS, w)
                src_ref[u, :, edge] = jnp.where(col <= row, src_ref[u, :, edge], -1e30)
            v_blk = v_ref[0, j * tq:j * tq + w, h * HEAD_PAD:(h + 1) * HEAD_PAD]
            m_new = jnp.maximum(ms[u], jnp.max(src_ref[u, :, :w], axis=-1, keepdims=True))
            p = jnp.exp2((src_ref[u, :, :w] - m_new).astype(BF16))
            accs[u] = jnp.exp2(ms[u] - m_new) * accs[u] + _dot(p, v_blk)
            ms[u] = m_new
        if j == qi:
            lane = lax.broadcasted_iota(jnp.int32, (ATTN_ROWS, HEAD_PAD), 1)
            outs = [acc / acc[:, MLA_V:MLA_V + 1] for acc in accs]
            for pair in range(heads // 2):
                for r in range(halves):
                    o_pair = jnp.where(lane < MLA_V, outs[2 * pair * halves + r],
                                       pltpu.roll(outs[(2 * pair + 1) * halves + r], MLA_V, axis=1))
                    o_ref[0, qi * tq + r * ATTN_ROWS:qi * tq + (r + 1) * ATTN_ROWS,
                          pair * HEAD_PAD:(pair + 1) * HEAD_PAD] = o_pair.astype(o_ref.dtype)


def _mla_attn_kernel(q_ref, k_ref, v_ref, o_ref, s0_ref, s1_ref, *, tq, groups):
    g = pl.program_id(2)
    for idx, tiles in enumerate(groups):
        @pl.when(g == idx)
        def _(tiles=tiles):
            _mla_attn_tiles(q_ref, k_ref, v_ref, o_ref, (s0_ref, s1_ref), tiles, tq)


def _mla_attn_call(q, k, v, tq, heads_per_step=2):
    bsz, seqlen, _ = q.shape
    width = heads_per_step * HEAD_PAD
    n_hsteps = MLA_HEADS // heads_per_step
    out_w = heads_per_step * MLA_V
    n_units = heads_per_step * (tq // ATTN_ROWS)
    n_q = seqlen // tq
    groups = tuple(tuple(sorted({i, n_q - 1 - i})) for i in range((n_q + 1) // 2))
    resident = lambda w: pl.BlockSpec((1, seqlen, w), lambda b, h, g: (b, 0, h))
    return pl.pallas_call(
        functools.partial(_mla_attn_kernel, tq=tq, groups=groups),
        grid=(bsz, n_hsteps, len(groups)),
        in_specs=[resident(width), resident(width), resident(width)],
        out_specs=resident(out_w),
        out_shape=jax.ShapeDtypeStruct((bsz, seqlen, MLA_WIDTH), BF16),
        scratch_shapes=[pltpu.VMEM((n_units, ATTN_ROWS, tq), F32),
                        pltpu.VMEM((n_units, ATTN_ROWS, tq), F32)],
        compiler_params=_params(("parallel", "parallel", "arbitrary")),
        name="mla_attention",
    )(q, k, v)


def _layernorm(z, g, b):
    mu = jnp.mean(z, axis=-1, keepdims=True)
    zc = z - mu
    var = jnp.mean(zc * zc, axis=-1, keepdims=True)
    return zc * lax.rsqrt(var + 1e-5) * g + b


def _ab_out_kernel(h_ref, ya_ref, gate_ref, x_ref, w_ref, g_ref, b_ref, o_ref):
    gate = gate_ref[0].astype(F32)
    y_rnn = (h_ref[0] * gate[:, :RNN_WIDTH]).astype(BF16)
    y_mla = (ya_ref[0].astype(F32) * gate[:, RNN_WIDTH:]).astype(BF16)
    y = _dot(y_rnn, w_ref[:RNN_WIDTH, :]) + _dot(y_mla, w_ref[RNN_WIDTH:, :])
    o_ref[0] = _layernorm(DN_ALPHA * x_ref[0] + y, g_ref[...], b_ref[...])


def _ab_out_call(h, y_mla, gate, x, w_out, ln_g, ln_b, tm):
    bsz, seqlen, _ = x.shape
    tok = lambda width: pl.BlockSpec((1, tm, width), lambda b, i: (b, i, 0))
    return pl.pallas_call(
        _ab_out_kernel,
        grid=(bsz, seqlen // tm),
        in_specs=[tok(RNN_WIDTH), tok(MLA_WIDTH), tok(AB_WIDTH), tok(D_MODEL),
                  _const_spec((AB_WIDTH, D_MODEL)), _const_spec((1, D_MODEL)),
                  _const_spec((1, D_MODEL))],
        out_specs=tok(D_MODEL),
        out_shape=jax.ShapeDtypeStruct((bsz, seqlen, D_MODEL), F32),
        compiler_params=_params(("parallel", "parallel")),
        name="ab_out_proj",
    )(h, y_mla, gate, x, w_out, ln_g, ln_b)


SSD_IN_COLS = 512
SUBLANES = 8


BF16_ROWS = 16


def _conv_silu_store(proj, tail, w, b, out_ref, cols):
    n_rows = proj.shape[0]
    taps = [w[k:k + 1, :] for k in range(CONV_WIDTH)]
    first = lax.broadcasted_iota(jnp.int32, tail.shape, 0) == 0
    prev = None
    done = []
    for i in range(-1, n_rows // SUBLANES):
        x = tail if i < 0 else proj[i * SUBLANES:(i + 1) * SUBLANES]
        acc = x * taps[0]
        rolled = []
        for k in range(1, CONV_WIDTH):
            r = pltpu.roll(acc, 1, axis=0)
            rolled.append(r)
            acc = (r if prev is None else jnp.where(first, prev[k - 1], r)) + x * taps[k]
        prev = rolled
        if i >= 0:
            done.append(acc + b)
        if len(done) == BF16_ROWS // SUBLANES:
            top = (i + 1) * SUBLANES
            out_ref[0, top - BF16_ROWS:top, cols] = _silu(
                jnp.concatenate(done, axis=0)).astype(BF16)
            done = []


SSD_PROJ_WIDTH = SSD_INNER + SSD_CONV_DIM + LANES


def _ssd_in_stage(step, x_ref, w_ref, dt_bias_ref, conv_w_ref, conv_b_ref,
                  z_ref, xbc_ref, dt_ref, new_ref, old_ref, project=True, finish=True):
    tm = x_ref.shape[1]
    x_off = SSD_INNER
    if finish:
        tails = []
        for c in range(SSD_CONV_DIM // SSD_IN_COLS):
            cols = slice(x_off + c * SSD_IN_COLS, x_off + (c + 1) * SSD_IN_COLS)
            tails.append(jnp.where(step == 1, 0.0, new_ref[tm - SUBLANES:, cols]))

    if project:
        xb = x_ref[0].astype(BF16)
        for c in range(0, SSD_PROJ_WIDTH, SSD_IN_COLS):
            cols = slice(c, min(c + SSD_IN_COLS, SSD_PROJ_WIDTH))
            new_ref[:, cols] = _dot(xb, w_ref[:, cols])
    if not finish:
        return

    for c in range(SSD_INNER // SSD_IN_COLS):
        cols = slice(c * SSD_IN_COLS, (c + 1) * SSD_IN_COLS)
        z_ref[0, :, cols] = _silu(old_ref[:, cols]).astype(BF16)
    for c in range(SSD_CONV_DIM // SSD_IN_COLS):
        cols = slice(c * SSD_IN_COLS, (c + 1) * SSD_IN_COLS)
        p_cols = slice(x_off + c * SSD_IN_COLS, x_off + (c + 1) * SSD_IN_COLS)
        _conv_silu_store(old_ref[:, p_cols], tails[c], conv_w_ref[:, cols], conv_b_ref[:, cols],
                         xbc_ref, cols)
    dt_ref[0] = _softplus(old_ref[:, SSD_INNER + SSD_CONV_DIM:] + dt_bias_ref[...])


def _ssd_in_kernel(x_ref, w_ref, dt_bias_ref, conv_w_ref, conv_b_ref,
                   z_ref, xbc_ref, dt_ref, p0_ref, p1_ref):
    step = pl.program_id(1)
    last = pl.num_programs(1) - 1
    args = (step, x_ref, w_ref, dt_bias_ref, conv_w_ref, conv_b_ref, z_ref, xbc_ref, dt_ref)
    bufs = (p0_ref, p1_ref)

    @pl.when(step == 0)
    def _():
        p1_ref[p1_ref.shape[0] - SUBLANES:, :] = jnp.zeros((SUBLANES, p1_ref.shape[1]), F32)
        _ssd_in_stage(*args, p0_ref, p1_ref, finish=False)

    for parity in range(2):
        @pl.when((step > 0) & (step < last) & (step % 2 == parity))
        def _(parity=parity):
            _ssd_in_stage(*args, bufs[parity], bufs[1 - parity])

        @pl.when((step == last) & (step % 2 == parity))
        def _(parity=parity):
            _ssd_in_stage(*args, bufs[parity], bufs[1 - parity], project=False)


def _ssd_in_call(x, w_all, dt_bias, conv_w, conv_b, tm):
    bsz, seqlen, _ = x.shape
    n_tiles = seqlen // tm
    tile_in = pl.BlockSpec((1, tm, D_MODEL), lambda b, i: (b, jnp.minimum(i, n_tiles - 1), 0))
    tile_out = lambda width: pl.BlockSpec((1, tm, width),
                                          lambda b, i: (b, jnp.maximum(i - 1, 0), 0))
    return pl.pallas_call(
        _ssd_in_kernel,
        grid=(bsz, n_tiles + 1),
        in_specs=[tile_in, _const_spec((D_MODEL, SSD_PROJ_WIDTH)),
                  _const_spec((1, LANES)), _const_spec((CONV_WIDTH, SSD_CONV_DIM)),
                  _const_spec((1, SSD_CONV_DIM))],
        out_specs=[tile_out(SSD_INNER), tile_out(SSD_CONV_DIM), tile_out(LANES)],
        out_shape=[jax.ShapeDtypeStruct((bsz, seqlen, SSD_INNER), BF16),
                   jax.ShapeDtypeStruct((bsz, seqlen, SSD_CONV_DIM), BF16),
                   jax.ShapeDtypeStruct((bsz, seqlen, LANES), F32)],
        scratch_shapes=[pltpu.VMEM((tm, SSD_PROJ_WIDTH), F32),
                        pltpu.VMEM((tm, SSD_PROJ_WIDTH), F32)],
        compiler_params=_params(("parallel", "arbitrary")),
        name="ssd_in_proj",
    )(x, w_all, dt_bias, conv_w, conv_b)


def _cumsum_rows(x):
    n = x.shape[0]
    row = lax.broadcasted_iota(jnp.int32, x.shape, 0)
    shift = 1
    while shift < n:
        x = x + jnp.where(row >= shift, pltpu.roll(x, shift, axis=0), 0.0)
        shift *= 2
    return x


LOG2E = math.log2(math.e)


def _ssd_chunk(xbc_ref, rows, dt, state_ref, a_neg2, d_skip_x):
    L = SSD_CHUNK
    gw = SSD_GROUPS * SSD_STATE

    cs = _cumsum_rows(dt * a_neg2)
    cs_t = cs.T
    dt_t = dt.T
    w_t = dt_t * jnp.exp2(cs_t[:, L - 1:L] - cs_t)

    row = lax.broadcasted_iota(jnp.int32, (L, L), 0)
    col = lax.broadcasted_iota(jnp.int32, (L, L), 1)
    tril = col <= row
    lane = lax.broadcasted_iota(jnp.int32, (L, LANES), 1)
    low = lane < SSD_HEAD_DIM
    keep_lo = low.astype(F32).astype(BF16)
    keep_hi = 1.0 - keep_lo

    hpg = SSD_HEADS // SSD_GROUPS
    y_parts = []
    for g in range(SSD_GROUPS):
        b_g = xbc_ref[0, rows, SSD_INNER + g * SSD_STATE:SSD_INNER + (g + 1) * SSD_STATE]
        c_g = xbc_ref[0, rows, SSD_INNER + gw + g * SSD_STATE:SSD_INNER + gw + (g + 1) * SSD_STATE]
        cb = lax.dot_general(c_g, b_g, (((1,), (1,)), ((), ())),
                             preferred_element_type=F32)
        cb = jnp.where(tril, cb, 0.0)
        b_t = b_g.astype(F32).T
        glanes = slice(g * SSD_GROUP_WIDTH, (g + 1) * SSD_GROUP_WIDTH)
        y_off = _dot(c_g, state_ref[:, glanes].astype(BF16))
        for pair in range(hpg // 2):
            h0 = g * hpg + 2 * pair
            plane = slice(h0 * SSD_HEAD_DIM, (h0 + 2) * SSD_HEAD_DIM)
            tops, bots, decays = [], [], []
            for h in (h0, h0 + 1):
                cs_col = jnp.broadcast_to(cs[:, h:h + 1], (L, LANES))
                seg = cs_col - cs_t[h:h + 1, :]
                m_h = jnp.exp2(jnp.minimum(seg, 0.0)) * (cb * dt_t[h:h + 1, :])
                tops.append(m_h.astype(BF16))
                bots.append((b_t * w_t[h:h + 1, :]).astype(BF16))
                decays.append(jnp.exp2(cs_col))
            lhs = jnp.concatenate([jnp.concatenate(tops, axis=1),
                                   jnp.concatenate(bots, axis=1)], axis=0)
            x_pair = xbc_ref[0, rows, plane]
            rhs = jnp.concatenate([x_pair * keep_lo, x_pair * keep_hi], axis=0)
            res = _dot(lhs, rhs)
            dec = jnp.where(low, decays[0], decays[1])
            off = pair * 2 * SSD_HEAD_DIM
            y_pair = (res[:L] + dec * y_off[:, off:off + 2 * SSD_HEAD_DIM]
                      + d_skip_x[:, plane] * x_pair.astype(F32))
            y_parts.append(y_pair)
            state_ref[:, plane] = state_ref[:, plane] * dec[L - 1:L, :] + res[L:]
    return jnp.concatenate(y_parts, axis=1)


def _ssd_kernel(z_ref, xbc_ref, dt_ref, x_ref, a_log_ref, d_skip_ref, norm_ref, w_out_ref,
                g_ref, b_ref, o_ref, state_ref, *, n_chunks):
    L = SSD_CHUNK

    @pl.when(pl.program_id(1) == 0)
    def _():
        state_ref[...] = jnp.zeros_like(state_ref)

    a_neg2 = -jnp.exp(a_log_ref[...]) * LOG2E
    for c in range(n_chunks):
        rows = slice(c * L, (c + 1) * L)
        y = _ssd_chunk(xbc_ref, rows, dt_ref[0, rows, :], state_ref, a_neg2, d_skip_ref[...])
        y = y * z_ref[0, rows, :].astype(F32)
        parts = []
        for g in range(SSD_GROUPS):
            yg = y[:, g * SSD_GROUP_WIDTH:(g + 1) * SSD_GROUP_WIDTH]
            parts.append(yg * lax.rsqrt(jnp.mean(yg * yg, axis=-1, keepdims=True) + 1e-6))
        yn = (jnp.concatenate(parts, axis=1) * norm_ref[...]).astype(BF16)
        out = _dot(yn, w_out_ref[...])
        o_ref[0, rows, :] = _layernorm(DN_ALPHA * x_ref[0, rows, :] + out, g_ref[...], b_ref[...])


def _ssd_call(z, xbc, dt, x, a_log, d_skip_x, norm_w, w_out, ln_g, ln_b, n_chunks):
    bsz, seqlen, _ = x.shape
    tm = n_chunks * SSD_CHUNK
    tok = lambda width: pl.BlockSpec((1, tm, width), lambda b, i: (b, i, 0))
    return pl.pallas_call(
        functools.partial(_ssd_kernel, n_chunks=n_chunks),
        grid=(bsz, seqlen // tm),
        in_specs=[tok(SSD_INNER), tok(SSD_CONV_DIM), tok(LANES), tok(D_MODEL),
                  _const_spec((1, LANES)), _const_spec((1, SSD_INNER)),
                  _const_spec((1, SSD_INNER)), _const_spec((SSD_INNER, D_MODEL)),
                  _const_spec((1, D_MODEL)), _const_spec((1, D_MODEL))],
        out_specs=tok(D_MODEL),
        out_shape=jax.ShapeDtypeStruct((bsz, seqlen, D_MODEL), F32),
        scratch_shapes=[pltpu.VMEM((SSD_STATE, SSD_INNER), F32)],
        compiler_params=_params(("parallel", "arbitrary")),
        name="ssd_scan_out",
    )(z, xbc, dt, x, a_log, d_skip_x, norm_w, w_out, ln_g, ln_b)


def _rot_cols(w):
    half = MLA_ROPE // 2
    return jnp.concatenate([-w[..., half:], w[..., :half]], axis=-1)


def _pad_heads(w_heads):
    r, h, c = w_heads.shape
    return jnp.pad(w_heads, ((0, 0), (0, 0), (0, HEAD_PAD - c))).reshape(r, h * HEAD_PAD)


def _block_diag(w):
    h, d, _ = w.shape
    eye = jnp.eye(h, dtype=w.dtype)
    return (eye[:, None, :, None] * w[:, :, None, :]).reshape(h * d, h * d)


def _layer0(x, pos_f, freq, w_in, conv_w, conv_b, gate_a_w, gate_a_b, gate_x_w, gate_x_b, lam,
            q_norm, kv_norm, w_uq, w_ukv, w_out, ln_g, ln_b, tm, ts, tq):
    o_kr = RNN_WIDTH + AB_WIDTH + MLA_Q_RANK + MLA_KV_RANK
    w_kr = w_in[:, o_kr:]
    w_in_ext = jnp.concatenate([w_in[:, :o_kr], jnp.zeros_like(w_in[:, :MLA_NOPE]), w_kr,
                                _rot_cols(w_kr)], axis=1).astype(BF16)

    uq = w_uq.reshape(MLA_Q_RANK, MLA_HEADS, MLA_NOPE + MLA_ROPE)
    wq = _pad_heads(jnp.concatenate([uq, _rot_cols(uq[..., MLA_NOPE:])], axis=-1)).astype(BF16)
    ukv = w_ukv.reshape(MLA_KV_RANK, MLA_HEADS, MLA_NOPE + MLA_V)
    wk = _pad_heads(ukv[..., :MLA_NOPE]).astype(BF16)
    wv = _pad_heads(ukv[..., MLA_NOPE:]).astype(BF16)
    vones = jnp.tile((jnp.arange(HEAD_PAD) >= MLA_V).astype(F32), MLA_HEADS)[None]

    xr, gate, q, k, v = _ab_in_call(x, pos_f, w_in_ext, freq, q_norm[None], kv_norm[None],
                                    wq, wk, wv, vones, tm)

    w_gate = jnp.concatenate([_block_diag(gate_a_w), _block_diag(gate_x_w)], axis=1).astype(BF16)
    b_gate = jnp.concatenate([gate_a_b, gate_x_b])[None]
    h = _rglru_call(xr, conv_w, conv_b[None], w_gate, b_gate, lam[None], ts)

    y_mla = _mla_attn_call(q, k, v, tq)
    return _ab_out_call(h, y_mla, gate, x, w_out.astype(BF16), ln_g[None], ln_b[None], tm)


def _layer1(x, w_in, conv_w, conv_b, dt_bias, a_log, d_skip, norm_w, w_out, ln_g, ln_b,
            tm, n_chunks):
    pad_h = lambda a: jnp.pad(a, ((0, 0), (0, LANES - SSD_HEADS)))
    w_all = pad_h(w_in).astype(BF16)
    z, xbc, dt = _ssd_in_call(x, w_all, pad_h(dt_bias[None]), conv_w, conv_b[None], tm)
    d_skip_x = jnp.repeat(d_skip, SSD_HEAD_DIM)[None]
    return _ssd_call(z, xbc, dt, x, pad_h(a_log[None]), d_skip_x, norm_w[None],
                     w_out.astype(BF16), ln_g[None], ln_b[None], n_chunks)


def kernel(x, positions, ab_w_in, ab_conv_w, ab_conv_b, ab_gate_a_w, ab_gate_a_b, ab_gate_x_w,
           ab_gate_x_b, ab_lambda, mla_q_norm, mla_kv_norm, mla_w_uq, mla_w_ukv, ab_w_out,
           ab_ln_g, ab_ln_b, ssd_w_in, ssd_conv_w, ssd_conv_b, ssd_dt_bias, ssd_a_log, ssd_d,
           ssd_norm, ssd_w_out, ssd_ln_g, ssd_ln_b):
    seqlen = x.shape[1]
    tm = min(512, seqlen)
    tm_ssd = min(256, seqlen)
    ts = min(128, seqlen)
    tq = min(512, seqlen)
    n_chunks = min(4, seqlen // SSD_CHUNK)

    inv_freq = ROPE_THETA ** (-jnp.arange(0, MLA_ROPE, 2, dtype=F32) / MLA_ROPE)
    freq = jnp.concatenate([jnp.zeros((MLA_NOPE,), F32), inv_freq, inv_freq,
                            jnp.zeros((HEAD_PAD - MLA_NOPE - MLA_ROPE,), F32)])[None]
    pos_f = positions.astype(F32)[..., None]

    for layer in range(DEPTH):
        j = layer // 2
        if layer % 2 == 0:
            x = _layer0(x, pos_f, freq, ab_w_in[j], ab_conv_w[j], ab_conv_b[j], ab_gate_a_w[j],
                        ab_gate_a_b[j], ab_gate_x_w[j], ab_gate_x_b[j], ab_lambda[j],
                        mla_q_norm[j], mla_kv_norm[j], mla_w_uq[j], mla_w_ukv[j], ab_w_out[j],
                        ab_ln_g[j], ab_ln_b[j], tm, ts, tq)
        else:
            x = _layer1(x, ssd_w_in[j], ssd_conv_w[j], ssd_conv_b[j], ssd_dt_bias[j],
                        ssd_a_log[j], ssd_d[j], ssd_norm[j], ssd_w_out[j], ssd_ln_g[j],
                        ssd_ln_b[j], tm_ssd, n_chunks)
    return x
```

```python
import functools
import math

import jax
import jax.numpy as jnp
from jax import lax
from jax.experimental import pallas as pl
from jax.experimental.pallas import tpu as pltpu

D_MODEL = 1024
DEPTH = 2
DN_ALPHA = (2.0 * DEPTH) ** 0.25

RNN_WIDTH = 512
RNN_HEADS = 8
RNN_HEAD_DIM = RNN_WIDTH // RNN_HEADS
CONV_WIDTH = 4
RG_C = 8.0

MLA_HEADS = 8
MLA_NOPE = 64
MLA_ROPE = 32
MLA_V = 64
MLA_Q_RANK = 256
MLA_KV_RANK = 128
MLA_WIDTH = MLA_HEADS * MLA_V
ROPE_THETA = 10000.0
AB_WIDTH = RNN_WIDTH + MLA_WIDTH

SSD_INNER = 2048
SSD_HEAD_DIM = 64
SSD_HEADS = 32
SSD_GROUPS = 4
SSD_STATE = 128
SSD_CHUNK = 128
SSD_CONV_DIM = SSD_INNER + 2 * SSD_GROUPS * SSD_STATE
SSD_GROUP_WIDTH = SSD_INNER // SSD_GROUPS

LANES = 128
HEAD_PAD = LANES
MLA_VT_ROWS = MLA_V + 16
QK_PAD_WIDTH = MLA_HEADS * HEAD_PAD
VMEM_LIMIT = 56 * 1024 * 1024

BF16 = jnp.bfloat16
F32 = jnp.float32


def _sigmoid(x):
    return 1.0 / (1.0 + jnp.exp(-x))


def _silu(x):
    return x * _sigmoid(x)


def _log1p(y):
    u = 1.0 + y
    return jnp.where(u == 1.0, y, jnp.log(u) * (y / (u - 1.0)))


def _softplus(x):
    return jnp.maximum(x, 0.0) + _log1p(jnp.exp(-jnp.abs(x)))


def _dot(a, b):
    return jnp.dot(a, b, preferred_element_type=F32)


def _const_spec(shape):
    zeros = (0,) * len(shape)
    return pl.BlockSpec(shape, lambda *_: zeros)


def _params(semantics, flags=None):
    return pltpu.CompilerParams(dimension_semantics=semantics,
                                vmem_limit_bytes=VMEM_LIMIT, flags=flags)


def _ab_in_kernel(x_ref, pos_ref, w_in_ref, freq_ref, qn_ref, kvn_ref, wq_ref,
                  wk_ref, wvt_ref, vones_ref,
                  xr_ref, gate_ref, q_ref, k_ref, vt_ref):
    xb = x_ref[0].astype(BF16)
    proj = _dot(xb, w_in_ref[...])
    o_gate = RNN_WIDTH
    o_cq = o_gate + AB_WIDTH
    o_ckv = o_cq + MLA_Q_RANK
    o_kr = o_ckv + MLA_KV_RANK

    xr_ref[0] = proj[:, :o_gate]
    gate_ref[0] = _silu(proj[:, o_gate:o_cq]).astype(BF16)

    ang = pos_ref[0] * freq_ref[...]
    cos = jnp.cos(ang)
    sin = jnp.sin(ang)
    to_rope_lanes = HEAD_PAD - MLA_ROPE

    c_q = proj[:, o_cq:o_ckv]
    c_q = c_q * lax.rsqrt(jnp.mean(c_q * c_q, axis=-1, keepdims=True) + 1e-6) * qn_ref[...]
    q_all = _dot(c_q.astype(BF16), wq_ref[...])
    scale = (MLA_NOPE + MLA_ROPE) ** -0.5 * math.log2(math.e)
    for h in range(MLA_HEADS):
        q_h = q_all[:, h * HEAD_PAD:(h + 1) * HEAD_PAD]
        q_h = q_h * cos + pltpu.roll(q_h, to_rope_lanes, axis=1) * sin
        q_ref[0, :, h * HEAD_PAD:(h + 1) * HEAD_PAD] = (q_h * scale).astype(BF16)

    c_kv = proj[:, o_ckv:o_kr]
    c_kv = c_kv * lax.rsqrt(jnp.mean(c_kv * c_kv, axis=-1, keepdims=True) + 1e-6) * kvn_ref[...]
    c_kvb = c_kv.astype(BF16)
    kr = proj[:, o_kr:o_kr + HEAD_PAD]
    lane = lax.broadcasted_iota(jnp.int32, kr.shape, 1)
    k_rope = jnp.where(lane < MLA_NOPE + MLA_ROPE,
                       kr * cos + pltpu.roll(kr, to_rope_lanes, axis=1) * sin, 0.0)
    k = _dot(c_kvb, wk_ref[...]) + jnp.concatenate([k_rope] * MLA_HEADS, axis=1)
    k_ref[0] = k.astype(BF16)
    vt = lax.dot_general(wvt_ref[...], c_kvb, (((1,), (1,)), ((), ())),
                         preferred_element_type=F32) + vones_ref[...]
    vt_ref[0] = vt.reshape(MLA_HEADS, MLA_VT_ROWS, vt.shape[1]).astype(BF16)


def _ab_in_call(x, pos_f, w_in_ext, freq, q_norm, kv_norm, wq, wk, wvt, vones, tm):
    bsz, seqlen, _ = x.shape
    n_in = w_in_ext.shape[1]
    grid = (bsz, seqlen // tm)
    tok = lambda width: pl.BlockSpec((1, tm, width), lambda b, i: (b, i, 0))
    vt_rows = MLA_HEADS * MLA_VT_ROWS
    return pl.pallas_call(
        _ab_in_kernel,
        grid=grid,
        in_specs=[tok(D_MODEL), tok(1), _const_spec((D_MODEL, n_in)), _const_spec((1, HEAD_PAD)),
                  _const_spec((1, MLA_Q_RANK)), _const_spec((1, MLA_KV_RANK)),
                  _const_spec((MLA_Q_RANK, QK_PAD_WIDTH)),
                  _const_spec((MLA_KV_RANK, QK_PAD_WIDTH)), _const_spec((vt_rows, MLA_KV_RANK)),
                  _const_spec((vt_rows, 1))],
        out_specs=[tok(RNN_WIDTH), tok(AB_WIDTH), tok(QK_PAD_WIDTH), tok(QK_PAD_WIDTH),
                   pl.BlockSpec((1, MLA_HEADS, MLA_VT_ROWS, tm), lambda b, i: (b, 0, 0, i))],
        out_shape=[jax.ShapeDtypeStruct((bsz, seqlen, RNN_WIDTH), F32),
                   jax.ShapeDtypeStruct((bsz, seqlen, AB_WIDTH), BF16),
                   jax.ShapeDtypeStruct((bsz, seqlen, QK_PAD_WIDTH), BF16),
                   jax.ShapeDtypeStruct((bsz, seqlen, QK_PAD_WIDTH), BF16),
                   jax.ShapeDtypeStruct((bsz, MLA_HEADS, MLA_VT_ROWS, seqlen), BF16)],
        compiler_params=_params(("parallel", "parallel")),
        name="ab_in_proj",
    )(x, pos_f, w_in_ext, freq, q_norm, kv_norm, wq, wk, wvt, vones)


def _rglru_kernel(xr_ref, conv_w_ref, conv_b_ref, w_gate_ref, b_gate_ref, lam_ref,
                  h_ref, tail_ref, carry_ref, a_ref, u_ref):
    bsz, ts, width = xr_ref.shape

    @pl.when(pl.program_id(0) == 0)
    def _():
        tail_ref[...] = jnp.zeros_like(tail_ref)
        carry_ref[...] = jnp.zeros_like(carry_ref)

    x_blk = jnp.swapaxes(xr_ref[...], 0, 1)
    x_ext = jnp.concatenate([tail_ref[...], x_blk], axis=0)
    tail_ref[...] = x_blk[ts - (CONV_WIDTH - 1):]
    xc = conv_b_ref[...][None]
    for k in range(CONV_WIDTH):
        xc = xc + x_ext[k:k + ts] * conv_w_ref[k:k + 1, :][None]

    xc2 = xc.reshape(ts * bsz, width)
    gates = _dot(xc2.astype(BF16), w_gate_ref[...]) + b_gate_ref[...]
    r = _sigmoid(gates[:, :width])
    i = _sigmoid(gates[:, width:])
    log_a = (-RG_C * r) * _softplus(-lam_ref[...])
    a = jnp.exp(log_a)
    mult = jnp.sqrt(-jnp.tanh(log_a) * (a * a + 1.0))
    u = mult * (i * xc2)
    a_ref[...] = a.reshape(ts, bsz, width)
    u_ref[...] = u.reshape(ts, bsz, width)

    def step(t, h):
        h = a_ref[t] * h + u_ref[t]
        u_ref[t] = h
        return h

    carry_ref[...] = lax.fori_loop(0, ts, step, carry_ref[...], unroll=8)
    h_ref[...] = jnp.swapaxes(u_ref[...], 0, 1)


def _rglru_call(xr, conv_w, conv_b, w_gate, b_gate, lam, ts):
    bsz, seqlen, width = xr.shape
    blk = pl.BlockSpec((bsz, ts, width), lambda i: (0, i, 0))
    return pl.pallas_call(
        _rglru_kernel,
        grid=(seqlen // ts,),
        in_specs=[blk, _const_spec((CONV_WIDTH, width)), _const_spec((1, width)),
                  _const_spec((width, 2 * width)), _const_spec((1, 2 * width)),
                  _const_spec((1, width))],
        out_specs=blk,
        out_shape=jax.ShapeDtypeStruct((bsz, seqlen, width), F32),
        scratch_shapes=[pltpu.VMEM((CONV_WIDTH - 1, bsz, width), F32),
                        pltpu.VMEM((bsz, width), F32),
                        pltpu.VMEM((ts, bsz, width), F32),
                        pltpu.VMEM((ts, bsz, width), F32)],
        compiler_params=_params(("arbitrary",)),
        name="rglru_scan",
    )(xr, conv_w, conv_b, w_gate, b_gate, lam)


ATTN_ROWS = 256


def _mla_attn_tiles(q_ref, k_ref, vt_ref, o_ref, s_refs, tiles, tq):
    heads = q_ref.shape[2] // HEAD_PAD
    halves = tq // ATTN_ROWS
    units = [(h, r) for h in range(heads) for r in range(halves)]
    steps = [(qi, j) for qi in tiles for j in range(qi + 1)]

    def depth(qi, j, r):
        return (r + 1) * ATTN_ROWS if j == qi else tq

    def scores_into(dst_ref, qi, j):
        for u, (h, r) in enumerate(units):
            w = depth(qi, j, r)
            q_u = q_ref[0, qi * tq + r * ATTN_ROWS:qi * tq + (r + 1) * ATTN_ROWS,
                        h * HEAD_PAD:(h + 1) * HEAD_PAD]
            k_blk = k_ref[0, j * tq:j * tq + w, h * HEAD_PAD:(h + 1) * HEAD_PAD]
            dst_ref[u, :w, :] = lax.dot_general(k_blk, q_u, (((1,), (1,)), ((), ())),
                                                preferred_element_type=F32)

    scores_into(s_refs[0], *steps[0])
    ms, accs = None, None
    for i, (qi, j) in enumerate(steps):
        src_ref = s_refs[i % 2]
        if i + 1 < len(steps):
            scores_into(s_refs[(i + 1) % 2], *steps[i + 1])
        if j == 0:
            ms = [jnp.full((1, ATTN_ROWS), -1e30, F32) for _ in units]
            accs = [jnp.zeros((MLA_VT_ROWS, ATTN_ROWS), F32) for _ in units]
        for u, (h, r) in enumerate(units):
            w = depth(qi, j, r)
            if j == qi:
                key = lax.broadcasted_iota(jnp.int32, (ATTN_ROWS, ATTN_ROWS), 0)
                qry = lax.broadcasted_iota(jnp.int32, (ATTN_ROWS, ATTN_ROWS), 1)
                edge = slice(w - ATTN_ROWS, w)
                src_ref[u, edge, :] = jnp.where(key <= qry, src_ref[u, edge, :], -1e30)
            vt_blk = vt_ref[0, h, :, j * tq:j * tq + w]
            m_new = jnp.maximum(ms[u], jnp.max(src_ref[u, :w, :], axis=0, keepdims=True))
            p = jnp.exp2((src_ref[u, :w, :] - m_new).astype(BF16))
            accs[u] = jnp.exp2(ms[u] - m_new) * accs[u] + _dot(vt_blk, p)
            ms[u] = m_new
        if j == qi:
            outs = [acc[:MLA_V] / acc[MLA_V:MLA_V + 1] for acc in accs]
            for pair in range(heads // 2):
                for r in range(halves):
                    o_t = jnp.concatenate([outs[2 * pair * halves + r],
                                           outs[(2 * pair + 1) * halves + r]], axis=0)
                    o_ref[0, qi * tq + r * ATTN_ROWS:qi * tq + (r + 1) * ATTN_ROWS,
                          pair * HEAD_PAD:(pair + 1) * HEAD_PAD] = o_t.T.astype(o_ref.dtype)


def _mla_attn_kernel(q_ref, k_ref, vt_ref, o_ref, s0_ref, s1_ref, *, tq, groups):
    g = pl.program_id(2)
    for idx, tiles in enumerate(groups):
        @pl.when(g == idx)
        def _(tiles=tiles):
            _mla_attn_tiles(q_ref, k_ref, vt_ref, o_ref, (s0_ref, s1_ref), tiles, tq)


def _mla_attn_call(q, k, vt, tq, heads_per_step=2):
    bsz, seqlen, _ = q.shape
    width = heads_per_step * HEAD_PAD
    n_hsteps = MLA_HEADS // heads_per_step
    out_w = heads_per_step * MLA_V
    n_units = heads_per_step * (tq // ATTN_ROWS)
    n_q = seqlen // tq
    groups = tuple(tuple(sorted({i, n_q - 1 - i})) for i in range((n_q + 1) // 2))
    resident = lambda w: pl.BlockSpec((1, seqlen, w), lambda b, h, g: (b, 0, h))
    return pl.pallas_call(
        functools.partial(_mla_attn_kernel, tq=tq, groups=groups),
        grid=(bsz, n_hsteps, len(groups)),
        in_specs=[resident(width), resident(width),
                  pl.BlockSpec((1, heads_per_step, MLA_VT_ROWS, seqlen),
                               lambda b, h, g: (b, h, 0, 0))],
        out_specs=resident(out_w),
        out_shape=jax.ShapeDtypeStruct((bsz, seqlen, MLA_WIDTH), BF16),
        scratch_shapes=[pltpu.VMEM((n_units, tq, ATTN_ROWS), F32),
                        pltpu.VMEM((n_units, tq, ATTN_ROWS), F32)],
        compiler_params=_params(("parallel", "parallel", "arbitrary")),
        name="mla_attention",
    )(q, k, vt)


def _layernorm(z, g, b):
    mu = jnp.mean(z, axis=-1, keepdims=True)
    zc = z - mu
    var = jnp.mean(zc * zc, axis=-1, keepdims=True)
    return zc * lax.rsqrt(var + 1e-5) * g + b


def _ab_out_kernel(h_ref, ya_ref, gate_ref, x_ref, w_ref, g_ref, b_ref, o_ref):
    gate = gate_ref[0].astype(F32)
    y_rnn = (h_ref[0] * gate[:, :RNN_WIDTH]).astype(BF16)
    y_mla = (ya_ref[0].astype(F32) * gate[:, RNN_WIDTH:]).astype(BF16)
    y = _dot(y_rnn, w_ref[:RNN_WIDTH, :]) + _dot(y_mla, w_ref[RNN_WIDTH:, :])
    o_ref[0] = _layernorm(DN_ALPHA * x_ref[0] + y, g_ref[...], b_ref[...])


def _ab_out_call(h, y_mla, gate, x, w_out, ln_g, ln_b, tm):
    bsz, seqlen, _ = x.shape
    tok = lambda width: pl.BlockSpec((1, tm, width), lambda b, i: (b, i, 0))
    return pl.pallas_call(
        _ab_out_kernel,
        grid=(bsz, seqlen // tm),
        in_specs=[tok(RNN_WIDTH), tok(MLA_WIDTH), tok(AB_WIDTH), tok(D_MODEL),
                  _const_spec((AB_WIDTH, D_MODEL)), _const_spec((1, D_MODEL)),
                  _const_spec((1, D_MODEL))],
        out_specs=tok(D_MODEL),
        out_shape=jax.ShapeDtypeStruct((bsz, seqlen, D_MODEL), F32),
        compiler_params=_params(("parallel", "parallel")),
        name="ab_out_proj",
    )(h, y_mla, gate, x, w_out, ln_g, ln_b)


SSD_IN_COLS = 512
SUBLANES = 8


BF16_ROWS = 16


def _conv_silu_store(proj, tail, w, b, out_ref, cols):
    n_rows = proj.shape[0]
    taps = [w[k:k + 1, :] for k in range(CONV_WIDTH)]
    first = lax.broadcasted_iota(jnp.int32, tail.shape, 0) == 0
    prev = None
    done = []
    for i in range(-1, n_rows // SUBLANES):
        x = tail if i < 0 else proj[i * SUBLANES:(i + 1) * SUBLANES]
        acc = x * taps[0]
        rolled = []
        for k in range(1, CONV_WIDTH):
            r = pltpu.roll(acc, 1, axis=0)
            rolled.append(r)
            acc = (r if prev is None else jnp.where(first, prev[k - 1], r)) + x * taps[k]
        prev = rolled
        if i >= 0:
            done.append(acc + b)
        if len(done) == BF16_ROWS // SUBLANES:
            top = (i + 1) * SUBLANES
            out_ref[0, top - BF16_ROWS:top, cols] = _silu(
                jnp.concatenate(done, axis=0)).astype(BF16)
            done = []


SSD_PROJ_WIDTH = SSD_INNER + SSD_CONV_DIM + LANES


def _ssd_in_stage(step, x_ref, w_ref, dt_bias_ref, conv_w_ref, conv_b_ref,
                  z_ref, xbc_ref, dt_ref, new_ref, old_ref, project=True, finish=True):
    tm = x_ref.shape[1]
    x_off = SSD_INNER
    if finish:
        tails = []
        for c in range(SSD_CONV_DIM // SSD_IN_COLS):
            cols = slice(x_off + c * SSD_IN_COLS, x_off + (c + 1) * SSD_IN_COLS)
            tails.append(jnp.where(step == 1, 0.0, new_ref[tm - SUBLANES:, cols]))

    if project:
        xb = x_ref[0].astype(BF16)
        for c in range(0, SSD_PROJ_WIDTH, SSD_IN_COLS):
            cols = slice(c, min(c + SSD_IN_COLS, SSD_PROJ_WIDTH))
            new_ref[:, cols] = _dot(xb, w_ref[:, cols])
    if not finish:
        return

    for c in range(SSD_INNER // SSD_IN_COLS):
        cols = slice(c * SSD_IN_COLS, (c + 1) * SSD_IN_COLS)
        z_ref[0, :, cols] = _silu(old_ref[:, cols]).astype(BF16)
    for c in range(SSD_CONV_DIM // SSD_IN_COLS):
        cols = slice(c * SSD_IN_COLS, (c + 1) * SSD_IN_COLS)
        p_cols = slice(x_off + c * SSD_IN_COLS, x_off + (c + 1) * SSD_IN_COLS)
        _conv_silu_store(old_ref[:, p_cols], tails[c], conv_w_ref[:, cols], conv_b_ref[:, cols],
                         xbc_ref, cols)
    dt_ref[0] = _softplus(old_ref[:, SSD_INNER + SSD_CONV_DIM:] + dt_bias_ref[...])


def _ssd_in_kernel(x_ref, w_ref, dt_bias_ref, conv_w_ref, conv_b_ref,
                   z_ref, xbc_ref, dt_ref, p0_ref, p1_ref):
    step = pl.program_id(1)
    last = pl.num_programs(1) - 1
    args = (step, x_ref, w_ref, dt_bias_ref, conv_w_ref, conv_b_ref, z_ref, xbc_ref, dt_ref)
    bufs = (p0_ref, p1_ref)

    @pl.when(step == 0)
    def _():
        p1_ref[p1_ref.shape[0] - SUBLANES:, :] = jnp.zeros((SUBLANES, p1_ref.shape[1]), F32)
        _ssd_in_stage(*args, p0_ref, p1_ref, finish=False)

    for parity in range(2):
        @pl.when((step > 0) & (step < last) & (step % 2 == parity))
        def _(parity=parity):
            _ssd_in_stage(*args, bufs[parity], bufs[1 - parity])

        @pl.when((step == last) & (step % 2 == parity))
        def _(parity=parity):
            _ssd_in_stage(*args, bufs[parity], bufs[1 - parity], project=False)


def _ssd_in_call(x, w_all, dt_bias, conv_w, conv_b, tm):
    bsz, seqlen, _ = x.shape
    n_tiles = seqlen // tm
    tile_in = pl.BlockSpec((1, tm, D_MODEL), lambda b, i: (b, jnp.minimum(i, n_tiles - 1), 0))
    tile_out = lambda width: pl.BlockSpec((1, tm, width),
                                          lambda b, i: (b, jnp.maximum(i - 1, 0), 0))
    return pl.pallas_call(
        _ssd_in_kernel,
        grid=(bsz, n_tiles + 1),
        in_specs=[tile_in, _const_spec((D_MODEL, SSD_PROJ_WIDTH)),
                  _const_spec((1, LANES)), _const_spec((CONV_WIDTH, SSD_CONV_DIM)),
                  _const_spec((1, SSD_CONV_DIM))],
        out_specs=[tile_out(SSD_INNER), tile_out(SSD_CONV_DIM), tile_out(LANES)],
        out_shape=[jax.ShapeDtypeStruct((bsz, seqlen, SSD_INNER), BF16),
                   jax.ShapeDtypeStruct((bsz, seqlen, SSD_CONV_DIM), BF16),
                   jax.ShapeDtypeStruct((bsz, seqlen, LANES), F32)],
        scratch_shapes=[pltpu.VMEM((tm, SSD_PROJ_WIDTH), F32),
                        pltpu.VMEM((tm, SSD_PROJ_WIDTH), F32)],
        compiler_params=_params(("parallel", "arbitrary")),
        name="ssd_in_proj",
    )(x, w_all, dt_bias, conv_w, conv_b)


def _cumsum_rows(x):
    n = x.shape[0]
    row = lax.broadcasted_iota(jnp.int32, x.shape, 0)
    shift = 1
    while shift < n:
        x = x + jnp.where(row >= shift, pltpu.roll(x, shift, axis=0), 0.0)
        shift *= 2
    return x


LOG2E = math.log2(math.e)


def _ssd_chunk(xbc_ref, rows, dt, state_ref, a_neg2, d_skip_x):
    L = SSD_CHUNK
    gw = SSD_GROUPS * SSD_STATE

    cs = _cumsum_rows(dt * a_neg2)
    cs_t = cs.T
    dt_t = dt.T
    w_t = dt_t * jnp.exp2(cs_t[:, L - 1:L] - cs_t)

    row = lax.broadcasted_iota(jnp.int32, (L, L), 0)
    col = lax.broadcasted_iota(jnp.int32, (L, L), 1)
    tril = col <= row
    lane = lax.broadcasted_iota(jnp.int32, (L, LANES), 1)
    low = lane < SSD_HEAD_DIM
    keep_lo = low.astype(F32).astype(BF16)
    keep_hi = 1.0 - keep_lo

    hpg = SSD_HEADS // SSD_GROUPS
    y_parts = []
    for g in range(SSD_GROUPS):
        b_g = xbc_ref[0, rows, SSD_INNER + g * SSD_STATE:SSD_INNER + (g + 1) * SSD_STATE]
        c_g = xbc_ref[0, rows, SSD_INNER + gw + g * SSD_STATE:SSD_INNER + gw + (g + 1) * SSD_STATE]
        cb = lax.dot_general(c_g, b_g, (((1,), (1,)), ((), ())),
                             preferred_element_type=F32)
        cb = jnp.where(tril, cb, 0.0)
        b_t = b_g.astype(F32).T
        glanes = slice(g * SSD_GROUP_WIDTH, (g + 1) * SSD_GROUP_WIDTH)
        y_off = _dot(c_g, state_ref[:, glanes].astype(BF16))
        for pair in range(hpg // 2):
            h0 = g * hpg + 2 * pair
            plane = slice(h0 * SSD_HEAD_DIM, (h0 + 2) * SSD_HEAD_DIM)
            tops, bots, decays = [], [], []
            for h in (h0, h0 + 1):
                cs_col = jnp.broadcast_to(cs[:, h:h + 1], (L, LANES))
                seg = cs_col - cs_t[h:h + 1, :]
                m_h = jnp.exp2(jnp.minimum(seg, 0.0)) * (cb * dt_t[h:h + 1, :])
                tops.append(m_h.astype(BF16))
                bots.append((b_t * w_t[h:h + 1, :]).astype(BF16))
                decays.append(jnp.exp2(cs_col))
            lhs = jnp.concatenate([jnp.concatenate(tops, axis=1),
                                   jnp.concatenate(bots, axis=1)], axis=0)
            x_pair = xbc_ref[0, rows, plane]
            rhs = jnp.concatenate([x_pair * keep_lo, x_pair * keep_hi], axis=0)
            res = _dot(lhs, rhs)
            dec = jnp.where(low, decays[0], decays[1])
            off = pair * 2 * SSD_HEAD_DIM
            y_pair = (res[:L] + dec * y_off[:, off:off + 2 * SSD_HEAD_DIM]
                      + d_skip_x[:, plane] * x_pair.astype(F32))
            y_parts.append(y_pair)
            state_ref[:, plane] = state_ref[:, plane] * dec[L - 1:L, :] + res[L:]
    return jnp.concatenate(y_parts, axis=1)


def _ssd_kernel(z_ref, xbc_ref, dt_ref, x_ref, a_log_ref, d_skip_ref, norm_ref, w_out_ref,
                g_ref, b_ref, o_ref, state_ref, *, n_chunks):
    L = SSD_CHUNK

    @pl.when(pl.program_id(1) == 0)
    def _():
        state_ref[...] = jnp.zeros_like(state_ref)

    a_neg2 = -jnp.exp(a_log_ref[...]) * LOG2E
    for c in range(n_chunks):
        rows = slice(c * L, (c + 1) * L)
        y = _ssd_chunk(xbc_ref, rows, dt_ref[0, rows, :], state_ref, a_neg2, d_skip_ref[...])
        y = y * z_ref[0, rows, :].astype(F32)
        parts = []
        for g in range(SSD_GROUPS):
            yg = y[:, g * SSD_GROUP_WIDTH:(g + 1) * SSD_GROUP_WIDTH]
            parts.append(yg * lax.rsqrt(jnp.mean(yg * yg, axis=-1, keepdims=True) + 1e-6))
        yn = (jnp.concatenate(parts, axis=1) * norm_ref[...]).astype(BF16)
        out = _dot(yn, w_out_ref[...])
        o_ref[0, rows, :] = _layernorm(DN_ALPHA * x_ref[0, rows, :] + out, g_ref[...], b_ref[...])


def _ssd_call(z, xbc, dt, x, a_log, d_skip_x, norm_w, w_out, ln_g, ln_b, n_chunks):
    bsz, seqlen, _ = x.shape
    tm = n_chunks * SSD_CHUNK
    tok = lambda width: pl.BlockSpec((1, tm, width), lambda b, i: (b, i, 0))
    return pl.pallas_call(
        functools.partial(_ssd_kernel, n_chunks=n_chunks),
        grid=(bsz, seqlen // tm),
        in_specs=[tok(SSD_INNER), tok(SSD_CONV_DIM), tok(LANES), tok(D_MODEL),
                  _const_spec((1, LANES)), _const_spec((1, SSD_INNER)),
                  _const_spec((1, SSD_INNER)), _const_spec((SSD_INNER, D_MODEL)),
                  _const_spec((1, D_MODEL)), _const_spec((1, D_MODEL))],
        out_specs=tok(D_MODEL),
        out_shape=jax.ShapeDtypeStruct((bsz, seqlen, D_MODEL), F32),
        scratch_shapes=[pltpu.VMEM((SSD_STATE, SSD_INNER), F32)],
        compiler_params=_params(("parallel", "arbitrary")),
        name="ssd_scan_out",
    )(z, xbc, dt, x, a_log, d_skip_x, norm_w, w_out, ln_g, ln_b)


def _rot_cols(w):
    half = MLA_ROPE // 2
    return jnp.concatenate([-w[..., half:], w[..., :half]], axis=-1)


def _pad_heads(w_heads):
    r, h, c = w_heads.shape
    return jnp.pad(w_heads, ((0, 0), (0, 0), (0, HEAD_PAD - c))).reshape(r, h * HEAD_PAD)


def _block_diag(w):
    h, d, _ = w.shape
    eye = jnp.eye(h, dtype=w.dtype)
    return (eye[:, None, :, None] * w[:, :, None, :]).reshape(h * d, h * d)


def _layer0(x, pos_f, freq, w_in, conv_w, conv_b, gate_a_w, gate_a_b, gate_x_w, gate_x_b, lam,
            q_norm, kv_norm, w_uq, w_ukv, w_out, ln_g, ln_b, tm, ts, tq):
    o_kr = RNN_WIDTH + AB_WIDTH + MLA_Q_RANK + MLA_KV_RANK
    w_kr = w_in[:, o_kr:]
    w_in_ext = jnp.concatenate([w_in[:, :o_kr], jnp.zeros_like(w_in[:, :MLA_NOPE]), w_kr,
                                _rot_cols(w_kr)], axis=1).astype(BF16)

    uq = w_uq.reshape(MLA_Q_RANK, MLA_HEADS, MLA_NOPE + MLA_ROPE)
    wq = _pad_heads(jnp.concatenate([uq, _rot_cols(uq[..., MLA_NOPE:])], axis=-1)).astype(BF16)
    ukv = w_ukv.reshape(MLA_KV_RANK, MLA_HEADS, MLA_NOPE + MLA_V)
    wk = _pad_heads(ukv[..., :MLA_NOPE]).astype(BF16)
    wvt = jnp.pad(jnp.transpose(ukv[..., MLA_NOPE:], (1, 2, 0)),
                  ((0, 0), (0, MLA_VT_ROWS - MLA_V), (0, 0)))
    wvt = wvt.reshape(MLA_HEADS * MLA_VT_ROWS, MLA_KV_RANK).astype(BF16)
    vones = jnp.tile((jnp.arange(MLA_VT_ROWS) >= MLA_V).astype(F32), MLA_HEADS)[:, None]

    xr, gate, q, k, vt = _ab_in_call(x, pos_f, w_in_ext, freq, q_norm[None], kv_norm[None],
                                     wq, wk, wvt, vones, tm)

    w_gate = jnp.concatenate([_block_diag(gate_a_w), _block_diag(gate_x_w)], axis=1).astype(BF16)
    b_gate = jnp.concatenate([gate_a_b, gate_x_b])[None]
    h = _rglru_call(xr, conv_w, conv_b[None], w_gate, b_gate, lam[None], ts)

    y_mla = _mla_attn_call(q, k, vt, tq)
    return _ab_out_call(h, y_mla, gate, x, w_out.astype(BF16), ln_g[None], ln_b[None], tm)


def _layer1(x, w_in, conv_w, conv_b, dt_bias, a_log, d_skip, norm_w, w_out, ln_g, ln_b,
            tm, n_chunks):
    pad_h = lambda a: jnp.pad(a, ((0, 0), (0, LANES - SSD_HEADS)))
    w_all = pad_h(w_in).astype(BF16)
    z, xbc, dt = _ssd_in_call(x, w_all, pad_h(dt_bias[None]), conv_w, conv_b[None], tm)
    d_skip_x = jnp.repeat(d_skip, SSD_HEAD_DIM)[None]
    return _ssd_call(z, xbc, dt, x, pad_h(a_log[None]), d_skip_x, norm_w[None],
                     w_out.astype(BF16), ln_g[None], ln_b[None], n_chunks)


def kernel(x, positions, ab_w_in, ab_conv_w, ab_conv_b, ab_gate_a_w, ab_gate_a_b, ab_gate_x_w,
           ab_gate_x_b, ab_lambda, mla_q_norm, mla_kv_norm, mla_w_uq, mla_w_ukv, ab_w_out,
           ab_ln_g, ab_ln_b, ssd_w_in, ssd_conv_w, ssd_conv_b, ssd_dt_bias, ssd_a_log, ssd_d,
           ssd_norm, ssd_w_out, ssd_ln_g, ssd_ln_b):
    seqlen = x.shape[1]
    tm = min(512, seqlen)
    tm_ssd = min(256, seqlen)
    ts = min(128, seqlen)
    tq = min(512, seqlen)
    n_chunks = min(4, seqlen // SSD_CHUNK)

    inv_freq = ROPE_THETA ** (-jnp.arange(0, MLA_ROPE, 2, dtype=F32) / MLA_ROPE)
    freq = jnp.concatenate([jnp.zeros((MLA_NOPE,), F32), inv_freq, inv_freq,
                            jnp.zeros((HEAD_PAD - MLA_NOPE - MLA_ROPE,), F32)])[None]
    pos_f = positions.astype(F32)[..., None]

    for layer in range(DEPTH):
        j = layer // 2
        if layer % 2 == 0:
            x = _layer0(x, pos_f, freq, ab_w_in[j], ab_conv_w[j], ab_conv_b[j], ab_gate_a_w[j],
                        ab_gate_a_b[j], ab_gate_x_w[j], ab_gate_x_b[j], ab_lambda[j],
                        mla_q_norm[j], mla_kv_norm[j], mla_w_uq[j], mla_w_ukv[j], ab_w_out[j],
                        ab_ln_g[j], ab_ln_b[j], tm, ts, tq)
        else:
            x = _layer1(x, ssd_w_in[j], ssd_conv_w[j], ssd_conv_b[j], ssd_dt_bias[j],
                        ssd_a_log[j], ssd_d[j], ssd_norm[j], ssd_w_out[j], ssd_ln_g[j],
                        ssd_ln_b[j], tm_ssd, n_chunks)
    return x
```

```python
import functools
import math

import jax
import jax.numpy as jnp
from jax import lax
from jax.experimental import pallas as pl
from jax.experimental.pallas import tpu as pltpu

D_MODEL = 1024
DEPTH = 2
DN_ALPHA = (2.0 * DEPTH) ** 0.25

RNN_WIDTH = 512
RNN_HEADS = 8
RNN_HEAD_DIM = RNN_WIDTH // RNN_HEADS
CONV_WIDTH = 4
RG_C = 8.0

MLA_HEADS = 8
MLA_NOPE = 64
MLA_ROPE = 32
MLA_V = 64
MLA_Q_RANK = 256
MLA_KV_RANK = 128
MLA_WIDTH = MLA_HEADS * MLA_V
ROPE_THETA = 10000.0
AB_WIDTH = RNN_WIDTH + MLA_WIDTH

SSD_INNER = 2048
SSD_HEAD_DIM = 64
SSD_HEADS = 32
SSD_GROUPS = 4
SSD_STATE = 128
SSD_CHUNK = 128
SSD_CONV_DIM = SSD_INNER + 2 * SSD_GROUPS * SSD_STATE
SSD_GROUP_WIDTH = SSD_INNER // SSD_GROUPS

LANES = 128
HEAD_PAD = LANES
MLA_VT_ROWS = MLA_V + 16
QK_PAD_WIDTH = MLA_HEADS * HEAD_PAD
VMEM_LIMIT = 56 * 1024 * 1024

BF16 = jnp.bfloat16
F32 = jnp.float32


NEG_LOG2E = -math.log2(math.e)


def _exp_neg(x):
    return jnp.exp2(x * NEG_LOG2E)


def _sigmoid(x):
    return 1.0 / (1.0 + _exp_neg(x))


def _silu(x):
    return x * _sigmoid(x)


def _log1p(y):
    u = 1.0 + y
    return jnp.where(u == 1.0, y, jnp.log(u) * (y / (u - 1.0)))


def _softplus(x):
    return jnp.maximum(x, 0.0) + _log1p(_exp_neg(jnp.abs(x)))


def _dot(a, b):
    return jnp.dot(a, b, preferred_element_type=F32)


def _const_spec(shape):
    zeros = (0,) * len(shape)
    return pl.BlockSpec(shape, lambda *_: zeros)


def _params(semantics, flags=None):
    return pltpu.CompilerParams(dimension_semantics=semantics,
                                vmem_limit_bytes=VMEM_LIMIT, flags=flags)


def _ab_in_kernel(x_ref, pos_ref, w_in_ref, freq_ref, qn_ref, kvn_ref, wq_ref,
                  wk_ref, wvt_ref, vones_ref,
                  xr_ref, gate_ref, q_ref, k_ref, vt_ref):
    xb = x_ref[0].astype(BF16)
    proj = _dot(xb, w_in_ref[...])
    o_gate = RNN_WIDTH
    o_cq = o_gate + AB_WIDTH
    o_ckv = o_cq + MLA_Q_RANK
    o_kr = o_ckv + MLA_KV_RANK

    xr_ref[0] = proj[:, :o_gate]
    gate_ref[0] = _silu(proj[:, o_gate:o_cq]).astype(BF16)

    ang = pos_ref[0] * freq_ref[...]
    cos = jnp.cos(ang)
    sin = jnp.sin(ang)
    to_rope_lanes = HEAD_PAD - MLA_ROPE

    c_q = proj[:, o_cq:o_ckv]
    c_q = c_q * lax.rsqrt(jnp.mean(c_q * c_q, axis=-1, keepdims=True) + 1e-6) * qn_ref[...]
    q_all = _dot(c_q.astype(BF16), wq_ref[...])
    scale = (MLA_NOPE + MLA_ROPE) ** -0.5 * math.log2(math.e)
    for h in range(MLA_HEADS):
        q_h = q_all[:, h * HEAD_PAD:(h + 1) * HEAD_PAD]
        q_h = q_h * cos + pltpu.roll(q_h, to_rope_lanes, axis=1) * sin
        q_ref[0, :, h * HEAD_PAD:(h + 1) * HEAD_PAD] = (q_h * scale).astype(BF16)

    c_kv = proj[:, o_ckv:o_kr]
    c_kv = c_kv * lax.rsqrt(jnp.mean(c_kv * c_kv, axis=-1, keepdims=True) + 1e-6) * kvn_ref[...]
    c_kvb = c_kv.astype(BF16)
    kr = proj[:, o_kr:o_kr + HEAD_PAD]
    lane = lax.broadcasted_iota(jnp.int32, kr.shape, 1)
    k_rope = jnp.where(lane < MLA_NOPE + MLA_ROPE,
                       kr * cos + pltpu.roll(kr, to_rope_lanes, axis=1) * sin, 0.0)
    k = _dot(c_kvb, wk_ref[...]) + jnp.concatenate([k_rope] * MLA_HEADS, axis=1)
    k_ref[0] = k.astype(BF16)
    vt = lax.dot_general(wvt_ref[...], c_kvb, (((1,), (1,)), ((), ())),
                         preferred_element_type=F32) + vones_ref[...]
    vt_ref[0] = vt.reshape(MLA_HEADS, MLA_VT_ROWS, vt.shape[1]).astype(BF16)


def _ab_in_call(x, pos_f, w_in_ext, freq, q_norm, kv_norm, wq, wk, wvt, vones, tm):
    bsz, seqlen, _ = x.shape
    n_in = w_in_ext.shape[1]
    grid = (bsz, seqlen // tm)
    tok = lambda width: pl.BlockSpec((1, tm, width), lambda b, i: (b, i, 0))
    vt_rows = MLA_HEADS * MLA_VT_ROWS
    return pl.pallas_call(
        _ab_in_kernel,
        grid=grid,
        in_specs=[tok(D_MODEL), tok(1), _const_spec((D_MODEL, n_in)), _const_spec((1, HEAD_PAD)),
                  _const_spec((1, MLA_Q_RANK)), _const_spec((1, MLA_KV_RANK)),
                  _const_spec((MLA_Q_RANK, QK_PAD_WIDTH)),
                  _const_spec((MLA_KV_RANK, QK_PAD_WIDTH)), _const_spec((vt_rows, MLA_KV_RANK)),
                  _const_spec((vt_rows, 1))],
        out_specs=[tok(RNN_WIDTH), tok(AB_WIDTH), tok(QK_PAD_WIDTH), tok(QK_PAD_WIDTH),
                   pl.BlockSpec((1, MLA_HEADS, MLA_VT_ROWS, tm), lambda b, i: (b, 0, 0, i))],
        out_shape=[jax.ShapeDtypeStruct((bsz, seqlen, RNN_WIDTH), F32),
                   jax.ShapeDtypeStruct((bsz, seqlen, AB_WIDTH), BF16),
                   jax.ShapeDtypeStruct((bsz, seqlen, QK_PAD_WIDTH), BF16),
                   jax.ShapeDtypeStruct((bsz, seqlen, QK_PAD_WIDTH), BF16),
                   jax.ShapeDtypeStruct((bsz, MLA_HEADS, MLA_VT_ROWS, seqlen), BF16)],
        compiler_params=_params(("parallel", "parallel")),
        name="ab_in_proj",
    )(x, pos_f, w_in_ext, freq, q_norm, kv_norm, wq, wk, wvt, vones)


def _rglru_kernel(xr_ref, conv_w_ref, conv_b_ref, w_gate_ref, b_gate_ref, lam_ref,
                  h_ref, tail_ref, carry_ref, a_ref, u_ref):
    bsz, ts, width = xr_ref.shape

    @pl.when(pl.program_id(0) == 0)
    def _():
        tail_ref[...] = jnp.zeros_like(tail_ref)
        carry_ref[...] = jnp.zeros_like(carry_ref)

    x_blk = jnp.swapaxes(xr_ref[...], 0, 1)
    x_ext = jnp.concatenate([tail_ref[...], x_blk], axis=0)
    tail_ref[...] = x_blk[ts - (CONV_WIDTH - 1):]
    xc = conv_b_ref[...][None]
    for k in range(CONV_WIDTH):
        xc = xc + x_ext[k:k + ts] * conv_w_ref[k:k + 1, :][None]

    xc2 = xc.reshape(ts * bsz, width)
    gates = _dot(xc2.astype(BF16), w_gate_ref[...]) + b_gate_ref[...]
    r = _sigmoid(gates[:, :width])
    i = _sigmoid(gates[:, width:])
    neg_c_softplus = -RG_C * _softplus(-lam_ref[...])
    log_a = r * neg_c_softplus
    a = jnp.exp2(r * (neg_c_softplus * -NEG_LOG2E))
    mult = jnp.sqrt(-jnp.tanh(log_a) * (a * a + 1.0))
    u = mult * (i * xc2)
    a_ref[...] = a.reshape(ts, bsz, width)
    u_ref[...] = u.reshape(ts, bsz, width)

    def step(t, h):
        h = a_ref[t] * h + u_ref[t]
        u_ref[t] = h
        return h

    carry_ref[...] = lax.fori_loop(0, ts, step, carry_ref[...], unroll=8)
    h_ref[...] = jnp.swapaxes(u_ref[...], 0, 1)


def _rglru_call(xr, conv_w, conv_b, w_gate, b_gate, lam, ts):
    bsz, seqlen, width = xr.shape
    blk = pl.BlockSpec((bsz, ts, width), lambda i: (0, i, 0))
    return pl.pallas_call(
        _rglru_kernel,
        grid=(seqlen // ts,),
        in_specs=[blk, _const_spec((CONV_WIDTH, width)), _const_spec((1, width)),
                  _const_spec((width, 2 * width)), _const_spec((1, 2 * width)),
                  _const_spec((1, width))],
        out_specs=blk,
        out_shape=jax.ShapeDtypeStruct((bsz, seqlen, width), F32),
        scratch_shapes=[pltpu.VMEM((CONV_WIDTH - 1, bsz, width), F32),
                        pltpu.VMEM((bsz, width), F32),
                        pltpu.VMEM((ts, bsz, width), F32),
                        pltpu.VMEM((ts, bsz, width), F32)],
        compiler_params=_params(("arbitrary",)),
        name="rglru_scan",
    )(xr, conv_w, conv_b, w_gate, b_gate, lam)


ATTN_ROWS = 256


def _mla_attn_tiles(q_ref, k_ref, vt_ref, o_ref, s_refs, tiles, tq):
    heads = q_ref.shape[2] // HEAD_PAD
    halves = tq // ATTN_ROWS
    units = [(h, r) for h in range(heads) for r in range(halves)]
    steps = [(qi, j) for qi in tiles for j in range(qi + 1)]

    def depth(qi, j, r):
        return (r + 1) * ATTN_ROWS if j == qi else tq

    def scores_into(dst_ref, qi, j):
        for u, (h, r) in enumerate(units):
            w = depth(qi, j, r)
            q_u = q_ref[0, qi * tq + r * ATTN_ROWS:qi * tq + (r + 1) * ATTN_ROWS,
                        h * HEAD_PAD:(h + 1) * HEAD_PAD]
            k_blk = k_ref[0, j * tq:j * tq + w, h * HEAD_PAD:(h + 1) * HEAD_PAD]
            dst_ref[u, :w, :] = lax.dot_general(k_blk, q_u, (((1,), (1,)), ((), ())),
                                                preferred_element_type=F32)

    scores_into(s_refs[0], *steps[0])
    ms, accs = None, None
    for i, (qi, j) in enumerate(steps):
        src_ref = s_refs[i % 2]
        if i + 1 < len(steps):
            scores_into(s_refs[(i + 1) % 2], *steps[i + 1])
        if j == 0:
            ms = [jnp.full((1, ATTN_ROWS), -1e30, F32) for _ in units]
            accs = [jnp.zeros((MLA_VT_ROWS, ATTN_ROWS), F32) for _ in units]
        for u, (h, r) in enumerate(units):
            w = depth(qi, j, r)
            if j == qi:
                key = lax.broadcasted_iota(jnp.int32, (ATTN_ROWS, ATTN_ROWS), 0)
                qry = lax.broadcasted_iota(jnp.int32, (ATTN_ROWS, ATTN_ROWS), 1)
                edge = slice(w - ATTN_ROWS, w)
                src_ref[u, edge, :] = jnp.where(key <= qry, src_ref[u, edge, :], -1e30)
            vt_blk = vt_ref[0, h, :, j * tq:j * tq + w]
            m_new = jnp.maximum(ms[u], jnp.max(src_ref[u, :w, :], axis=0, keepdims=True))
            p = jnp.exp2((src_ref[u, :w, :] - m_new).astype(BF16))
            accs[u] = jnp.exp2(ms[u] - m_new) * accs[u] + _dot(vt_blk, p)
            ms[u] = m_new
        if j == qi:
            outs = [acc[:MLA_V] / acc[MLA_V:MLA_V + 1] for acc in accs]
            for pair in range(heads // 2):
                for r in range(halves):
                    o_t = jnp.concatenate([outs[2 * pair * halves + r],
                                           outs[(2 * pair + 1) * halves + r]], axis=0)
                    o_ref[0, qi * tq + r * ATTN_ROWS:qi * tq + (r + 1) * ATTN_ROWS,
                          pair * HEAD_PAD:(pair + 1) * HEAD_PAD] = o_t.T.astype(o_ref.dtype)


def _mla_attn_kernel(q_ref, k_ref, vt_ref, o_ref, s0_ref, s1_ref, *, tq, groups):
    g = pl.program_id(2)
    for idx, tiles in enumerate(groups):
        @pl.when(g == idx)
        def _(tiles=tiles):
            _mla_attn_tiles(q_ref, k_ref, vt_ref, o_ref, (s0_ref, s1_ref), tiles, tq)


def _mla_attn_call(q, k, vt, tq, heads_per_step=2):
    bsz, seqlen, _ = q.shape
    width = heads_per_step * HEAD_PAD
    n_hsteps = MLA_HEADS // heads_per_step
    out_w = heads_per_step * MLA_V
    n_units = heads_per_step * (tq // ATTN_ROWS)
    n_q = seqlen // tq
    groups = tuple(tuple(sorted({i, n_q - 1 - i})) for i in range((n_q + 1) // 2))
    resident = lambda w: pl.BlockSpec((1, seqlen, w), lambda b, h, g: (b, 0, h))
    return pl.pallas_call(
        functools.partial(_mla_attn_kernel, tq=tq, groups=groups),
        grid=(bsz, n_hsteps, len(groups)),
        in_specs=[resident(width), resident(width),
                  pl.BlockSpec((1, heads_per_step, MLA_VT_ROWS, seqlen),
                               lambda b, h, g: (b, h, 0, 0))],
        out_specs=resident(out_w),
        out_shape=jax.ShapeDtypeStruct((bsz, seqlen, MLA_WIDTH), BF16),
        scratch_shapes=[pltpu.VMEM((n_units, tq, ATTN_ROWS), F32),
                        pltpu.VMEM((n_units, tq, ATTN_ROWS), F32)],
        compiler_params=_params(("parallel", "parallel", "arbitrary")),
        name="mla_attention",
    )(q, k, vt)


def _layernorm(z, g, b):
    mu = jnp.mean(z, axis=-1, keepdims=True)
    zc = z - mu
    var = jnp.mean(zc * zc, axis=-1, keepdims=True)
    return zc * lax.rsqrt(var + 1e-5) * g + b


def _ab_out_kernel(h_ref, ya_ref, gate_ref, x_ref, w_ref, g_ref, b_ref, o_ref):
    gate = gate_ref[0].astype(F32)
    y_rnn = (h_ref[0] * gate[:, :RNN_WIDTH]).astype(BF16)
    y_mla = (ya_ref[0].astype(F32) * gate[:, RNN_WIDTH:]).astype(BF16)
    y = _dot(y_rnn, w_ref[:RNN_WIDTH, :]) + _dot(y_mla, w_ref[RNN_WIDTH:, :])
    o_ref[0] = _layernorm(DN_ALPHA * x_ref[0] + y, g_ref[...], b_ref[...])


def _ab_out_call(h, y_mla, gate, x, w_out, ln_g, ln_b, tm):
    bsz, seqlen, _ = x.shape
    tok = lambda width: pl.BlockSpec((1, tm, width), lambda b, i: (b, i, 0))
    return pl.pallas_call(
        _ab_out_kernel,
        grid=(bsz, seqlen // tm),
        in_specs=[tok(RNN_WIDTH), tok(MLA_WIDTH), tok(AB_WIDTH), tok(D_MODEL),
                  _const_spec((AB_WIDTH, D_MODEL)), _const_spec((1, D_MODEL)),
                  _const_spec((1, D_MODEL))],
        out_specs=tok(D_MODEL),
        out_shape=jax.ShapeDtypeStruct((bsz, seqlen, D_MODEL), F32),
        compiler_params=_params(("parallel", "parallel")),
        name="ab_out_proj",
    )(h, y_mla, gate, x, w_out, ln_g, ln_b)


SSD_IN_COLS = 512
SUBLANES = 8


BF16_ROWS = 16


def _conv_silu_store(proj, tail, w, b, out_ref, cols):
    n_rows = proj.shape[0]
    taps = [w[k:k + 1, :] for k in range(CONV_WIDTH)]
    first = lax.broadcasted_iota(jnp.int32, tail.shape, 0) == 0
    prev = None
    done = []
    for i in range(-1, n_rows // SUBLANES):
        x = tail if i < 0 else proj[i * SUBLANES:(i + 1) * SUBLANES]
        acc = x * taps[0]
        rolled = []
        for k in range(1, CONV_WIDTH):
            r = pltpu.roll(acc, 1, axis=0)
            rolled.append(r)
            acc = (r if prev is None else jnp.where(first, prev[k - 1], r)) + x * taps[k]
        prev = rolled
        if i >= 0:
            done.append(acc + b)
        if len(done) == BF16_ROWS // SUBLANES:
            top = (i + 1) * SUBLANES
            out_ref[0, top - BF16_ROWS:top, cols] = _silu(
                jnp.concatenate(done, axis=0)).astype(BF16)
            done = []


SSD_PROJ_WIDTH = SSD_INNER + SSD_CONV_DIM + LANES


def _ssd_in_stage(step, x_ref, w_ref, dt_bias_ref, conv_w_ref, conv_b_ref,
                  z_ref, xbc_ref, dt_ref, new_ref, old_ref, project=True, finish=True):
    tm = x_ref.shape[1]
    x_off = SSD_INNER
    if finish:
        tails = []
        for c in range(SSD_CONV_DIM // SSD_IN_COLS):
            cols = slice(x_off + c * SSD_IN_COLS, x_off + (c + 1) * SSD_IN_COLS)
            tails.append(jnp.where(step == 1, 0.0, new_ref[tm - SUBLANES:, cols]))

    if project:
        xb = x_ref[0].astype(BF16)
        for c in range(0, SSD_PROJ_WIDTH, SSD_IN_COLS):
            cols = slice(c, min(c + SSD_IN_COLS, SSD_PROJ_WIDTH))
            new_ref[:, cols] = _dot(xb, w_ref[:, cols])
    if not finish:
        return

    for c in range(SSD_INNER // SSD_IN_COLS):
        cols = slice(c * SSD_IN_COLS, (c + 1) * SSD_IN_COLS)
        z_ref[0, :, cols] = _silu(old_ref[:, cols]).astype(BF16)
    for c in range(SSD_CONV_DIM // SSD_IN_COLS):
        cols = slice(c * SSD_IN_COLS, (c + 1) * SSD_IN_COLS)
        p_cols = slice(x_off + c * SSD_IN_COLS, x_off + (c + 1) * SSD_IN_COLS)
        _conv_silu_store(old_ref[:, p_cols], tails[c], conv_w_ref[:, cols], conv_b_ref[:, cols],
                         xbc_ref, cols)
    dt_ref[0] = _softplus(old_ref[:, SSD_INNER + SSD_CONV_DIM:] + dt_bias_ref[...])


def _ssd_in_kernel(x_ref, w_ref, dt_bias_ref, conv_w_ref, conv_b_ref,
                   z_ref, xbc_ref, dt_ref, p0_ref, p1_ref):
    step = pl.program_id(1)
    last = pl.num_programs(1) - 1
    args = (step, x_ref, w_ref, dt_bias_ref, conv_w_ref, conv_b_ref, z_ref, xbc_ref, dt_ref)
    bufs = (p0_ref, p1_ref)

    @pl.when(step == 0)
    def _():
        p1_ref[p1_ref.shape[0] - SUBLANES:, :] = jnp.zeros((SUBLANES, p1_ref.shape[1]), F32)
        _ssd_in_stage(*args, p0_ref, p1_ref, finish=False)

    for parity in range(2):
        @pl.when((step > 0) & (step < last) & (step % 2 == parity))
        def _(parity=parity):
            _ssd_in_stage(*args, bufs[parity], bufs[1 - parity])

        @pl.when((step == last) & (step % 2 == parity))
        def _(parity=parity):
            _ssd_in_stage(*args, bufs[parity], bufs[1 - parity], project=False)


def _ssd_in_call(x, w_all, dt_bias, conv_w, conv_b, tm):
    bsz, seqlen, _ = x.shape
    n_tiles = seqlen // tm
    tile_in = pl.BlockSpec((1, tm, D_MODEL), lambda b, i: (b, jnp.minimum(i, n_tiles - 1), 0))
    tile_out = lambda width: pl.BlockSpec((1, tm, width),
                                          lambda b, i: (b, jnp.maximum(i - 1, 0), 0))
    return pl.pallas_call(
        _ssd_in_kernel,
        grid=(bsz, n_tiles + 1),
        in_specs=[tile_in, _const_spec((D_MODEL, SSD_PROJ_WIDTH)),
                  _const_spec((1, LANES)), _const_spec((CONV_WIDTH, SSD_CONV_DIM)),
                  _const_spec((1, SSD_CONV_DIM))],
        out_specs=[tile_out(SSD_INNER), tile_out(SSD_CONV_DIM), tile_out(LANES)],
        out_shape=[jax.ShapeDtypeStruct((bsz, seqlen, SSD_INNER), BF16),
                   jax.ShapeDtypeStruct((bsz, seqlen, SSD_CONV_DIM), BF16),
                   jax.ShapeDtypeStruct((bsz, seqlen, LANES), F32)],
        scratch_shapes=[pltpu.VMEM((tm, SSD_PROJ_WIDTH), F32),
                        pltpu.VMEM((tm, SSD_PROJ_WIDTH), F32)],
        compiler_params=_params(("parallel", "arbitrary")),
        name="ssd_in_proj",
    )(x, w_all, dt_bias, conv_w, conv_b)


def _cumsum_rows(x):
    n = x.shape[0]
    row = lax.broadcasted_iota(jnp.int32, x.shape, 0)
    shift = 1
    while shift < n:
        x = x + jnp.where(row >= shift, pltpu.roll(x, shift, axis=0), 0.0)
        shift *= 2
    return x


LOG2E = math.log2(math.e)


def _ssd_chunk(xbc_ref, rows, dt, state_ref, a_neg2, d_skip_x):
    L = SSD_CHUNK
    gw = SSD_GROUPS * SSD_STATE

    cs = _cumsum_rows(dt * a_neg2)
    cs_t = cs.T
    dt_t = dt.T
    w_t = dt_t * jnp.exp2(cs_t[:, L - 1:L] - cs_t)

    row = lax.broadcasted_iota(jnp.int32, (L, L), 0)
    col = lax.broadcasted_iota(jnp.int32, (L, L), 1)
    tril = col <= row
    lane = lax.broadcasted_iota(jnp.int32, (L, LANES), 1)
    low = lane < SSD_HEAD_DIM
    keep_lo = low.astype(F32).astype(BF16)
    keep_hi = 1.0 - keep_lo

    hpg = SSD_HEADS // SSD_GROUPS
    y_parts = []
    for g in range(SSD_GROUPS):
        b_g = xbc_ref[0, rows, SSD_INNER + g * SSD_STATE:SSD_INNER + (g + 1) * SSD_STATE]
        c_g = xbc_ref[0, rows, SSD_INNER + gw + g * SSD_STATE:SSD_INNER + gw + (g + 1) * SSD_STATE]
        cb = lax.dot_general(c_g, b_g, (((1,), (1,)), ((), ())),
                             preferred_element_type=F32)
        cb = jnp.where(tril, cb, 0.0)
        b_t = b_g.astype(F32).T
        glanes = slice(g * SSD_GROUP_WIDTH, (g + 1) * SSD_GROUP_WIDTH)
        y_off = _dot(c_g, state_ref[:, glanes].astype(BF16))
        for pair in range(hpg // 2):
            h0 = g * hpg + 2 * pair
            plane = slice(h0 * SSD_HEAD_DIM, (h0 + 2) * SSD_HEAD_DIM)
            tops, bots, decays = [], [], []
            for h in (h0, h0 + 1):
                cs_col = jnp.broadcast_to(cs[:, h:h + 1], (L, LANES))
                seg = cs_col - cs_t[h:h + 1, :]
                m_h = jnp.exp2(jnp.minimum(seg, 0.0)) * (cb * dt_t[h:h + 1, :])
                tops.append(m_h.astype(BF16))
                bots.append((b_t * w_t[h:h + 1, :]).astype(BF16))
                decays.append(jnp.exp2(cs_col))
            lhs = jnp.concatenate([jnp.concatenate(tops, axis=1),
                                   jnp.concatenate(bots, axis=1)], axis=0)
            x_pair = xbc_ref[0, rows, plane]
            rhs = jnp.concatenate([x_pair * keep_lo, x_pair * keep_hi], axis=0)
            res = _dot(lhs, rhs)
            dec = jnp.where(low, decays[0], decays[1])
            off = pair * 2 * SSD_HEAD_DIM
            y_pair = (res[:L] + dec * y_off[:, off:off + 2 * SSD_HEAD_DIM]
                      + d_skip_x[:, plane] * x_pair.astype(F32))
            y_parts.append(y_pair)
            state_ref[:, plane] = state_ref[:, plane] * dec[L - 1:L, :] + res[L:]
    return jnp.concatenate(y_parts, axis=1)


def _ssd_kernel(z_ref, xbc_ref, dt_ref, x_ref, a_log_ref, d_skip_ref, norm_ref, w_out_ref,
                g_ref, b_ref, o_ref, state_ref, *, n_chunks):
    L = SSD_CHUNK

    @pl.when(pl.program_id(1) == 0)
    def _():
        state_ref[...] = jnp.zeros_like(state_ref)

    a_neg2 = -jnp.exp(a_log_ref[...]) * LOG2E
    for c in range(n_chunks):
        rows = slice(c * L, (c + 1) * L)
        y = _ssd_chunk(xbc_ref, rows, dt_ref[0, rows, :], state_ref, a_neg2, d_skip_ref[...])
        y = y * z_ref[0, rows, :].astype(F32)
        parts = []
        for g in range(SSD_GROUPS):
            yg = y[:, g * SSD_GROUP_WIDTH:(g + 1) * SSD_GROUP_WIDTH]
            parts.append(yg * lax.rsqrt(jnp.mean(yg * yg, axis=-1, keepdims=True) + 1e-6))
        yn = (jnp.concatenate(parts, axis=1) * norm_ref[...]).astype(BF16)
        out = _dot(yn, w_out_ref[...])
        o_ref[0, rows, :] = _layernorm(DN_ALPHA * x_ref[0, rows, :] + out, g_ref[...], b_ref[...])


def _ssd_call(z, xbc, dt, x, a_log, d_skip_x, norm_w, w_out, ln_g, ln_b, n_chunks):
    bsz, seqlen, _ = x.shape
    tm = n_chunks * SSD_CHUNK
    tok = lambda width: pl.BlockSpec((1, tm, width), lambda b, i: (b, i, 0))
    return pl.pallas_call(
        functools.partial(_ssd_kernel, n_chunks=n_chunks),
        grid=(bsz, seqlen // tm),
        in_specs=[tok(SSD_INNER), tok(SSD_CONV_DIM), tok(LANES), tok(D_MODEL),
                  _const_spec((1, LANES)), _const_spec((1, SSD_INNER)),
                  _const_spec((1, SSD_INNER)), _const_spec((SSD_INNER, D_MODEL)),
                  _const_spec((1, D_MODEL)), _const_spec((1, D_MODEL))],
        out_specs=tok(D_MODEL),
        out_shape=jax.ShapeDtypeStruct((bsz, seqlen, D_MODEL), F32),
        scratch_shapes=[pltpu.VMEM((SSD_STATE, SSD_INNER), F32)],
        compiler_params=_params(("parallel", "arbitrary")),
        name="ssd_scan_out",
    )(z, xbc, dt, x, a_log, d_skip_x, norm_w, w_out, ln_g, ln_b)


def _rot_cols(w):
    half = MLA_ROPE // 2
    return jnp.concatenate([-w[..., half:], w[..., :half]], axis=-1)


def _pad_heads(w_heads):
    r, h, c = w_heads.shape
    return jnp.pad(w_heads, ((0, 0), (0, 0), (0, HEAD_PAD - c))).reshape(r, h * HEAD_PAD)


def _block_diag(w):
    h, d, _ = w.shape
    eye = jnp.eye(h, dtype=w.dtype)
    return (eye[:, None, :, None] * w[:, :, None, :]).reshape(h * d, h * d)


def _layer0(x, pos_f, freq, w_in, conv_w, conv_b, gate_a_w, gate_a_b, gate_x_w, gate_x_b, lam,
            q_norm, kv_norm, w_uq, w_ukv, w_out, ln_g, ln_b, tm, ts, tq):
    o_kr = RNN_WIDTH + AB_WIDTH + MLA_Q_RANK + MLA_KV_RANK
    w_kr = w_in[:, o_kr:]
    w_in_ext = jnp.concatenate([w_in[:, :o_kr], jnp.zeros_like(w_in[:, :MLA_NOPE]), w_kr,
                                _rot_cols(w_kr)], axis=1).astype(BF16)

    uq = w_uq.reshape(MLA_Q_RANK, MLA_HEADS, MLA_NOPE + MLA_ROPE)
    wq = _pad_heads(jnp.concatenate([uq, _rot_cols(uq[..., MLA_NOPE:])], axis=-1)).astype(BF16)
    ukv = w_ukv.reshape(MLA_KV_RANK, MLA_HEADS, MLA_NOPE + MLA_V)
    wk = _pad_heads(ukv[..., :MLA_NOPE]).astype(BF16)
    wvt = jnp.pad(jnp.transpose(ukv[..., MLA_NOPE:], (1, 2, 0)),
                  ((0, 0), (0, MLA_VT_ROWS - MLA_V), (0, 0)))
    wvt = wvt.reshape(MLA_HEADS * MLA_VT_ROWS, MLA_KV_RANK).astype(BF16)
    vones = jnp.tile((jnp.arange(MLA_VT_ROWS) >= MLA_V).astype(F32), MLA_HEADS)[:, None]

    xr, gate, q, k, vt = _ab_in_call(x, pos_f, w_in_ext, freq, q_norm[None], kv_norm[None],
                                     wq, wk, wvt, vones, tm)

    w_gate = jnp.concatenate([_block_diag(gate_a_w), _block_diag(gate_x_w)], axis=1).astype(BF16)
    b_gate = jnp.concatenate([gate_a_b, gate_x_b])[None]
    h = _rglru_call(xr, conv_w, conv_b[None], w_gate, b_gate, lam[None], ts)

    y_mla = _mla_attn_call(q, k, vt, tq)
    return _ab_out_call(h, y_mla, gate, x, w_out.astype(BF16), ln_g[None], ln_b[None],
                        min(2 * tm, x.shape[1]))


def _layer1(x, w_in, conv_w, conv_b, dt_bias, a_log, d_skip, norm_w, w_out, ln_g, ln_b,
            tm, n_chunks):
    pad_h = lambda a: jnp.pad(a, ((0, 0), (0, LANES - SSD_HEADS)))
    w_all = pad_h(w_in).astype(BF16)
    z, xbc, dt = _ssd_in_call(x, w_all, pad_h(dt_bias[None]), conv_w, conv_b[None], tm)
    d_skip_x = jnp.repeat(d_skip, SSD_HEAD_DIM)[None]
    return _ssd_call(z, xbc, dt, x, pad_h(a_log[None]), d_skip_x, norm_w[None],
                     w_out.astype(BF16), ln_g[None], ln_b[None], n_chunks)


def kernel(x, positions, ab_w_in, ab_conv_w, ab_conv_b, ab_gate_a_w, ab_gate_a_b, ab_gate_x_w,
           ab_gate_x_b, ab_lambda, mla_q_norm, mla_kv_norm, mla_w_uq, mla_w_ukv, ab_w_out,
           ab_ln_g, ab_ln_b, ssd_w_in, ssd_conv_w, ssd_conv_b, ssd_dt_bias, ssd_a_log, ssd_d,
           ssd_norm, ssd_w_out, ssd_ln_g, ssd_ln_b):
    seqlen = x.shape[1]
    tm = min(512, seqlen)
    tm_ssd = min(256, seqlen)
    ts = min(128, seqlen)
    tq = min(512, seqlen)
    n_chunks = min(4, seqlen // SSD_CHUNK)

    inv_freq = ROPE_THETA ** (-jnp.arange(0, MLA_ROPE, 2, dtype=F32) / MLA_ROPE)
    freq = jnp.concatenate([jnp.zeros((MLA_NOPE,), F32), inv_freq, inv_freq,
                            jnp.zeros((HEAD_PAD - MLA_NOPE - MLA_ROPE,), F32)])[None]
    pos_f = positions.astype(F32)[..., None]

    for layer in range(DEPTH):
        j = layer // 2
        if layer % 2 == 0:
            x = _layer0(x, pos_f, freq, ab_w_in[j], ab_conv_w[j], ab_conv_b[j], ab_gate_a_w[j],
                        ab_gate_a_b[j], ab_gate_x_w[j], ab_gate_x_b[j], ab_lambda[j],
                        mla_q_norm[j], mla_kv_norm[j], mla_w_uq[j], mla_w_ukv[j], ab_w_out[j],
                        ab_ln_g[j], ab_ln_b[j], tm, ts, tq)
        else:
            x = _layer1(x, ssd_w_in[j], ssd_conv_w[j], ssd_conv_b[j], ssd_dt_bias[j],
                        ssd_a_log[j], ssd_d[j], ssd_norm[j], ssd_w_out[j], ssd_ln_g[j],
                        ssd_ln_b[j], tm_ssd, n_chunks)
    return x
```

```python
import functools
import math

import jax
import jax.numpy as jnp
from jax import lax
from jax.experimental import pallas as pl
from jax.experimental.pallas import tpu as pltpu

D_MODEL = 1024
DEPTH = 2
DN_ALPHA = (2.0 * DEPTH) ** 0.25

RNN_WIDTH = 512
RNN_HEADS = 8
RNN_HEAD_DIM = RNN_WIDTH // RNN_HEADS
CONV_WIDTH = 4
RG_C = 8.0

MLA_HEADS = 8
MLA_NOPE = 64
MLA_ROPE = 32
MLA_V = 64
MLA_Q_RANK = 256
MLA_KV_RANK = 128
MLA_WIDTH = MLA_HEADS * MLA_V
ROPE_THETA = 10000.0
AB_WIDTH = RNN_WIDTH + MLA_WIDTH

SSD_INNER = 2048
SSD_HEAD_DIM = 64
SSD_HEADS = 32
SSD_GROUPS = 4
SSD_STATE = 128
SSD_CHUNK = 128
SSD_CONV_DIM = SSD_INNER + 2 * SSD_GROUPS * SSD_STATE
SSD_GROUP_WIDTH = SSD_INNER // SSD_GROUPS

LANES = 128
SUBLANES = 8
BF16_ROWS = 2 * SUBLANES
HEAD_PAD = LANES
MLA_VT_ROWS = MLA_V + BF16_ROWS
QK_PAD_WIDTH = MLA_HEADS * HEAD_PAD
VMEM_LIMIT = 56 * 1024 * 1024

BF16 = jnp.bfloat16
F32 = jnp.float32


NEG_LOG2E = -math.log2(math.e)


def _exp_neg(x):
    return jnp.exp2(x * NEG_LOG2E)


def _sigmoid(x):
    return 1.0 / (1.0 + _exp_neg(x))


def _silu(x):
    return x * _sigmoid(x)


def _log1p(y):
    u = 1.0 + y
    return jnp.where(u == 1.0, y, jnp.log(u) * (y / (u - 1.0)))


def _softplus(x):
    return jnp.maximum(x, 0.0) + _log1p(_exp_neg(jnp.abs(x)))


def _dot(a, b):
    return jnp.dot(a, b, preferred_element_type=F32)


def _const_spec(shape):
    zeros = (0,) * len(shape)
    return pl.BlockSpec(shape, lambda *_: zeros)


def _params(semantics, flags=None):
    return pltpu.CompilerParams(dimension_semantics=semantics,
                                vmem_limit_bytes=VMEM_LIMIT, flags=flags)


def _ab_in_kernel(x_ref, pos_ref, w_in_ref, freq_ref, qn_ref, kvn_ref, wq_ref,
                  wk_ref, wvt_ref, vones_ref,
                  xr_ref, gate_ref, q_ref, k_ref, vt_ref):
    xb = x_ref[0].astype(BF16)
    proj = _dot(xb, w_in_ref[...])
    o_gate = RNN_WIDTH
    o_cq = o_gate + AB_WIDTH
    o_ckv = o_cq + MLA_Q_RANK
    o_kr = o_ckv + MLA_KV_RANK

    xr_ref[0] = proj[:, :o_gate]
    gate_ref[0] = _silu(proj[:, o_gate:o_cq]).astype(BF16)

    ang = pos_ref[0] * freq_ref[...]
    cos = jnp.cos(ang)
    sin = jnp.sin(ang)
    to_rope_lanes = HEAD_PAD - MLA_ROPE

    c_q = proj[:, o_cq:o_ckv]
    c_q = c_q * lax.rsqrt(jnp.mean(c_q * c_q, axis=-1, keepdims=True) + 1e-6) * qn_ref[...]
    q_all = _dot(c_q.astype(BF16), wq_ref[...])
    scale = (MLA_NOPE + MLA_ROPE) ** -0.5 * math.log2(math.e)
    for h in range(MLA_HEADS):
        q_h = q_all[:, h * HEAD_PAD:(h + 1) * HEAD_PAD]
        q_h = q_h * cos + pltpu.roll(q_h, to_rope_lanes, axis=1) * sin
        q_ref[0, :, h * HEAD_PAD:(h + 1) * HEAD_PAD] = (q_h * scale).astype(BF16)

    c_kv = proj[:, o_ckv:o_kr]
    c_kv = c_kv * lax.rsqrt(jnp.mean(c_kv * c_kv, axis=-1, keepdims=True) + 1e-6) * kvn_ref[...]
    c_kvb = c_kv.astype(BF16)
    kr = proj[:, o_kr:o_kr + HEAD_PAD]
    lane = lax.broadcasted_iota(jnp.int32, kr.shape, 1)
    k_rope = jnp.where(lane < MLA_NOPE + MLA_ROPE,
                       kr * cos + pltpu.roll(kr, to_rope_lanes, axis=1) * sin, 0.0)
    k = _dot(c_kvb, wk_ref[...]) + jnp.concatenate([k_rope] * MLA_HEADS, axis=1)
    k_ref[0] = k.astype(BF16)
    vt = lax.dot_general(wvt_ref[...], c_kvb, (((1,), (1,)), ((), ())),
                         preferred_element_type=F32) + vones_ref[...]
    vt_ref[0] = vt.reshape(MLA_HEADS, MLA_VT_ROWS, vt.shape[1]).astype(BF16)


def _ab_in_call(x, pos_f, w_in_ext, freq, q_norm, kv_norm, wq, wk, wvt, vones, tm):
    bsz, seqlen, _ = x.shape
    n_in = w_in_ext.shape[1]
    grid = (bsz, seqlen // tm)
    tok = lambda width: pl.BlockSpec((1, tm, width), lambda b, i: (b, i, 0))
    vt_rows = MLA_HEADS * MLA_VT_ROWS
    return pl.pallas_call(
        _ab_in_kernel,
        grid=grid,
        in_specs=[tok(D_MODEL), tok(1), _const_spec((D_MODEL, n_in)), _const_spec((1, HEAD_PAD)),
                  _const_spec((1, MLA_Q_RANK)), _const_spec((1, MLA_KV_RANK)),
                  _const_spec((MLA_Q_RANK, QK_PAD_WIDTH)),
                  _const_spec((MLA_KV_RANK, QK_PAD_WIDTH)), _const_spec((vt_rows, MLA_KV_RANK)),
                  _const_spec((vt_rows, 1))],
        out_specs=[tok(RNN_WIDTH), tok(AB_WIDTH), tok(QK_PAD_WIDTH), tok(QK_PAD_WIDTH),
                   pl.BlockSpec((1, MLA_HEADS, MLA_VT_ROWS, tm), lambda b, i: (b, 0, 0, i))],
        out_shape=[jax.ShapeDtypeStruct((bsz, seqlen, RNN_WIDTH), F32),
                   jax.ShapeDtypeStruct((bsz, seqlen, AB_WIDTH), BF16),
                   jax.ShapeDtypeStruct((bsz, seqlen, QK_PAD_WIDTH), BF16),
                   jax.ShapeDtypeStruct((bsz, seqlen, QK_PAD_WIDTH), BF16),
                   jax.ShapeDtypeStruct((bsz, MLA_HEADS, MLA_VT_ROWS, seqlen), BF16)],
        compiler_params=_params(("parallel", "parallel")),
        name="ab_in_proj",
    )(x, pos_f, w_in_ext, freq, q_norm, kv_norm, wq, wk, wvt, vones)


def _layernorm(z, g, b):
    mu = jnp.mean(z, axis=-1, keepdims=True)
    zc = z - mu
    var = jnp.mean(zc * zc, axis=-1, keepdims=True)
    return zc * lax.rsqrt(var + 1e-5) * g + b


def _rglru_kernel(xr_ref, ya_ref, gate_ref, x_ref, conv_w_ref, conv_b_ref, w_gate_ref,
                  b_gate_ref, lam_ref, w_out_ref, ln_g_ref, ln_b_ref,
                  o_ref, tail_ref, carry_ref, a_ref, u_ref):
    bsz, ts, width = xr_ref.shape

    @pl.when(pl.program_id(0) == 0)
    def _():
        tail_ref[...] = jnp.zeros_like(tail_ref)
        carry_ref[...] = jnp.zeros_like(carry_ref)

    x_blk = jnp.swapaxes(xr_ref[...], 0, 1)
    x_ext = jnp.concatenate([tail_ref[...], x_blk], axis=0)
    tail_ref[...] = x_blk[ts - (CONV_WIDTH - 1):]
    xc = conv_b_ref[...][None]
    for k in range(CONV_WIDTH):
        xc = xc + x_ext[k:k + ts] * conv_w_ref[k:k + 1, :][None]

    xc2 = xc.reshape(ts * bsz, width)
    gates = _dot(xc2.astype(BF16), w_gate_ref[...]) + b_gate_ref[...]
    r = _sigmoid(gates[:, :width])
    i = _sigmoid(gates[:, width:])
    neg_c_softplus = -RG_C * _softplus(-lam_ref[...])
    log_a = r * neg_c_softplus
    a = jnp.exp2(r * (neg_c_softplus * -NEG_LOG2E))
    mult = jnp.sqrt(-jnp.tanh(log_a) * (a * a + 1.0))
    u = mult * (i * xc2)
    a_ref[...] = a.reshape(ts, bsz, width)
    u_ref[...] = u.reshape(ts, bsz, width)

    def step(t, h):
        h = a_ref[t] * h + u_ref[t]
        u_ref[t] = h
        return h

    carry_ref[...] = lax.fori_loop(0, ts, step, carry_ref[...], unroll=8)

    rows = bsz * ts
    h = jnp.swapaxes(u_ref[...], 0, 1).reshape(rows, width)
    gate = gate_ref[...].reshape(rows, AB_WIDTH).astype(F32)
    y_rnn = (h * gate[:, :RNN_WIDTH]).astype(BF16)
    y_mla = (ya_ref[...].reshape(rows, MLA_WIDTH).astype(F32) * gate[:, RNN_WIDTH:]).astype(BF16)
    y = _dot(y_rnn, w_out_ref[:RNN_WIDTH, :]) + _dot(y_mla, w_out_ref[RNN_WIDTH:, :])
    out = _layernorm(DN_ALPHA * x_ref[...].reshape(rows, D_MODEL) + y,
                     ln_g_ref[...], ln_b_ref[...])
    o_ref[...] = out.reshape(bsz, ts, D_MODEL)


def _rglru_call(xr, y_mla, gate, x, conv_w, conv_b, w_gate, b_gate, lam, w_out, ln_g, ln_b, ts):
    bsz, seqlen, width = xr.shape
    blk = lambda w: pl.BlockSpec((bsz, ts, w), lambda i: (0, i, 0))
    return pl.pallas_call(
        _rglru_kernel,
        grid=(seqlen // ts,),
        in_specs=[blk(width), blk(MLA_WIDTH), blk(AB_WIDTH), blk(D_MODEL),
                  _const_spec((CONV_WIDTH, width)), _const_spec((1, width)),
                  _const_spec((width, 2 * width)), _const_spec((1, 2 * width)),
                  _const_spec((1, width)), _const_spec((AB_WIDTH, D_MODEL)),
                  _const_spec((1, D_MODEL)), _const_spec((1, D_MODEL))],
        out_specs=blk(D_MODEL),
        out_shape=jax.ShapeDtypeStruct((bsz, seqlen, D_MODEL), F32),
        scratch_shapes=[pltpu.VMEM((CONV_WIDTH - 1, bsz, width), F32),
                        pltpu.VMEM((bsz, width), F32),
                        pltpu.VMEM((ts, bsz, width), F32),
                        pltpu.VMEM((ts, bsz, width), F32)],
        compiler_params=_params(("arbitrary",)),
        name="rglru_out_proj",
    )(xr, y_mla, gate, x, conv_w, conv_b, w_gate, b_gate, lam, w_out, ln_g, ln_b)


ATTN_ROWS = 256


def _mla_attn_tiles(q_ref, k_ref, vt_ref, o_ref, s_refs, tiles, tq):
    heads = q_ref.shape[2] // HEAD_PAD
    halves = tq // ATTN_ROWS
    units = [(h, r) for h in range(heads) for r in range(halves)]
    steps = [(qi, j) for qi in tiles for j in range(qi + 1)]

    def depth(qi, j, r):
        return (r + 1) * ATTN_ROWS if j == qi else tq

    def scores_into(dst_ref, qi, j):
        for u, (h, r) in enumerate(units):
            w = depth(qi, j, r)
            q_u = q_ref[0, qi * tq + r * ATTN_ROWS:qi * tq + (r + 1) * ATTN_ROWS,
                        h * HEAD_PAD:(h + 1) * HEAD_PAD]
            k_blk = k_ref[0, j * tq:j * tq + w, h * HEAD_PAD:(h + 1) * HEAD_PAD]
            dst_ref[u, :w, :] = lax.dot_general(k_blk, q_u, (((1,), (1,)), ((), ())),
                                                preferred_element_type=F32)

    scores_into(s_refs[0], *steps[0])
    ms, accs = None, None
    for i, (qi, j) in enumerate(steps):
        src_ref = s_refs[i % 2]
        if i + 1 < len(steps):
            scores_into(s_refs[(i + 1) % 2], *steps[i + 1])
        if j == 0:
            ms = [jnp.full((1, ATTN_ROWS), -1e30, F32) for _ in units]
            accs = [jnp.zeros((MLA_VT_ROWS, ATTN_ROWS), F32) for _ in units]
        for u, (h, r) in enumerate(units):
            w = depth(qi, j, r)
            if j == qi:
                key = lax.broadcasted_iota(jnp.int32, (ATTN_ROWS, ATTN_ROWS), 0)
                qry = lax.broadcasted_iota(jnp.int32, (ATTN_ROWS, ATTN_ROWS), 1)
                edge = slice(w - ATTN_ROWS, w)
                src_ref[u, edge, :] = jnp.where(key <= qry, src_ref[u, edge, :], -1e30)
            vt_blk = vt_ref[0, h, :, j * tq:j * tq + w]
            m_new = jnp.maximum(ms[u], jnp.max(src_ref[u, :w, :], axis=0, keepdims=True))
            p = jnp.exp2((src_ref[u, :w, :] - m_new).astype(BF16))
            accs[u] = jnp.exp2(ms[u] - m_new) * accs[u] + _dot(vt_blk, p)
            ms[u] = m_new
        if j == qi:
            outs = [acc[:MLA_V] / acc[MLA_V:MLA_V + 1] for acc in accs]
            for pair in range(heads // 2):
                for r in range(halves):
                    o_t = jnp.concatenate([outs[2 * pair * halves + r],
                                           outs[(2 * pair + 1) * halves + r]], axis=0)
                    o_ref[0, qi * tq + r * ATTN_ROWS:qi * tq + (r + 1) * ATTN_ROWS,
                          pair * HEAD_PAD:(pair + 1) * HEAD_PAD] = o_t.T.astype(o_ref.dtype)


def _mla_attn_kernel(q_ref, k_ref, vt_ref, o_ref, s0_ref, s1_ref, *, tq, groups):
    g = pl.program_id(2)
    for idx, tiles in enumerate(groups):
        @pl.when(g == idx)
        def _(tiles=tiles):
            _mla_attn_tiles(q_ref, k_ref, vt_ref, o_ref, (s0_ref, s1_ref), tiles, tq)


def _mla_attn_call(q, k, vt, tq, heads_per_step=2):
    bsz, seqlen, _ = q.shape
    width = heads_per_step * HEAD_PAD
    n_hsteps = MLA_HEADS // heads_per_step
    out_w = heads_per_step * MLA_V
    n_units = heads_per_step * (tq // ATTN_ROWS)
    n_q = seqlen // tq
    groups = tuple(tuple(sorted({i, n_q - 1 - i})) for i in range((n_q + 1) // 2))
    resident = lambda w: pl.BlockSpec((1, seqlen, w), lambda b, h, g: (b, 0, h))
    return pl.pallas_call(
        functools.partial(_mla_attn_kernel, tq=tq, groups=groups),
        grid=(bsz, n_hsteps, len(groups)),
        in_specs=[resident(width), resident(width),
                  pl.BlockSpec((1, heads_per_step, MLA_VT_ROWS, seqlen),
                               lambda b, h, g: (b, h, 0, 0))],
        out_specs=resident(out_w),
        out_shape=jax.ShapeDtypeStruct((bsz, seqlen, MLA_WIDTH), BF16),
        scratch_shapes=[pltpu.VMEM((n_units, tq, ATTN_ROWS), F32),
                        pltpu.VMEM((n_units, tq, ATTN_ROWS), F32)],
        compiler_params=_params(("parallel", "parallel", "arbitrary")),
        name="mla_attention",
    )(q, k, vt)


SSD_IN_COLS = 512


def _conv_silu_store(proj, tail, w, b, out_ref, cols):
    n_rows = proj.shape[0]
    taps = [w[k:k + 1, :] for k in range(CONV_WIDTH)]
    first = lax.broadcasted_iota(jnp.int32, tail.shape, 0) == 0
    prev = None
    done = []
    for i in range(-1, n_rows // SUBLANES):
        x = tail if i < 0 else proj[i * SUBLANES:(i + 1) * SUBLANES]
        acc = x * taps[0]
        rolled = []
        for k in range(1, CONV_WIDTH):
            r = pltpu.roll(acc, 1, axis=0)
            rolled.append(r)
            acc = (r if prev is None else jnp.where(first, prev[k - 1], r)) + x * taps[k]
        prev = rolled
        if i >= 0:
            done.append(acc + b)
        if len(done) == BF16_ROWS // SUBLANES:
            top = (i + 1) * SUBLANES
            out_ref[0, top - BF16_ROWS:top, cols] = _silu(
                jnp.concatenate(done, axis=0)).astype(BF16)
            done = []


SSD_PROJ_WIDTH = SSD_INNER + SSD_CONV_DIM + LANES


def _ssd_in_stage(step, x_ref, w_ref, dt_bias_ref, conv_w_ref, conv_b_ref,
                  z_ref, xbc_ref, dt_ref, new_ref, old_ref, project=True, finish=True):
    tm = x_ref.shape[1]
    x_off = SSD_INNER
    if finish:
        tails = []
        for c in range(SSD_CONV_DIM // SSD_IN_COLS):
            cols = slice(x_off + c * SSD_IN_COLS, x_off + (c + 1) * SSD_IN_COLS)
            tails.append(jnp.where(step == 1, 0.0, new_ref[tm - SUBLANES:, cols]))

    if project:
        xb = x_ref[0].astype(BF16)
        for c in range(0, SSD_PROJ_WIDTH, SSD_IN_COLS):
            cols = slice(c, min(c + SSD_IN_COLS, SSD_PROJ_WIDTH))
            new_ref[:, cols] = _dot(xb, w_ref[:, cols])
    if not finish:
        return

    for c in range(SSD_INNER // SSD_IN_COLS):
        cols = slice(c * SSD_IN_COLS, (c + 1) * SSD_IN_COLS)
        z_ref[0, :, cols] = _silu(old_ref[:, cols]).astype(BF16)
    for c in range(SSD_CONV_DIM // SSD_IN_COLS):
        cols = slice(c * SSD_IN_COLS, (c + 1) * SSD_IN_COLS)
        p_cols = slice(x_off + c * SSD_IN_COLS, x_off + (c + 1) * SSD_IN_COLS)
        _conv_silu_store(old_ref[:, p_cols], tails[c], conv_w_ref[:, cols], conv_b_ref[:, cols],
                         xbc_ref, cols)
    dt_ref[0] = _softplus(old_ref[:, SSD_INNER + SSD_CONV_DIM:] + dt_bias_ref[...])


def _ssd_in_kernel(x_ref, w_ref, dt_bias_ref, conv_w_ref, conv_b_ref,
                   z_ref, xbc_ref, dt_ref, p0_ref, p1_ref):
    step = pl.program_id(1)
    last = pl.num_programs(1) - 1
    args = (step, x_ref, w_ref, dt_bias_ref, conv_w_ref, conv_b_ref, z_ref, xbc_ref, dt_ref)
    bufs = (p0_ref, p1_ref)

    @pl.when(step == 0)
    def _():
        p1_ref[p1_ref.shape[0] - SUBLANES:, :] = jnp.zeros((SUBLANES, p1_ref.shape[1]), F32)
        _ssd_in_stage(*args, p0_ref, p1_ref, finish=False)

    for parity in range(2):
        @pl.when((step > 0) & (step < last) & (step % 2 == parity))
        def _(parity=parity):
            _ssd_in_stage(*args, bufs[parity], bufs[1 - parity])

        @pl.when((step == last) & (step % 2 == parity))
        def _(parity=parity):
            _ssd_in_stage(*args, bufs[parity], bufs[1 - parity], project=False)


def _ssd_in_call(x, w_all, dt_bias, conv_w, conv_b, tm):
    bsz, seqlen, _ = x.shape
    n_tiles = seqlen // tm
    tile_in = pl.BlockSpec((1, tm, D_MODEL), lambda b, i: (b, jnp.minimum(i, n_tiles - 1), 0))
    tile_out = lambda width: pl.BlockSpec((1, tm, width),
                                          lambda b, i: (b, jnp.maximum(i - 1, 0), 0))
    return pl.pallas_call(
        _ssd_in_kernel,
        grid=(bsz, n_tiles + 1),
        in_specs=[tile_in, _const_spec((D_MODEL, SSD_PROJ_WIDTH)),
                  _const_spec((1, LANES)), _const_spec((CONV_WIDTH, SSD_CONV_DIM)),
                  _const_spec((1, SSD_CONV_DIM))],
        out_specs=[tile_out(SSD_INNER), tile_out(SSD_CONV_DIM), tile_out(LANES)],
        out_shape=[jax.ShapeDtypeStruct((bsz, seqlen, SSD_INNER), BF16),
                   jax.ShapeDtypeStruct((bsz, seqlen, SSD_CONV_DIM), BF16),
                   jax.ShapeDtypeStruct((bsz, seqlen, LANES), F32)],
        scratch_shapes=[pltpu.VMEM((tm, SSD_PROJ_WIDTH), F32),
                        pltpu.VMEM((tm, SSD_PROJ_WIDTH), F32)],
        compiler_params=_params(("parallel", "arbitrary")),
        name="ssd_in_proj",
    )(x, w_all, dt_bias, conv_w, conv_b)


def _cumsum_rows(x):
    n = x.shape[0]
    row = lax.broadcasted_iota(jnp.int32, x.shape, 0)
    shift = 1
    while shift < n:
        x = x + jnp.where(row >= shift, pltpu.roll(x, shift, axis=0), 0.0)
        shift *= 2
    return x


LOG2E = math.log2(math.e)


def _ssd_chunk(xbc_ref, rows, dt, state_ref, a_neg2, d_skip_x):
    L = SSD_CHUNK
    gw = SSD_GROUPS * SSD_STATE

    cs = _cumsum_rows(dt * a_neg2)
    cs_t = cs.T
    dt_t = dt.T
    w_t = dt_t * jnp.exp2(cs_t[:, L - 1:L] - cs_t)

    row = lax.broadcasted_iota(jnp.int32, (L, L), 0)
    col = lax.broadcasted_iota(jnp.int32, (L, L), 1)
    tril = col <= row
    lane = lax.broadcasted_iota(jnp.int32, (L, LANES), 1)
    low = lane < SSD_HEAD_DIM
    keep_lo = low.astype(F32).astype(BF16)
    keep_hi = 1.0 - keep_lo

    hpg = SSD_HEADS // SSD_GROUPS
    y_parts = []
    for g in range(SSD_GROUPS):
        b_g = xbc_ref[0, rows, SSD_INNER + g * SSD_STATE:SSD_INNER + (g + 1) * SSD_STATE]
        c_g = xbc_ref[0, rows, SSD_INNER + gw + g * SSD_STATE:SSD_INNER + gw + (g + 1) * SSD_STATE]
        cb = lax.dot_general(c_g, b_g, (((1,), (1,)), ((), ())),
                             preferred_element_type=F32)
        cb = jnp.where(tril, cb, 0.0)
        b_t = b_g.astype(F32).T
        glanes = slice(g * SSD_GROUP_WIDTH, (g + 1) * SSD_GROUP_WIDTH)
        y_off = _dot(c_g, state_ref[:, glanes].astype(BF16))
        for pair in range(hpg // 2):
            h0 = g * hpg + 2 * pair
            plane = slice(h0 * SSD_HEAD_DIM, (h0 + 2) * SSD_HEAD_DIM)
            tops, bots, decays = [], [], []
            for h in (h0, h0 + 1):
                cs_col = jnp.broadcast_to(cs[:, h:h + 1], (L, LANES))
                seg = cs_col - cs_t[h:h + 1, :]
                m_h = jnp.exp2(jnp.minimum(seg, 0.0)) * (cb * dt_t[h:h + 1, :])
                tops.append(m_h.astype(BF16))
                bots.append((b_t * w_t[h:h + 1, :]).astype(BF16))
                decays.append(jnp.exp2(cs_col))
            lhs = jnp.concatenate([jnp.concatenate(tops, axis=1),
                                   jnp.concatenate(bots, axis=1)], axis=0)
            x_pair = xbc_ref[0, rows, plane]
            rhs = jnp.concatenate([x_pair * keep_lo, x_pair * keep_hi], axis=0)
            res = _dot(lhs, rhs)
            dec = jnp.where(low, decays[0], decays[1])
            off = pair * 2 * SSD_HEAD_DIM
            y_pair = (res[:L] + dec * y_off[:, off:off + 2 * SSD_HEAD_DIM]
                      + d_skip_x[:, plane] * x_pair.astype(F32))
            y_parts.append(y_pair)
            state_ref[:, plane] = state_ref[:, plane] * dec[L - 1:L, :] + res[L:]
    return jnp.concatenate(y_parts, axis=1)


def _ssd_kernel(z_ref, xbc_ref, dt_ref, x_ref, a_log_ref, d_skip_ref, norm_ref, w_out_ref,
                g_ref, b_ref, o_ref, state_ref, *, n_chunks):
    L = SSD_CHUNK

    @pl.when(pl.program_id(1) == 0)
    def _():
        state_ref[...] = jnp.zeros_like(state_ref)

    a_neg2 = -jnp.exp(a_log_ref[...]) * LOG2E
    for c in range(n_chunks):
        rows = slice(c * L, (c + 1) * L)
        y = _ssd_chunk(xbc_ref, rows, dt_ref[0, rows, :], state_ref, a_neg2, d_skip_ref[...])
        y = y * z_ref[0, rows, :].astype(F32)
        parts = []
        for g in range(SSD_GROUPS):
            yg = y[:, g * SSD_GROUP_WIDTH:(g + 1) * SSD_GROUP_WIDTH]
            parts.append(yg * lax.rsqrt(jnp.mean(yg * yg, axis=-1, keepdims=True) + 1e-6))
        yn = (jnp.concatenate(parts, axis=1) * norm_ref[...]).astype(BF16)
        out = _dot(yn, w_out_ref[...])
        o_ref[0, rows, :] = _layernorm(DN_ALPHA * x_ref[0, rows, :] + out, g_ref[...], b_ref[...])


def _ssd_call(z, xbc, dt, x, a_log, d_skip_x, norm_w, w_out, ln_g, ln_b, n_chunks):
    bsz, seqlen, _ = x.shape
    tm = n_chunks * SSD_CHUNK
    tok = lambda width: pl.BlockSpec((1, tm, width), lambda b, i: (b, i, 0))
    return pl.pallas_call(
        functools.partial(_ssd_kernel, n_chunks=n_chunks),
        grid=(bsz, seqlen // tm),
        in_specs=[tok(SSD_INNER), tok(SSD_CONV_DIM), tok(LANES), tok(D_MODEL),
                  _const_spec((1, LANES)), _const_spec((1, SSD_INNER)),
                  _const_spec((1, SSD_INNER)), _const_spec((SSD_INNER, D_MODEL)),
                  _const_spec((1, D_MODEL)), _const_spec((1, D_MODEL))],
        out_specs=tok(D_MODEL),
        out_shape=jax.ShapeDtypeStruct((bsz, seqlen, D_MODEL), F32),
        scratch_shapes=[pltpu.VMEM((SSD_STATE, SSD_INNER), F32)],
        compiler_params=_params(("parallel", "arbitrary")),
        name="ssd_scan_out",
    )(z, xbc, dt, x, a_log, d_skip_x, norm_w, w_out, ln_g, ln_b)


def _rot_cols(w):
    half = MLA_ROPE // 2
    return jnp.concatenate([-w[..., half:], w[..., :half]], axis=-1)


def _pad_heads(w_heads):
    r, h, c = w_heads.shape
    return jnp.pad(w_heads, ((0, 0), (0, 0), (0, HEAD_PAD - c))).reshape(r, h * HEAD_PAD)


def _block_diag(w):
    h, d, _ = w.shape
    eye = jnp.eye(h, dtype=w.dtype)
    return (eye[:, None, :, None] * w[:, :, None, :]).reshape(h * d, h * d)


def _layer0(x, pos_f, freq, w_in, conv_w, conv_b, gate_a_w, gate_a_b, gate_x_w, gate_x_b, lam,
            q_norm, kv_norm, w_uq, w_ukv, w_out, ln_g, ln_b, tm, ts, tq):
    o_kr = RNN_WIDTH + AB_WIDTH + MLA_Q_RANK + MLA_KV_RANK
    w_kr = w_in[:, o_kr:]
    w_in_ext = jnp.concatenate([w_in[:, :o_kr], jnp.zeros_like(w_in[:, :MLA_NOPE]), w_kr,
                                _rot_cols(w_kr)], axis=1).astype(BF16)

    uq = w_uq.reshape(MLA_Q_RANK, MLA_HEADS, MLA_NOPE + MLA_ROPE)
    wq = _pad_heads(jnp.concatenate([uq, _rot_cols(uq[..., MLA_NOPE:])], axis=-1)).astype(BF16)
    ukv = w_ukv.reshape(MLA_KV_RANK, MLA_HEADS, MLA_NOPE + MLA_V)
    wk = _pad_heads(ukv[..., :MLA_NOPE]).astype(BF16)
    wvt = jnp.pad(jnp.transpose(ukv[..., MLA_NOPE:], (1, 2, 0)),
                  ((0, 0), (0, MLA_VT_ROWS - MLA_V), (0, 0)))
    wvt = wvt.reshape(MLA_HEADS * MLA_VT_ROWS, MLA_KV_RANK).astype(BF16)
    vones = jnp.tile((jnp.arange(MLA_VT_ROWS) >= MLA_V).astype(F32), MLA_HEADS)[:, None]

    xr, gate, q, k, vt = _ab_in_call(x, pos_f, w_in_ext, freq, q_norm[None], kv_norm[None],
                                     wq, wk, wvt, vones, tm)

    w_gate = jnp.concatenate([_block_diag(gate_a_w), _block_diag(gate_x_w)], axis=1).astype(BF16)
    b_gate = jnp.concatenate([gate_a_b, gate_x_b])[None]
    y_mla = _mla_attn_call(q, k, vt, tq)
    return _rglru_call(xr, y_mla, gate, x, conv_w, conv_b[None], w_gate, b_gate, lam[None],
                       w_out.astype(BF16), ln_g[None], ln_b[None], ts)


def _layer1(x, w_in, conv_w, conv_b, dt_bias, a_log, d_skip, norm_w, w_out, ln_g, ln_b,
            tm, n_chunks):
    pad_h = lambda a: jnp.pad(a, ((0, 0), (0, LANES - SSD_HEADS)))
    w_all = pad_h(w_in).astype(BF16)
    z, xbc, dt = _ssd_in_call(x, w_all, pad_h(dt_bias[None]), conv_w, conv_b[None], tm)
    d_skip_x = jnp.repeat(d_skip, SSD_HEAD_DIM)[None]
    return _ssd_call(z, xbc, dt, x, pad_h(a_log[None]), d_skip_x, norm_w[None],
                     w_out.astype(BF16), ln_g[None], ln_b[None], n_chunks)


def kernel(x, positions, ab_w_in, ab_conv_w, ab_conv_b, ab_gate_a_w, ab_gate_a_b, ab_gate_x_w,
           ab_gate_x_b, ab_lambda, mla_q_norm, mla_kv_norm, mla_w_uq, mla_w_ukv, ab_w_out,
           ab_ln_g, ab_ln_b, ssd_w_in, ssd_conv_w, ssd_conv_b, ssd_dt_bias, ssd_a_log, ssd_d,
           ssd_norm, ssd_w_out, ssd_ln_g, ssd_ln_b):
    seqlen = x.shape[1]
    tm = min(512, seqlen)
    tm_ssd = min(256, seqlen)
    ts = min(128, seqlen)
    tq = min(512, seqlen)
    n_chunks = min(4, seqlen // SSD_CHUNK)

    inv_freq = ROPE_THETA ** (-jnp.arange(0, MLA_ROPE, 2, dtype=F32) / MLA_ROPE)
    freq = jnp.concatenate([jnp.zeros((MLA_NOPE,), F32), inv_freq, inv_freq,
                            jnp.zeros((HEAD_PAD - MLA_NOPE - MLA_ROPE,), F32)])[None]
    pos_f = positions.astype(F32)[..., None]

    for layer in range(DEPTH):
        j = layer // 2
        if layer % 2 == 0:
            x = _layer0(x, pos_f, freq, ab_w_in[j], ab_conv_w[j], ab_conv_b[j], ab_gate_a_w[j],
                        ab_gate_a_b[j], ab_gate_x_w[j], ab_gate_x_b[j], ab_lambda[j],
                        mla_q_norm[j], mla_kv_norm[j], mla_w_uq[j], mla_w_ukv[j], ab_w_out[j],
                        ab_ln_g[j], ab_ln_b[j], tm, ts, tq)
        else:
            x = _layer1(x, ssd_w_in[j], ssd_conv_w[j], ssd_conv_b[j], ssd_dt_bias[j],
                        ssd_a_log[j], ssd_d[j], ssd_norm[j], ssd_w_out[j], ssd_ln_g[j],
                        ssd_ln_b[j], tm_ssd, n_chunks)
    return x
```

```python
import functools
import math

import jax
import jax.numpy as jnp
from jax import lax
from jax.experimental import pallas as pl
from jax.experimental.pallas import tpu as pltpu

D_MODEL = 1024
DEPTH = 2
DN_ALPHA = (2.0 * DEPTH) ** 0.25

RNN_WIDTH = 512
RNN_HEADS = 8
RNN_HEAD_DIM = RNN_WIDTH // RNN_HEADS
CONV_WIDTH = 4
RG_C = 8.0

MLA_HEADS = 8
MLA_NOPE = 64
MLA_ROPE = 32
MLA_V = 64
MLA_Q_RANK = 256
MLA_KV_RANK = 128
MLA_WIDTH = MLA_HEADS * MLA_V
ROPE_THETA = 10000.0
AB_WIDTH = RNN_WIDTH + MLA_WIDTH

SSD_INNER = 2048
SSD_HEAD_DIM = 64
SSD_HEADS = 32
SSD_GROUPS = 4
SSD_STATE = 128
SSD_CHUNK = 128
SSD_CONV_DIM = SSD_INNER + 2 * SSD_GROUPS * SSD_STATE
SSD_GROUP_WIDTH = SSD_INNER // SSD_GROUPS

LANES = 128
SUBLANES = 8
BF16_ROWS = 2 * SUBLANES
HEAD_PAD = LANES
MLA_VT_ROWS = MLA_V + BF16_ROWS
QK_PAD_WIDTH = MLA_HEADS * HEAD_PAD
VMEM_LIMIT = 56 * 1024 * 1024

BF16 = jnp.bfloat16
F32 = jnp.float32


NEG_LOG2E = -math.log2(math.e)


def _exp_neg(x):
    return jnp.exp2(x * NEG_LOG2E)


def _sigmoid(x):
    return 1.0 / (1.0 + _exp_neg(x))


def _silu(x):
    return x * _sigmoid(x)


def _log1p(y):
    u = 1.0 + y
    return jnp.where(u == 1.0, y, jnp.log(u) * (y / (u - 1.0)))


def _softplus(x):
    return jnp.maximum(x, 0.0) + _log1p(_exp_neg(jnp.abs(x)))


def _dot(a, b):
    return jnp.dot(a, b, preferred_element_type=F32)


def _const_spec(shape):
    zeros = (0,) * len(shape)
    return pl.BlockSpec(shape, lambda *_: zeros)


def _params(semantics, flags=None):
    return pltpu.CompilerParams(dimension_semantics=semantics,
                                vmem_limit_bytes=VMEM_LIMIT, flags=flags)


def _ab_in_kernel(x_ref, pos_ref, w_in_ref, freq_ref, qn_ref, kvn_ref, wq_ref,
                  wk_ref, wvt_ref, vones_ref,
                  xr_ref, gate_ref, q_ref, k_ref, vt_ref):
    xb = x_ref[0].astype(BF16)
    proj = _dot(xb, w_in_ref[...])
    o_gate = RNN_WIDTH
    o_cq = o_gate + AB_WIDTH
    o_ckv = o_cq + MLA_Q_RANK
    o_kr = o_ckv + MLA_KV_RANK

    xr_ref[0] = proj[:, :o_gate]
    gate_ref[0] = _silu(proj[:, o_gate:o_cq]).astype(BF16)

    pos_rows = jnp.broadcast_to(pos_ref[0], (HEAD_PAD, pos_ref.shape[2])).T
    ang = pos_rows * freq_ref[...]
    cos = jnp.cos(ang)
    sin = jnp.sin(ang)
    to_rope_lanes = HEAD_PAD - MLA_ROPE

    c_q = proj[:, o_cq:o_ckv]
    c_q = c_q * lax.rsqrt(jnp.mean(c_q * c_q, axis=-1, keepdims=True) + 1e-6) * qn_ref[...]
    q_all = _dot(c_q.astype(BF16), wq_ref[...])
    scale = (MLA_NOPE + MLA_ROPE) ** -0.5 * math.log2(math.e)
    for h in range(MLA_HEADS):
        q_h = q_all[:, h * HEAD_PAD:(h + 1) * HEAD_PAD]
        q_h = q_h * cos + pltpu.roll(q_h, to_rope_lanes, axis=1) * sin
        q_ref[0, :, h * HEAD_PAD:(h + 1) * HEAD_PAD] = (q_h * scale).astype(BF16)

    c_kv = proj[:, o_ckv:o_kr]
    c_kv = c_kv * lax.rsqrt(jnp.mean(c_kv * c_kv, axis=-1, keepdims=True) + 1e-6) * kvn_ref[...]
    c_kvb = c_kv.astype(BF16)
    kr = proj[:, o_kr:o_kr + HEAD_PAD]
    lane = lax.broadcasted_iota(jnp.int32, kr.shape, 1)
    k_rope = jnp.where(lane < MLA_NOPE + MLA_ROPE,
                       kr * cos + pltpu.roll(kr, to_rope_lanes, axis=1) * sin, 0.0)
    k = _dot(c_kvb, wk_ref[...]) + jnp.concatenate([k_rope] * MLA_HEADS, axis=1)
    k_ref[0] = k.astype(BF16)
    vt = lax.dot_general(wvt_ref[...], c_kvb, (((1,), (1,)), ((), ())),
                         preferred_element_type=F32) + vones_ref[...]
    vt_ref[0] = vt.reshape(MLA_HEADS, MLA_VT_ROWS, vt.shape[1]).astype(BF16)


def _ab_in_call(x, pos_f, w_in_ext, freq, q_norm, kv_norm, wq, wk, wvt, vones, tm):
    bsz, seqlen, _ = x.shape
    n_in = w_in_ext.shape[1]
    grid = (bsz, seqlen // tm)
    tok = lambda width: pl.BlockSpec((1, tm, width), lambda b, i: (b, i, 0))
    vt_rows = MLA_HEADS * MLA_VT_ROWS
    return pl.pallas_call(
        _ab_in_kernel,
        grid=grid,
        in_specs=[tok(D_MODEL), pl.BlockSpec((1, 1, tm), lambda b, i: (b, 0, i)),
                  _const_spec((D_MODEL, n_in)), _const_spec((1, HEAD_PAD)),
                  _const_spec((1, MLA_Q_RANK)), _const_spec((1, MLA_KV_RANK)),
                  _const_spec((MLA_Q_RANK, QK_PAD_WIDTH)),
                  _const_spec((MLA_KV_RANK, QK_PAD_WIDTH)), _const_spec((vt_rows, MLA_KV_RANK)),
                  _const_spec((vt_rows, 1))],
        out_specs=[tok(RNN_WIDTH), tok(AB_WIDTH), tok(QK_PAD_WIDTH), tok(QK_PAD_WIDTH),
                   pl.BlockSpec((1, MLA_HEADS, MLA_VT_ROWS, tm), lambda b, i: (b, 0, 0, i))],
        out_shape=[jax.ShapeDtypeStruct((bsz, seqlen, RNN_WIDTH), F32),
                   jax.ShapeDtypeStruct((bsz, seqlen, AB_WIDTH), BF16),
                   jax.ShapeDtypeStruct((bsz, seqlen, QK_PAD_WIDTH), BF16),
                   jax.ShapeDtypeStruct((bsz, seqlen, QK_PAD_WIDTH), BF16),
                   jax.ShapeDtypeStruct((bsz, MLA_HEADS, MLA_VT_ROWS, seqlen), BF16)],
        compiler_params=_params(("parallel", "parallel")),
        name="ab_in_proj",
    )(x, pos_f, w_in_ext, freq, q_norm, kv_norm, wq, wk, wvt, vones)


def _layernorm(z, g, b):
    mu = jnp.mean(z, axis=-1, keepdims=True)
    zc = z - mu
    var = jnp.mean(zc * zc, axis=-1, keepdims=True)
    return zc * lax.rsqrt(var + 1e-5) * g + b


def _rglru_kernel(xr_ref, ya_ref, gate_ref, x_ref, conv_w_ref, conv_b_ref, w_gate_ref,
                  b_gate_ref, lam_ref, w_out_ref, ln_g_ref, ln_b_ref,
                  o_ref, tail_ref, carry_ref, a_ref, u_ref):
    bsz, ts, width = xr_ref.shape

    @pl.when(pl.program_id(0) == 0)
    def _():
        tail_ref[...] = jnp.zeros_like(tail_ref)
        carry_ref[...] = jnp.zeros_like(carry_ref)

    x_blk = jnp.swapaxes(xr_ref[...], 0, 1)
    x_ext = jnp.concatenate([tail_ref[...], x_blk], axis=0)
    tail_ref[...] = x_blk[ts - (CONV_WIDTH - 1):]
    xc = conv_b_ref[...][None]
    for k in range(CONV_WIDTH):
        xc = xc + x_ext[k:k + ts] * conv_w_ref[k:k + 1, :][None]

    xc2 = xc.reshape(ts * bsz, width)
    gates = _dot(xc2.astype(BF16), w_gate_ref[...]) + b_gate_ref[...]
    r = _sigmoid(gates[:, :width])
    i = _sigmoid(gates[:, width:])
    neg_c_softplus = -RG_C * _softplus(-lam_ref[...])
    log_a = r * neg_c_softplus
    a = jnp.exp2(r * (neg_c_softplus * -NEG_LOG2E))
    mult = jnp.sqrt(-jnp.tanh(log_a) * (a * a + 1.0))
    u = mult * (i * xc2)
    a_ref[...] = a.reshape(ts, bsz, width)
    u_ref[...] = u.reshape(ts, bsz, width)

    def step(t, h):
        h = a_ref[t] * h + u_ref[t]
        u_ref[t] = h
        return h

    carry_ref[...] = lax.fori_loop(0, ts, step, carry_ref[...], unroll=8)

    rows = bsz * ts
    h = jnp.swapaxes(u_ref[...], 0, 1).reshape(rows, width)
    gate = gate_ref[...].reshape(rows, AB_WIDTH).astype(F32)
    y_rnn = (h * gate[:, :RNN_WIDTH]).astype(BF16)
    y_mla = (ya_ref[...].reshape(rows, MLA_WIDTH).astype(F32) * gate[:, RNN_WIDTH:]).astype(BF16)
    y = _dot(y_rnn, w_out_ref[:RNN_WIDTH, :]) + _dot(y_mla, w_out_ref[RNN_WIDTH:, :])
    out = _layernorm(DN_ALPHA * x_ref[...].reshape(rows, D_MODEL) + y,
                     ln_g_ref[...], ln_b_ref[...])
    o_ref[...] = out.reshape(bsz, ts, D_MODEL)


def _rglru_call(xr, y_mla, gate, x, conv_w, conv_b, w_gate, b_gate, lam, w_out, ln_g, ln_b, ts):
    bsz, seqlen, width = xr.shape
    blk = lambda w: pl.BlockSpec((bsz, ts, w), lambda i: (0, i, 0))
    return pl.pallas_call(
        _rglru_kernel,
        grid=(seqlen // ts,),
        in_specs=[blk(width), blk(MLA_WIDTH), blk(AB_WIDTH), blk(D_MODEL),
                  _const_spec((CONV_WIDTH, width)), _const_spec((1, width)),
                  _const_spec((width, 2 * width)), _const_spec((1, 2 * width)),
                  _const_spec((1, width)), _const_spec((AB_WIDTH, D_MODEL)),
                  _const_spec((1, D_MODEL)), _const_spec((1, D_MODEL))],
        out_specs=blk(D_MODEL),
        out_shape=jax.ShapeDtypeStruct((bsz, seqlen, D_MODEL), F32),
        scratch_shapes=[pltpu.VMEM((CONV_WIDTH - 1, bsz, width), F32),
                        pltpu.VMEM((bsz, width), F32),
                        pltpu.VMEM((ts, bsz, width), F32),
                        pltpu.VMEM((ts, bsz, width), F32)],
        compiler_params=_params(("arbitrary",)),
        name="rglru_out_proj",
    )(xr, y_mla, gate, x, conv_w, conv_b, w_gate, b_gate, lam, w_out, ln_g, ln_b)


ATTN_ROWS = 256


def _mla_attn_tiles(q_ref, k_ref, vt_ref, o_ref, s_refs, tiles, tq):
    heads = q_ref.shape[2] // HEAD_PAD
    halves = tq // ATTN_ROWS
    units = [(h, r) for h in range(heads) for r in range(halves)]
    steps = [(qi, j) for qi in tiles for j in range(qi + 1)]

    def depth(qi, j, r):
        return (r + 1) * ATTN_ROWS if j == qi else tq

    def scores_into(dst_ref, qi, j):
        for u, (h, r) in enumerate(units):
            w = depth(qi, j, r)
            q_u = q_ref[0, qi * tq + r * ATTN_ROWS:qi * tq + (r + 1) * ATTN_ROWS,
                        h * HEAD_PAD:(h + 1) * HEAD_PAD]
            k_blk = k_ref[0, j * tq:j * tq + w, h * HEAD_PAD:(h + 1) * HEAD_PAD]
            dst_ref[u, :w, :] = lax.dot_general(k_blk, q_u, (((1,), (1,)), ((), ())),
                                                preferred_element_type=F32)

    scores_into(s_refs[0], *steps[0])
    ms, accs = None, None
    for i, (qi, j) in enumerate(steps):
        src_ref = s_refs[i % 2]
        if i + 1 < len(steps):
            scores_into(s_refs[(i + 1) % 2], *steps[i + 1])
        if j == 0:
            ms = [jnp.full((1, ATTN_ROWS), -1e30, F32) for _ in units]
            accs = [jnp.zeros((MLA_VT_ROWS, ATTN_ROWS), F32) for _ in units]
        for u, (h, r) in enumerate(units):
            w = depth(qi, j, r)
            if j == qi:
                key = lax.broadcasted_iota(jnp.int32, (ATTN_ROWS, ATTN_ROWS), 0)
                qry = lax.broadcasted_iota(jnp.int32, (ATTN_ROWS, ATTN_ROWS), 1)
                edge = slice(w - ATTN_ROWS, w)
                src_ref[u, edge, :] = jnp.where(key <= qry, src_ref[u, edge, :], -1e30)
            vt_blk = vt_ref[0, h, :, j * tq:j * tq + w]
            m_new = jnp.maximum(ms[u], jnp.max(src_ref[u, :w, :], axis=0, keepdims=True))
            p = jnp.exp2((src_ref[u, :w, :] - m_new).astype(BF16))
            accs[u] = jnp.exp2(ms[u] - m_new) * accs[u] + _dot(vt_blk, p)
            ms[u] = m_new
        if j == qi:
            outs = [acc[:MLA_V] / acc[MLA_V:MLA_V + 1] for acc in accs]
            for pair in range(heads // 2):
                for r in range(halves):
                    o_t = jnp.concatenate([outs[2 * pair * halves + r],
                                           outs[(2 * pair + 1) * halves + r]], axis=0)
                    o_ref[0, qi * tq + r * ATTN_ROWS:qi * tq + (r + 1) * ATTN_ROWS,
                          pair * HEAD_PAD:(pair + 1) * HEAD_PAD] = o_t.T.astype(o_ref.dtype)


def _mla_attn_kernel(q_ref, k_ref, vt_ref, o_ref, s0_ref, s1_ref, *, tq, groups):
    g = pl.program_id(2)
    for idx, tiles in enumerate(groups):
        @pl.when(g == idx)
        def _(tiles=tiles):
            _mla_attn_tiles(q_ref, k_ref, vt_ref, o_ref, (s0_ref, s1_ref), tiles, tq)


def _mla_attn_call(q, k, vt, tq, heads_per_step=2):
    bsz, seqlen, _ = q.shape
    width = heads_per_step * HEAD_PAD
    n_hsteps = MLA_HEADS // heads_per_step
    out_w = heads_per_step * MLA_V
    n_units = heads_per_step * (tq // ATTN_ROWS)
    n_q = seqlen // tq
    groups = tuple(tuple(sorted({i, n_q - 1 - i})) for i in range((n_q + 1) // 2))
    resident = lambda w: pl.BlockSpec((1, seqlen, w), lambda b, h, g: (b, 0, h))
    return pl.pallas_call(
        functools.partial(_mla_attn_kernel, tq=tq, groups=groups),
        grid=(bsz, n_hsteps, len(groups)),
        in_specs=[resident(width), resident(width),
                  pl.BlockSpec((1, heads_per_step, MLA_VT_ROWS, seqlen),
                               lambda b, h, g: (b, h, 0, 0))],
        out_specs=resident(out_w),
        out_shape=jax.ShapeDtypeStruct((bsz, seqlen, MLA_WIDTH), BF16),
        scratch_shapes=[pltpu.VMEM((n_units, tq, ATTN_ROWS), F32),
                        pltpu.VMEM((n_units, tq, ATTN_ROWS), F32)],
        compiler_params=_params(("parallel", "parallel", "arbitrary")),
        name="mla_attention",
    )(q, k, vt)


SSD_IN_COLS = 512


def _conv_silu_store(proj, tail, w, b, out_ref, cols):
    n_rows = proj.shape[0]
    taps = [w[k:k + 1, :] for k in range(CONV_WIDTH)]
    first = lax.broadcasted_iota(jnp.int32, tail.shape, 0) == 0
    prev = None
    done = []
    for i in range(-1, n_rows // SUBLANES):
        x = tail if i < 0 else proj[i * SUBLANES:(i + 1) * SUBLANES]
        acc = x * taps[0]
        rolled = []
        for k in range(1, CONV_WIDTH):
            r = pltpu.roll(acc, 1, axis=0)
            rolled.append(r)
            acc = (r if prev is None else jnp.where(first, prev[k - 1], r)) + x * taps[k]
        prev = rolled
        if i >= 0:
            done.append(acc + b)
        if len(done) == BF16_ROWS // SUBLANES:
            top = (i + 1) * SUBLANES
            out_ref[0, top - BF16_ROWS:top, cols] = _silu(
                jnp.concatenate(done, axis=0)).astype(BF16)
            done = []


SSD_PROJ_WIDTH = SSD_INNER + SSD_CONV_DIM + LANES


def _ssd_in_stage(step, x_ref, w_ref, wdt_ref, dt_bias_ref, conv_w_ref, conv_b_ref,
                  z_ref, xbc_ref, dt_ref, new_ref, old_ref, project=True, finish=True):
    tm = x_ref.shape[1]
    x_off = SSD_INNER
    if finish:
        tails = []
        for c in range(SSD_CONV_DIM // SSD_IN_COLS):
            cols = slice(x_off + c * SSD_IN_COLS, x_off + (c + 1) * SSD_IN_COLS)
            tails.append(jnp.where(step == 1, 0.0, new_ref[tm - SUBLANES:, cols]))

    if project:
        xb = x_ref[0].astype(BF16)
        for c in range(0, SSD_INNER + SSD_CONV_DIM, SSD_IN_COLS):
            cols = slice(c, c + SSD_IN_COLS)
            new_ref[:, cols] = _dot(xb, w_ref[:, cols])
        new_ref[:, SSD_INNER + SSD_CONV_DIM:] = _dot(xb, wdt_ref[...])
    if not finish:
        return

    for c in range(SSD_INNER // SSD_IN_COLS):
        cols = slice(c * SSD_IN_COLS, (c + 1) * SSD_IN_COLS)
        z_ref[0, :, cols] = _silu(old_ref[:, cols]).astype(BF16)
    for c in range(SSD_CONV_DIM // SSD_IN_COLS):
        cols = slice(c * SSD_IN_COLS, (c + 1) * SSD_IN_COLS)
        p_cols = slice(x_off + c * SSD_IN_COLS, x_off + (c + 1) * SSD_IN_COLS)
        _conv_silu_store(old_ref[:, p_cols], tails[c], conv_w_ref[:, cols], conv_b_ref[:, cols],
                         xbc_ref, cols)
    dt_ref[0] = _softplus(old_ref[:, SSD_INNER + SSD_CONV_DIM:] + dt_bias_ref[...])


def _ssd_in_kernel(x_ref, w_ref, wdt_ref, dt_bias_ref, conv_w_ref, conv_b_ref,
                   z_ref, xbc_ref, dt_ref, p0_ref, p1_ref):
    step = pl.program_id(1)
    last = pl.num_programs(1) - 1
    args = (step, x_ref, w_ref, wdt_ref, dt_bias_ref, conv_w_ref, conv_b_ref,
            z_ref, xbc_ref, dt_ref)
    bufs = (p0_ref, p1_ref)

    @pl.when(step == 0)
    def _():
        p1_ref[p1_ref.shape[0] - SUBLANES:, :] = jnp.zeros((SUBLANES, p1_ref.shape[1]), F32)
        _ssd_in_stage(*args, p0_ref, p1_ref, finish=False)

    for parity in range(2):
        @pl.when((step > 0) & (step < last) & (step % 2 == parity))
        def _(parity=parity):
            _ssd_in_stage(*args, bufs[parity], bufs[1 - parity])

        @pl.when((step == last) & (step % 2 == parity))
        def _(parity=parity):
            _ssd_in_stage(*args, bufs[parity], bufs[1 - parity], project=False)


def _ssd_in_call(x, w_zx, wdt, dt_bias, conv_w, conv_b, tm):
    bsz, seqlen, _ = x.shape
    n_tiles = seqlen // tm
    tile_in = pl.BlockSpec((1, tm, D_MODEL), lambda b, i: (b, jnp.minimum(i, n_tiles - 1), 0))
    tile_out = lambda width: pl.BlockSpec((1, tm, width),
                                          lambda b, i: (b, jnp.maximum(i - 1, 0), 0))
    return pl.pallas_call(
        _ssd_in_kernel,
        grid=(bsz, n_tiles + 1),
        in_specs=[tile_in, _const_spec((D_MODEL, SSD_INNER + SSD_CONV_DIM)),
                  _const_spec((D_MODEL, LANES)),
                  _const_spec((1, LANES)), _const_spec((CONV_WIDTH, SSD_CONV_DIM)),
                  _const_spec((1, SSD_CONV_DIM))],
        out_specs=[tile_out(SSD_INNER), tile_out(SSD_CONV_DIM), tile_out(LANES)],
        out_shape=[jax.ShapeDtypeStruct((bsz, seqlen, SSD_INNER), BF16),
                   jax.ShapeDtypeStruct((bsz, seqlen, SSD_CONV_DIM), BF16),
                   jax.ShapeDtypeStruct((bsz, seqlen, LANES), F32)],
        scratch_shapes=[pltpu.VMEM((tm, SSD_PROJ_WIDTH), F32),
                        pltpu.VMEM((tm, SSD_PROJ_WIDTH), F32)],
        compiler_params=_params(("parallel", "arbitrary")),
        name="ssd_in_proj",
    )(x, w_zx, wdt, dt_bias, conv_w, conv_b)


def _cumsum_rows(x):
    n = x.shape[0]
    row = lax.broadcasted_iota(jnp.int32, x.shape, 0)
    shift = 1
    while shift < n:
        x = x + jnp.where(row >= shift, pltpu.roll(x, shift, axis=0), 0.0)
        shift *= 2
    return x


LOG2E = math.log2(math.e)


def _ssd_chunk(xbc_ref, rows, dt, state_ref, a_neg2, d_skip_x):
    L = SSD_CHUNK
    gw = SSD_GROUPS * SSD_STATE

    cs = _cumsum_rows(dt * a_neg2)
    cs_t = cs.T
    dt_t = dt.T
    w_t = dt_t * jnp.exp2(cs_t[:, L - 1:L] - cs_t)

    row = lax.broadcasted_iota(jnp.int32, (L, L), 0)
    col = lax.broadcasted_iota(jnp.int32, (L, L), 1)
    tril = col <= row
    lane = lax.broadcasted_iota(jnp.int32, (L, LANES), 1)
    low = lane < SSD_HEAD_DIM
    keep_lo = low.astype(F32).astype(BF16)
    keep_hi = 1.0 - keep_lo

    hpg = SSD_HEADS // SSD_GROUPS
    y_parts = []
    for g in range(SSD_GROUPS):
        b_g = xbc_ref[0, rows, SSD_INNER + g * SSD_STATE:SSD_INNER + (g + 1) * SSD_STATE]
        c_g = xbc_ref[0, rows, SSD_INNER + gw + g * SSD_STATE:SSD_INNER + gw + (g + 1) * SSD_STATE]
        cb = lax.dot_general(c_g, b_g, (((1,), (1,)), ((), ())),
                             preferred_element_type=F32)
        cb = jnp.where(tril, cb, 0.0)
        b_t = b_g.astype(F32).T
        glanes = slice(g * SSD_GROUP_WIDTH, (g + 1) * SSD_GROUP_WIDTH)
        y_off = _dot(c_g, state_ref[:, glanes].astype(BF16))
        for pair in range(hpg // 2):
            h0 = g * hpg + 2 * pair
            plane = slice(h0 * SSD_HEAD_DIM, (h0 + 2) * SSD_HEAD_DIM)
            tops, bots, decays = [], [], []
            for h in (h0, h0 + 1):
                cs_col = jnp.broadcast_to(cs[:, h:h + 1], (L, LANES))
                seg = cs_col - cs_t[h:h + 1, :]
                m_h = jnp.exp2(jnp.minimum(seg, 0.0)) * (cb * dt_t[h:h + 1, :])
                tops.append(m_h.astype(BF16))
                bots.append((b_t * w_t[h:h + 1, :]).astype(BF16))
                decays.append(jnp.exp2(cs_col))
            lhs = jnp.concatenate([jnp.concatenate(tops, axis=1),
                                   jnp.concatenate(bots, axis=1)], axis=0)
            x_pair = xbc_ref[0, rows, plane]
            rhs = jnp.concatenate([x_pair * keep_lo, x_pair * keep_hi], axis=0)
            res = _dot(lhs, rhs)
            dec = jnp.where(low, decays[0], decays[1])
            off = pair * 2 * SSD_HEAD_DIM
            y_pair = (res[:L] + dec * y_off[:, off:off + 2 * SSD_HEAD_DIM]
                      + d_skip_x[:, plane] * x_pair.astype(F32))
            y_parts.append(y_pair)
            state_ref[:, plane] = state_ref[:, plane] * dec[L - 1:L, :] + res[L:]
    return jnp.concatenate(y_parts, axis=1)


def _ssd_kernel(z_ref, xbc_ref, dt_ref, x_ref, a_log_ref, d_skip_ref, norm_ref, w_out_ref,
                g_ref, b_ref, o_ref, state_ref, *, n_chunks):
    L = SSD_CHUNK

    @pl.when(pl.program_id(1) == 0)
    def _():
        state_ref[...] = jnp.zeros_like(state_ref)

    a_neg2 = -jnp.exp(a_log_ref[...]) * LOG2E
    for c in range(n_chunks):
        rows = slice(c * L, (c + 1) * L)
        y = _ssd_chunk(xbc_ref, rows, dt_ref[0, rows, :], state_ref, a_neg2, d_skip_ref[...])
        y = y * z_ref[0, rows, :].astype(F32)
        parts = []
        for g in range(SSD_GROUPS):
            yg = y[:, g * SSD_GROUP_WIDTH:(g + 1) * SSD_GROUP_WIDTH]
            parts.append(yg * lax.rsqrt(jnp.mean(yg * yg, axis=-1, keepdims=True) + 1e-6))
        yn = (jnp.concatenate(parts, axis=1) * norm_ref[...]).astype(BF16)
        out = _dot(yn, w_out_ref[...])
        o_ref[0, rows, :] = _layernorm(DN_ALPHA * x_ref[0, rows, :] + out, g_ref[...], b_ref[...])


def _ssd_call(z, xbc, dt, x, a_log, d_skip_x, norm_w, w_out, ln_g, ln_b, n_chunks):
    bsz, seqlen, _ = x.shape
    tm = n_chunks * SSD_CHUNK
    tok = lambda width: pl.BlockSpec((1, tm, width), lambda b, i: (b, i, 0))
    return pl.pallas_call(
        functools.partial(_ssd_kernel, n_chunks=n_chunks),
        grid=(bsz, seqlen // tm),
        in_specs=[tok(SSD_INNER), tok(SSD_CONV_DIM), tok(LANES), tok(D_MODEL),
                  _const_spec((1, LANES)), _const_spec((1, SSD_INNER)),
                  _const_spec((1, SSD_INNER)), _const_spec((SSD_INNER, D_MODEL)),
                  _const_spec((1, D_MODEL)), _const_spec((1, D_MODEL))],
        out_specs=tok(D_MODEL),
        out_shape=jax.ShapeDtypeStruct((bsz, seqlen, D_MODEL), F32),
        scratch_shapes=[pltpu.VMEM((SSD_STATE, SSD_INNER), F32)],
        compiler_params=_params(("parallel", "arbitrary")),
        name="ssd_scan_out",
    )(z, xbc, dt, x, a_log, d_skip_x, norm_w, w_out, ln_g, ln_b)


def _rot_cols(w):
    half = MLA_ROPE // 2
    return jnp.concatenate([-w[..., half:], w[..., :half]], axis=-1)


def _pad_heads(w_heads):
    r, h, c = w_heads.shape
    return jnp.pad(w_heads, ((0, 0), (0, 0), (0, HEAD_PAD - c))).reshape(r, h * HEAD_PAD)


def _block_diag(w):
    h, d, _ = w.shape
    eye = jnp.eye(h, dtype=w.dtype)
    return (eye[:, None, :, None] * w[:, :, None, :]).reshape(h * d, h * d)


def _layer0(x, pos_f, freq, w_in, conv_w, conv_b, gate_a_w, gate_a_b, gate_x_w, gate_x_b, lam,
            q_norm, kv_norm, w_uq, w_ukv, w_out, ln_g, ln_b, tm, ts, tq):
    o_kr = RNN_WIDTH + AB_WIDTH + MLA_Q_RANK + MLA_KV_RANK
    w_kr = w_in[:, o_kr:]
    w_in_ext = jnp.concatenate([w_in[:, :o_kr], jnp.zeros_like(w_in[:, :MLA_NOPE]), w_kr,
                                _rot_cols(w_kr)], axis=1).astype(BF16)

    uq = w_uq.reshape(MLA_Q_RANK, MLA_HEADS, MLA_NOPE + MLA_ROPE)
    wq = _pad_heads(jnp.concatenate([uq, _rot_cols(uq[..., MLA_NOPE:])], axis=-1)).astype(BF16)
    ukv = w_ukv.reshape(MLA_KV_RANK, MLA_HEADS, MLA_NOPE + MLA_V)
    wk = _pad_heads(ukv[..., :MLA_NOPE]).astype(BF16)
    wvt = jnp.pad(jnp.transpose(ukv[..., MLA_NOPE:], (1, 2, 0)),
                  ((0, 0), (0, MLA_VT_ROWS - MLA_V), (0, 0)))
    wvt = wvt.reshape(MLA_HEADS * MLA_VT_ROWS, MLA_KV_RANK).astype(BF16)
    vones = jnp.tile((jnp.arange(MLA_VT_ROWS) >= MLA_V).astype(F32), MLA_HEADS)[:, None]

    xr, gate, q, k, vt = _ab_in_call(x, pos_f, w_in_ext, freq, q_norm[None], kv_norm[None],
                                     wq, wk, wvt, vones, tm)

    w_gate = jnp.concatenate([_block_diag(gate_a_w), _block_diag(gate_x_w)], axis=1).astype(BF16)
    b_gate = jnp.concatenate([gate_a_b, gate_x_b])[None]
    y_mla = _mla_attn_call(q, k, vt, tq)
    return _rglru_call(xr, y_mla, gate, x, conv_w, conv_b[None], w_gate, b_gate, lam[None],
                       w_out.astype(BF16), ln_g[None], ln_b[None], ts)


def _layer1(x, w_in, conv_w, conv_b, dt_bias, a_log, d_skip, norm_w, w_out, ln_g, ln_b,
            tm, n_chunks):
    pad_h = lambda a: jnp.pad(a, ((0, 0), (0, LANES - SSD_HEADS)))
    w_zx = w_in[:, :SSD_INNER + SSD_CONV_DIM].astype(BF16)
    wdt = pad_h(w_in[:, SSD_INNER + SSD_CONV_DIM:]).astype(BF16)
    z, xbc, dt = _ssd_in_call(x, w_zx, wdt, pad_h(dt_bias[None]), conv_w, conv_b[None], tm)
    d_skip_x = jnp.repeat(d_skip, SSD_HEAD_DIM)[None]
    return _ssd_call(z, xbc, dt, x, pad_h(a_log[None]), d_skip_x, norm_w[None],
                     w_out.astype(BF16), ln_g[None], ln_b[None], n_chunks)


def kernel(x, positions, ab_w_in, ab_conv_w, ab_conv_b, ab_gate_a_w, ab_gate_a_b, ab_gate_x_w,
           ab_gate_x_b, ab_lambda, mla_q_norm, mla_kv_norm, mla_w_uq, mla_w_ukv, ab_w_out,
           ab_ln_g, ab_ln_b, ssd_w_in, ssd_conv_w, ssd_conv_b, ssd_dt_bias, ssd_a_log, ssd_d,
           ssd_norm, ssd_w_out, ssd_ln_g, ssd_ln_b):
    seqlen = x.shape[1]
    tm = min(1024, seqlen)
    tm_ssd = min(256, seqlen)
    ts = min(128, seqlen)
    tq = min(512, seqlen)
    n_chunks = min(8, seqlen // SSD_CHUNK)

    inv_freq = ROPE_THETA ** (-jnp.arange(0, MLA_ROPE, 2, dtype=F32) / MLA_ROPE)
    freq = jnp.concatenate([jnp.zeros((MLA_NOPE,), F32), inv_freq, inv_freq,
                            jnp.zeros((HEAD_PAD - MLA_NOPE - MLA_ROPE,), F32)])[None]
    pos_f = positions.astype(F32)[:, None, :]

    for layer in range(DEPTH):
        j = layer // 2
        if layer % 2 == 0:
            x = _layer0(x, pos_f, freq, ab_w_in[j], ab_conv_w[j], ab_conv_b[j], ab_gate_a_w[j],
                        ab_gate_a_b[j], ab_gate_x_w[j], ab_gate_x_b[j], ab_lambda[j],
                        mla_q_norm[j], mla_kv_norm[j], mla_w_uq[j], mla_w_ukv[j], ab_w_out[j],
                        ab_ln_g[j], ab_ln_b[j], tm, ts, tq)
        else:
            x = _layer1(x, ssd_w_in[j], ssd_conv_w[j], ssd_conv_b[j], ssd_dt_bias[j],
                        ssd_a_log[j], ssd_d[j], ssd_norm[j], ssd_w_out[j], ssd_ln_g[j],
                        ssd_ln_b[j], tm_ssd, n_chunks)
    return x
```

```python
import functools
import math

import jax
import jax.numpy as jnp
from jax import lax
from jax.experimental import pallas as pl
from jax.experimental.pallas import tpu as pltpu

D_MODEL = 1024
DEPTH = 2
DN_ALPHA = (2.0 * DEPTH) ** 0.25

RNN_WIDTH = 512
RNN_HEADS = 8
RNN_HEAD_DIM = RNN_WIDTH // RNN_HEADS
CONV_WIDTH = 4
RG_C = 8.0

MLA_HEADS = 8
MLA_NOPE = 64
MLA_ROPE = 32
MLA_V = 64
MLA_Q_RANK = 256
MLA_KV_RANK = 128
MLA_WIDTH = MLA_HEADS * MLA_V
ROPE_THETA = 10000.0
AB_WIDTH = RNN_WIDTH + MLA_WIDTH

SSD_INNER = 2048
SSD_HEAD_DIM = 64
SSD_HEADS = 32
SSD_GROUPS = 4
SSD_STATE = 128
SSD_CHUNK = 128
SSD_CONV_DIM = SSD_INNER + 2 * SSD_GROUPS * SSD_STATE
SSD_GROUP_WIDTH = SSD_INNER // SSD_GROUPS

LANES = 128
SUBLANES = 8
BF16_ROWS = 2 * SUBLANES
HEAD_PAD = LANES
MLA_VT_ROWS = MLA_V + BF16_ROWS
QK_PAD_WIDTH = MLA_HEADS * HEAD_PAD
VMEM_LIMIT = 56 * 1024 * 1024

BF16 = jnp.bfloat16
F32 = jnp.float32


NEG_LOG2E = -math.log2(math.e)


def _exp_neg(x):
    return jnp.exp2(x * NEG_LOG2E)


def _sigmoid(x):
    return 1.0 / (1.0 + _exp_neg(x))


def _silu(x):
    return x * _sigmoid(x)


def _log1p(y):
    u = 1.0 + y
    return jnp.where(u == 1.0, y, jnp.log(u) * (y / (u - 1.0)))


def _softplus(x):
    return jnp.maximum(x, 0.0) + _log1p(_exp_neg(jnp.abs(x)))


def _dot(a, b):
    return jnp.dot(a, b, preferred_element_type=F32)


def _const_spec(shape):
    zeros = (0,) * len(shape)
    return pl.BlockSpec(shape, lambda *_: zeros)


def _params(semantics, flags=None):
    return pltpu.CompilerParams(dimension_semantics=semantics,
                                vmem_limit_bytes=VMEM_LIMIT, flags=flags)


def _ab_in_kernel(x_ref, pos_ref, w_in_ref, freq_ref, qn_ref, kvn_ref, wq_ref,
                  wk_ref, wvt_ref, vones_ref,
                  xr_ref, gate_ref, q_ref, k_ref, vt_ref):
    xb = x_ref[0].astype(BF16)
    proj = _dot(xb, w_in_ref[...])
    o_gate = RNN_WIDTH
    o_cq = o_gate + AB_WIDTH
    o_ckv = o_cq + MLA_Q_RANK
    o_kr = o_ckv + MLA_KV_RANK

    xr_ref[0] = proj[:, :o_gate]
    gate_ref[0] = _silu(proj[:, o_gate:o_cq]).astype(BF16)

    pos_rows = jnp.broadcast_to(pos_ref[0], (HEAD_PAD, pos_ref.shape[2])).T
    ang = pos_rows * freq_ref[...]
    cos = jnp.cos(ang)
    sin = jnp.sin(ang)
    to_rope_lanes = HEAD_PAD - MLA_ROPE

    c_q = proj[:, o_cq:o_ckv]
    c_q = c_q * lax.rsqrt(jnp.mean(c_q * c_q, axis=-1, keepdims=True) + 1e-6) * qn_ref[...]
    q_all = _dot(c_q.astype(BF16), wq_ref[...])
    scale = (MLA_NOPE + MLA_ROPE) ** -0.5 * math.log2(math.e)
    for h in range(MLA_HEADS):
        q_h = q_all[:, h * HEAD_PAD:(h + 1) * HEAD_PAD]
        q_h = q_h * cos + pltpu.roll(q_h, to_rope_lanes, axis=1) * sin
        q_ref[0, :, h * HEAD_PAD:(h + 1) * HEAD_PAD] = (q_h * scale).astype(BF16)

    c_kv = proj[:, o_ckv:o_kr]
    c_kv = c_kv * lax.rsqrt(jnp.mean(c_kv * c_kv, axis=-1, keepdims=True) + 1e-6) * kvn_ref[...]
    c_kvb = c_kv.astype(BF16)
    kr = proj[:, o_kr:o_kr + HEAD_PAD]
    lane = lax.broadcasted_iota(jnp.int32, kr.shape, 1)
    k_rope = jnp.where(lane < MLA_NOPE + MLA_ROPE,
                       kr * cos + pltpu.roll(kr, to_rope_lanes, axis=1) * sin, 0.0)
    k = _dot(c_kvb, wk_ref[...]) + jnp.concatenate([k_rope] * MLA_HEADS, axis=1)
    k_ref[0] = k.astype(BF16)
    vt = lax.dot_general(wvt_ref[...], c_kvb, (((1,), (1,)), ((), ())),
                         preferred_element_type=F32) + vones_ref[...]
    vt_ref[0] = vt.reshape(MLA_HEADS, MLA_VT_ROWS, vt.shape[1]).astype(BF16)


def _ab_in_call(x, pos_f, w_in_ext, freq, q_norm, kv_norm, wq, wk, wvt, vones, tm):
    bsz, seqlen, _ = x.shape
    n_in = w_in_ext.shape[1]
    grid = (bsz, seqlen // tm)
    tok = lambda width: pl.BlockSpec((1, tm, width), lambda b, i: (b, i, 0))
    vt_rows = MLA_HEADS * MLA_VT_ROWS
    return pl.pallas_call(
        _ab_in_kernel,
        grid=grid,
        in_specs=[tok(D_MODEL), pl.BlockSpec((1, 1, tm), lambda b, i: (b, 0, i)),
                  _const_spec((D_MODEL, n_in)), _const_spec((1, HEAD_PAD)),
                  _const_spec((1, MLA_Q_RANK)), _const_spec((1, MLA_KV_RANK)),
                  _const_spec((MLA_Q_RANK, QK_PAD_WIDTH)),
                  _const_spec((MLA_KV_RANK, QK_PAD_WIDTH)), _const_spec((vt_rows, MLA_KV_RANK)),
                  _const_spec((vt_rows, 1))],
        out_specs=[tok(RNN_WIDTH), tok(AB_WIDTH), tok(QK_PAD_WIDTH), tok(QK_PAD_WIDTH),
                   pl.BlockSpec((1, MLA_HEADS, MLA_VT_ROWS, tm), lambda b, i: (b, 0, 0, i))],
        out_shape=[jax.ShapeDtypeStruct((bsz, seqlen, RNN_WIDTH), F32),
                   jax.ShapeDtypeStruct((bsz, seqlen, AB_WIDTH), BF16),
                   jax.ShapeDtypeStruct((bsz, seqlen, QK_PAD_WIDTH), BF16),
                   jax.ShapeDtypeStruct((bsz, seqlen, QK_PAD_WIDTH), BF16),
                   jax.ShapeDtypeStruct((bsz, MLA_HEADS, MLA_VT_ROWS, seqlen), BF16)],
        compiler_params=_params(("parallel", "parallel")),
        name="ab_in_proj",
    )(x, pos_f, w_in_ext, freq, q_norm, kv_norm, wq, wk, wvt, vones)


def _layernorm(z, g, b):
    mu = jnp.mean(z, axis=-1, keepdims=True)
    zc = z - mu
    var = jnp.mean(zc * zc, axis=-1, keepdims=True)
    return zc * lax.rsqrt(var + 1e-5) * g + b


def _rglru_kernel(xr_ref, ya_ref, gate_ref, x_ref, conv_w_ref, conv_b_ref, w_gate_ref,
                  b_gate_ref, lam_ref, w_out_ref, ln_g_ref, ln_b_ref,
                  o_ref, tail_ref, carry_ref, a_ref, u_ref):
    bsz, ts, width = xr_ref.shape

    @pl.when(pl.program_id(0) == 0)
    def _():
        tail_ref[...] = jnp.zeros_like(tail_ref)
        carry_ref[...] = jnp.zeros_like(carry_ref)

    x_blk = jnp.swapaxes(xr_ref[...], 0, 1)
    x_ext = jnp.concatenate([tail_ref[...], x_blk], axis=0)
    tail_ref[...] = x_blk[ts - (CONV_WIDTH - 1):]
    xc = conv_b_ref[...][None]
    for k in range(CONV_WIDTH):
        xc = xc + x_ext[k:k + ts] * conv_w_ref[k:k + 1, :][None]

    xc2 = xc.reshape(ts * bsz, width)
    gates = _dot(xc2.astype(BF16), w_gate_ref[...]) + b_gate_ref[...]
    r = _sigmoid(gates[:, :width])
    i = _sigmoid(gates[:, width:])
    neg_c_softplus = -RG_C * _softplus(-lam_ref[...])
    log_a = r * neg_c_softplus
    a = jnp.exp2(r * (neg_c_softplus * -NEG_LOG2E))
    mult = jnp.sqrt(-jnp.tanh(log_a) * (a * a + 1.0))
    u = mult * (i * xc2)
    a_ref[...] = a.reshape(ts, bsz, width)
    u_ref[...] = u.reshape(ts, bsz, width)

    def step(t, h):
        h = a_ref[t] * h + u_ref[t]
        u_ref[t] = h
        return h

    carry_ref[...] = lax.fori_loop(0, ts, step, carry_ref[...], unroll=8)

    rows = bsz * ts
    h = jnp.swapaxes(u_ref[...], 0, 1).reshape(rows, width)
    gate = gate_ref[...].reshape(rows, AB_WIDTH).astype(F32)
    y_rnn = (h * gate[:, :RNN_WIDTH]).astype(BF16)
    y_mla = (ya_ref[...].reshape(rows, MLA_WIDTH).astype(F32) * gate[:, RNN_WIDTH:]).astype(BF16)
    y = _dot(y_rnn, w_out_ref[:RNN_WIDTH, :]) + _dot(y_mla, w_out_ref[RNN_WIDTH:, :])
    out = _layernorm(DN_ALPHA * x_ref[...].reshape(rows, D_MODEL) + y,
                     ln_g_ref[...], ln_b_ref[...])
    o_ref[...] = out.reshape(bsz, ts, D_MODEL)


def _rglru_call(xr, y_mla, gate, x, conv_w, conv_b, w_gate, b_gate, lam, w_out, ln_g, ln_b, ts):
    bsz, seqlen, width = xr.shape
    blk = lambda w: pl.BlockSpec((bsz, ts, w), lambda i: (0, i, 0))
    return pl.pallas_call(
        _rglru_kernel,
        grid=(seqlen // ts,),
        in_specs=[blk(width), blk(MLA_WIDTH), blk(AB_WIDTH), blk(D_MODEL),
                  _const_spec((CONV_WIDTH, width)), _const_spec((1, width)),
                  _const_spec((width, 2 * width)), _const_spec((1, 2 * width)),
                  _const_spec((1, width)), _const_spec((AB_WIDTH, D_MODEL)),
                  _const_spec((1, D_MODEL)), _const_spec((1, D_MODEL))],
        out_specs=blk(D_MODEL),
        out_shape=jax.ShapeDtypeStruct((bsz, seqlen, D_MODEL), F32),
        scratch_shapes=[pltpu.VMEM((CONV_WIDTH - 1, bsz, width), F32),
                        pltpu.VMEM((bsz, width), F32),
                        pltpu.VMEM((ts, bsz, width), F32),
                        pltpu.VMEM((ts, bsz, width), F32)],
        compiler_params=_params(("arbitrary",)),
        name="rglru_out_proj",
    )(xr, y_mla, gate, x, conv_w, conv_b, w_gate, b_gate, lam, w_out, ln_g, ln_b)


ATTN_ROWS = 256


def _mla_attn_tiles(q_ref, k_ref, vt_ref, o_ref, s_refs, tiles, tq):
    heads = q_ref.shape[2] // HEAD_PAD
    halves = tq // ATTN_ROWS
    units = [(h, r) for h in range(heads) for r in range(halves)]
    steps = [(qi, j) for qi in tiles for j in range(qi + 1)]

    def depth(qi, j, r):
        return (r + 1) * ATTN_ROWS if j == qi else tq

    def scores_into(dst_ref, qi, j):
        for u, (h, r) in enumerate(units):
            w = depth(qi, j, r)
            q_u = q_ref[0, qi * tq + r * ATTN_ROWS:qi * tq + (r + 1) * ATTN_ROWS,
                        h * HEAD_PAD:(h + 1) * HEAD_PAD]
            k_blk = k_ref[0, j * tq:j * tq + w, h * HEAD_PAD:(h + 1) * HEAD_PAD]
            dst_ref[u, :w, :] = lax.dot_general(k_blk, q_u, (((1,), (1,)), ((), ())),
                                                preferred_element_type=F32)

    scores_into(s_refs[0], *steps[0])
    ms, accs = None, None
    for i, (qi, j) in enumerate(steps):
        src_ref = s_refs[i % 2]
        if i + 1 < len(steps):
            scores_into(s_refs[(i + 1) % 2], *steps[i + 1])
        if j == 0:
            ms = [jnp.full((1, ATTN_ROWS), -1e30, F32) for _ in units]
            accs = [jnp.zeros((MLA_VT_ROWS, ATTN_ROWS), F32) for _ in units]
        for u, (h, r) in enumerate(units):
            w = depth(qi, j, r)
            if j == qi:
                key = lax.broadcasted_iota(jnp.int32, (ATTN_ROWS, ATTN_ROWS), 0)
                qry = lax.broadcasted_iota(jnp.int32, (ATTN_ROWS, ATTN_ROWS), 1)
                edge = slice(w - ATTN_ROWS, w)
                src_ref[u, edge, :] = jnp.where(key <= qry, src_ref[u, edge, :], -1e30)
            vt_blk = vt_ref[0, h, :, j * tq:j * tq + w]
            m_new = jnp.maximum(ms[u], jnp.max(src_ref[u, :w, :], axis=0, keepdims=True))
            p = jnp.exp2((src_ref[u, :w, :] - m_new).astype(BF16))
            accs[u] = jnp.exp2(ms[u] - m_new) * accs[u] + _dot(vt_blk, p)
            ms[u] = m_new
        if j == qi:
            outs = [acc[:MLA_V] / acc[MLA_V:MLA_V + 1] for acc in accs]
            for pair in range(heads // 2):
                for r in range(halves):
                    o_t = jnp.concatenate([outs[2 * pair * halves + r],
                                           outs[(2 * pair + 1) * halves + r]], axis=0)
                    o_ref[0, qi * tq + r * ATTN_ROWS:qi * tq + (r + 1) * ATTN_ROWS,
                          pair * HEAD_PAD:(pair + 1) * HEAD_PAD] = o_t.T.astype(o_ref.dtype)


def _mla_attn_kernel(q_ref, k_ref, vt_ref, o_ref, s0_ref, s1_ref, *, tq, groups):
    g = pl.program_id(2)
    for idx, tiles in enumerate(groups):
        @pl.when(g == idx)
        def _(tiles=tiles):
            _mla_attn_tiles(q_ref, k_ref, vt_ref, o_ref, (s0_ref, s1_ref), tiles, tq)


def _mla_attn_call(q, k, vt, tq, heads_per_step=2):
    bsz, seqlen, _ = q.shape
    width = heads_per_step * HEAD_PAD
    n_hsteps = MLA_HEADS // heads_per_step
    out_w = heads_per_step * MLA_V
    n_units = heads_per_step * (tq // ATTN_ROWS)
    n_q = seqlen // tq
    groups = tuple(tuple(sorted({i, n_q - 1 - i})) for i in range((n_q + 1) // 2))
    resident = lambda w: pl.BlockSpec((1, seqlen, w), lambda b, h, g: (b, 0, h))
    return pl.pallas_call(
        functools.partial(_mla_attn_kernel, tq=tq, groups=groups),
        grid=(bsz, n_hsteps, len(groups)),
        in_specs=[resident(width), resident(width),
                  pl.BlockSpec((1, heads_per_step, MLA_VT_ROWS, seqlen),
                               lambda b, h, g: (b, h, 0, 0))],
        out_specs=resident(out_w),
        out_shape=jax.ShapeDtypeStruct((bsz, seqlen, MLA_WIDTH), BF16),
        scratch_shapes=[pltpu.VMEM((n_units, tq, ATTN_ROWS), F32),
                        pltpu.VMEM((n_units, tq, ATTN_ROWS), F32)],
        compiler_params=_params(("parallel", "parallel", "arbitrary")),
        name="mla_attention",
    )(q, k, vt)


SSD_IN_COLS = 512


def _conv_silu_store(proj, tail, w, b, out_ref, cols):
    n_rows = proj.shape[0]
    taps = [w[k:k + 1, :] for k in range(CONV_WIDTH)]
    first = lax.broadcasted_iota(jnp.int32, tail.shape, 0) == 0
    prev = None
    done = []
    for i in range(-1, n_rows // SUBLANES):
        x = tail if i < 0 else proj[i * SUBLANES:(i + 1) * SUBLANES]
        acc = x * taps[0]
        rolled = []
        for k in range(1, CONV_WIDTH):
            r = pltpu.roll(acc, 1, axis=0)
            rolled.append(r)
            acc = (r if prev is None else jnp.where(first, prev[k - 1], r)) + x * taps[k]
        prev = rolled
        if i >= 0:
            done.append(acc + b)
        if len(done) == BF16_ROWS // SUBLANES:
            top = (i + 1) * SUBLANES
            out_ref[0, top - BF16_ROWS:top, cols] = _silu(
                jnp.concatenate(done, axis=0)).astype(BF16)
            done = []


SSD_PROJ_WIDTH = SSD_INNER + SSD_CONV_DIM + LANES


def _ssd_in_stage(step, x_ref, w_ref, dt_bias_ref, conv_w_ref, conv_b_ref,
                  z_ref, xbc_ref, dt_ref, new_ref, old_ref, project=True, finish=True):
    tm = x_ref.shape[1]
    x_off = SSD_INNER
    if finish:
        tails = []
        for c in range(SSD_CONV_DIM // SSD_IN_COLS):
            cols = slice(x_off + c * SSD_IN_COLS, x_off + (c + 1) * SSD_IN_COLS)
            tails.append(jnp.where(step == 1, 0.0, new_ref[tm - SUBLANES:, cols]))

    if project:
        xb = x_ref[0].astype(BF16)
        for c in range(0, SSD_PROJ_WIDTH, SSD_IN_COLS):
            cols = slice(c, min(c + SSD_IN_COLS, SSD_PROJ_WIDTH))
            new_ref[:, cols] = _dot(xb, w_ref[:, cols])
    if not finish:
        return

    for c in range(SSD_INNER // SSD_IN_COLS):
        cols = slice(c * SSD_IN_COLS, (c + 1) * SSD_IN_COLS)
        z_ref[0, :, cols] = _silu(old_ref[:, cols]).astype(BF16)
    for c in range(SSD_CONV_DIM // SSD_IN_COLS):
        cols = slice(c * SSD_IN_COLS, (c + 1) * SSD_IN_COLS)
        p_cols = slice(x_off + c * SSD_IN_COLS, x_off + (c + 1) * SSD_IN_COLS)
        _conv_silu_store(old_ref[:, p_cols], tails[c], conv_w_ref[:, cols], conv_b_ref[:, cols],
                         xbc_ref, cols)
    dt_ref[0] = _softplus(old_ref[:, SSD_INNER + SSD_CONV_DIM:] + dt_bias_ref[...])


def _ssd_in_kernel(x_ref, w_ref, dt_bias_ref, conv_w_ref, conv_b_ref,
                   z_ref, xbc_ref, dt_ref, p0_ref, p1_ref):
    step = pl.program_id(1)
    last = pl.num_programs(1) - 1
    args = (step, x_ref, w_ref, dt_bias_ref, conv_w_ref, conv_b_ref, z_ref, xbc_ref, dt_ref)
    bufs = (p0_ref, p1_ref)

    @pl.when(step == 0)
    def _():
        p1_ref[p1_ref.shape[0] - SUBLANES:, :] = jnp.zeros((SUBLANES, p1_ref.shape[1]), F32)
        _ssd_in_stage(*args, p0_ref, p1_ref, finish=False)

    for parity in range(2):
        @pl.when((step > 0) & (step < last) & (step % 2 == parity))
        def _(parity=parity):
            _ssd_in_stage(*args, bufs[parity], bufs[1 - parity])

        @pl.when((step == last) & (step % 2 == parity))
        def _(parity=parity):
            _ssd_in_stage(*args, bufs[parity], bufs[1 - parity], project=False)


def _ssd_in_call(x, w_all, dt_bias, conv_w, conv_b, tm):
    bsz, seqlen, _ = x.shape
    n_tiles = seqlen // tm
    tile_in = pl.BlockSpec((1, tm, D_MODEL), lambda b, i: (b, jnp.minimum(i, n_tiles - 1), 0))
    tile_out = lambda width: pl.BlockSpec((1, tm, width),
                                          lambda b, i: (b, jnp.maximum(i - 1, 0), 0))
    return pl.pallas_call(
        _ssd_in_kernel,
        grid=(bsz, n_tiles + 1),
        in_specs=[tile_in, _const_spec((D_MODEL, SSD_PROJ_WIDTH)),
                  _const_spec((1, LANES)), _const_spec((CONV_WIDTH, SSD_CONV_DIM)),
                  _const_spec((1, SSD_CONV_DIM))],
        out_specs=[tile_out(SSD_INNER), tile_out(SSD_CONV_DIM), tile_out(LANES)],
        out_shape=[jax.ShapeDtypeStruct((bsz, seqlen, SSD_INNER), BF16),
                   jax.ShapeDtypeStruct((bsz, seqlen, SSD_CONV_DIM), BF16),
                   jax.ShapeDtypeStruct((bsz, seqlen, LANES), F32)],
        scratch_shapes=[pltpu.VMEM((tm, SSD_PROJ_WIDTH), F32),
                        pltpu.VMEM((tm, SSD_PROJ_WIDTH), F32)],
        compiler_params=_params(("parallel", "arbitrary")),
        name="ssd_in_proj",
    )(x, w_all, dt_bias, conv_w, conv_b)


def _cumsum_rows(x):
    n = x.shape[0]
    row = lax.broadcasted_iota(jnp.int32, x.shape, 0)
    shift = 1
    while shift < n:
        x = x + jnp.where(row >= shift, pltpu.roll(x, shift, axis=0), 0.0)
        shift *= 2
    return x


LOG2E = math.log2(math.e)


def _ssd_chunk(xbc_ref, rows, dt, state_ref, a_neg2, d_skip_x):
    L = SSD_CHUNK
    gw = SSD_GROUPS * SSD_STATE

    cs = _cumsum_rows(dt * a_neg2)
    cs_t = cs.T
    dt_t = dt.T
    w_t = dt_t * jnp.exp2(cs_t[:, L - 1:L] - cs_t)

    row = lax.broadcasted_iota(jnp.int32, (L, L), 0)
    col = lax.broadcasted_iota(jnp.int32, (L, L), 1)
    tril = col <= row
    lane = lax.broadcasted_iota(jnp.int32, (L, LANES), 1)
    low = lane < SSD_HEAD_DIM
    keep_lo = low.astype(F32).astype(BF16)
    keep_hi = 1.0 - keep_lo

    hpg = SSD_HEADS // SSD_GROUPS
    y_parts = []
    for g in range(SSD_GROUPS):
        b_g = xbc_ref[0, rows, SSD_INNER + g * SSD_STATE:SSD_INNER + (g + 1) * SSD_STATE]
        c_g = xbc_ref[0, rows, SSD_INNER + gw + g * SSD_STATE:SSD_INNER + gw + (g + 1) * SSD_STATE]
        cb = lax.dot_general(c_g, b_g, (((1,), (1,)), ((), ())),
                             preferred_element_type=F32)
        cb = jnp.where(tril, cb, 0.0)
        b_t = b_g.astype(F32).T
        glanes = slice(g * SSD_GROUP_WIDTH, (g + 1) * SSD_GROUP_WIDTH)
        y_off = _dot(c_g, state_ref[:, glanes].astype(BF16))
        for pair in range(hpg // 2):
            h0 = g * hpg + 2 * pair
            plane = slice(h0 * SSD_HEAD_DIM, (h0 + 2) * SSD_HEAD_DIM)
            tops, bots, decays = [], [], []
            for h in (h0, h0 + 1):
                cs_col = jnp.broadcast_to(cs[:, h:h + 1], (L, LANES))
                seg = cs_col - cs_t[h:h + 1, :]
                m_h = jnp.exp2(jnp.minimum(seg, 0.0)) * (cb * dt_t[h:h + 1, :])
                tops.append(m_h.astype(BF16))
                bots.append((b_t * w_t[h:h + 1, :]).astype(BF16))
                decays.append(jnp.exp2(cs_col))
            lhs = jnp.concatenate([jnp.concatenate(tops, axis=1),
                                   jnp.concatenate(bots, axis=1)], axis=0)
            x_pair = xbc_ref[0, rows, plane]
            rhs = jnp.concatenate([x_pair * keep_lo, x_pair * keep_hi], axis=0)
            res = _dot(lhs, rhs)
            dec = jnp.where(low, decays[0], decays[1])
            off = pair * 2 * SSD_HEAD_DIM
            y_pair = (res[:L] + dec * y_off[:, off:off + 2 * SSD_HEAD_DIM]
                      + d_skip_x[:, plane] * x_pair.astype(F32))
            y_parts.append(y_pair)
            state_ref[:, plane] = state_ref[:, plane] * dec[L - 1:L, :] + res[L:]
    return jnp.concatenate(y_parts, axis=1)


def _ssd_kernel(z_ref, xbc_ref, dt_ref, x_ref, a_log_ref, d_skip_ref, norm_ref, w_out_ref,
                g_ref, b_ref, o_ref, state_ref, *, n_chunks):
    L = SSD_CHUNK

    @pl.when(pl.program_id(1) == 0)
    def _():
        state_ref[...] = jnp.zeros_like(state_ref)

    a_neg2 = -jnp.exp(a_log_ref[...]) * LOG2E
    for c in range(n_chunks):
        rows = slice(c * L, (c + 1) * L)
        y = _ssd_chunk(xbc_ref, rows, dt_ref[0, rows, :], state_ref, a_neg2, d_skip_ref[...])
        y = y * z_ref[0, rows, :].astype(F32)
        parts = []
        for g in range(SSD_GROUPS):
            yg = y[:, g * SSD_GROUP_WIDTH:(g + 1) * SSD_GROUP_WIDTH]
            parts.append(yg * lax.rsqrt(jnp.mean(yg * yg, axis=-1, keepdims=True) + 1e-6))
        yn = (jnp.concatenate(parts, axis=1) * norm_ref[...]).astype(BF16)
        out = _dot(yn, w_out_ref[...])
        o_ref[0, rows, :] = _layernorm(DN_ALPHA * x_ref[0, rows, :] + out, g_ref[...], b_ref[...])


def _ssd_call(z, xbc, dt, x, a_log, d_skip_x, norm_w, w_out, ln_g, ln_b, n_chunks):
    bsz, seqlen, _ = x.shape
    tm = n_chunks * SSD_CHUNK
    tok = lambda width: pl.BlockSpec((1, tm, width), lambda b, i: (b, i, 0))
    return pl.pallas_call(
        functools.partial(_ssd_kernel, n_chunks=n_chunks),
        grid=(bsz, seqlen // tm),
        in_specs=[tok(SSD_INNER), tok(SSD_CONV_DIM), tok(LANES), tok(D_MODEL),
                  _const_spec((1, LANES)), _const_spec((1, SSD_INNER)),
                  _const_spec((1, SSD_INNER)), _const_spec((SSD_INNER, D_MODEL)),
                  _const_spec((1, D_MODEL)), _const_spec((1, D_MODEL))],
        out_specs=tok(D_MODEL),
        out_shape=jax.ShapeDtypeStruct((bsz, seqlen, D_MODEL), F32),
        scratch_shapes=[pltpu.VMEM((SSD_STATE, SSD_INNER), F32)],
        compiler_params=_params(("parallel", "arbitrary")),
        name="ssd_scan_out",
    )(z, xbc, dt, x, a_log, d_skip_x, norm_w, w_out, ln_g, ln_b)


def _rot_cols(w):
    half = MLA_ROPE // 2
    return jnp.concatenate([-w[..., half:], w[..., :half]], axis=-1)


def _pad_heads(w_heads):
    r, h, c = w_heads.shape
    return jnp.pad(w_heads, ((0, 0), (0, 0), (0, HEAD_PAD - c))).reshape(r, h * HEAD_PAD)


def _block_diag(w):
    h, d, _ = w.shape
    eye = jnp.eye(h, dtype=w.dtype)
    return (eye[:, None, :, None] * w[:, :, None, :]).reshape(h * d, h * d)


def _layer0(x, pos_f, freq, w_in, conv_w, conv_b, gate_a_w, gate_a_b, gate_x_w, gate_x_b, lam,
            q_norm, kv_norm, w_uq, w_ukv, w_out, ln_g, ln_b, tm, ts, tq):
    o_kr = RNN_WIDTH + AB_WIDTH + MLA_Q_RANK + MLA_KV_RANK
    w_kr = w_in[:, o_kr:]
    w_in_ext = jnp.concatenate([w_in[:, :o_kr], jnp.zeros_like(w_in[:, :MLA_NOPE]), w_kr,
                                _rot_cols(w_kr)], axis=1).astype(BF16)

    uq = w_uq.reshape(MLA_Q_RANK, MLA_HEADS, MLA_NOPE + MLA_ROPE)
    wq = _pad_heads(jnp.concatenate([uq, _rot_cols(uq[..., MLA_NOPE:])], axis=-1)).astype(BF16)
    ukv = w_ukv.reshape(MLA_KV_RANK, MLA_HEADS, MLA_NOPE + MLA_V)
    wk = _pad_heads(ukv[..., :MLA_NOPE]).astype(BF16)
    wvt = jnp.pad(jnp.transpose(ukv[..., MLA_NOPE:], (1, 2, 0)),
                  ((0, 0), (0, MLA_VT_ROWS - MLA_V), (0, 0)))
    wvt = wvt.reshape(MLA_HEADS * MLA_VT_ROWS, MLA_KV_RANK).astype(BF16)
    vones = jnp.tile((jnp.arange(MLA_VT_ROWS) >= MLA_V).astype(F32), MLA_HEADS)[:, None]

    xr, gate, q, k, vt = _ab_in_call(x, pos_f, w_in_ext, freq, q_norm[None], kv_norm[None],
                                     wq, wk, wvt, vones, tm)

    w_gate = jnp.concatenate([_block_diag(gate_a_w), _block_diag(gate_x_w)], axis=1).astype(BF16)
    b_gate = jnp.concatenate([gate_a_b, gate_x_b])[None]
    y_mla = _mla_attn_call(q, k, vt, tq)
    return _rglru_call(xr, y_mla, gate, x, conv_w, conv_b[None], w_gate, b_gate, lam[None],
                       w_out.astype(BF16), ln_g[None], ln_b[None], ts)


def _layer1(x, w_in, conv_w, conv_b, dt_bias, a_log, d_skip, norm_w, w_out, ln_g, ln_b,
            tm, n_chunks):
    pad_h = lambda a: jnp.pad(a, ((0, 0), (0, LANES - SSD_HEADS)))
    w_all = pad_h(w_in).astype(BF16)
    z, xbc, dt = _ssd_in_call(x, w_all, pad_h(dt_bias[None]), conv_w, conv_b[None], tm)
    d_skip_x = jnp.repeat(d_skip, SSD_HEAD_DIM)[None]
    return _ssd_call(z, xbc, dt, x, pad_h(a_log[None]), d_skip_x, norm_w[None],
                     w_out.astype(BF16), ln_g[None], ln_b[None], n_chunks)


def kernel(x, positions, ab_w_in, ab_conv_w, ab_conv_b, ab_gate_a_w, ab_gate_a_b, ab_gate_x_w,
           ab_gate_x_b, ab_lambda, mla_q_norm, mla_kv_norm, mla_w_uq, mla_w_ukv, ab_w_out,
           ab_ln_g, ab_ln_b, ssd_w_in, ssd_conv_w, ssd_conv_b, ssd_dt_bias, ssd_a_log, ssd_d,
           ssd_norm, ssd_w_out, ssd_ln_g, ssd_ln_b):
    seqlen = x.shape[1]
    tm = min(1024, seqlen)
    tm_ssd = min(256, seqlen)
    ts = min(128, seqlen)
    tq = min(512, seqlen)
    n_chunks = min(4, seqlen // SSD_CHUNK)

    inv_freq = ROPE_THETA ** (-jnp.arange(0, MLA_ROPE, 2, dtype=F32) / MLA_ROPE)
    freq = jnp.concatenate([jnp.zeros((MLA_NOPE,), F32), inv_freq, inv_freq,
                            jnp.zeros((HEAD_PAD - MLA_NOPE - MLA_ROPE,), F32)])[None]
    pos_f = positions.astype(F32)[:, None, :]

    for layer in range(DEPTH):
        j = layer // 2
        if layer % 2 == 0:
            x = _layer0(x, pos_f, freq, ab_w_in[j], ab_conv_w[j], ab_conv_b[j], ab_gate_a_w[j],
                        ab_gate_a_b[j], ab_gate_x_w[j], ab_gate_x_b[j], ab_lambda[j],
                        mla_q_norm[j], mla_kv_norm[j], mla_w_uq[j], mla_w_ukv[j], ab_w_out[j],
                        ab_ln_g[j], ab_ln_b[j], tm, ts, tq)
        else:
            x = _layer1(x, ssd_w_in[j], ssd_conv_w[j], ssd_conv_b[j], ssd_dt_bias[j],
                        ssd_a_log[j], ssd_d[j], ssd_norm[j], ssd_w_out[j], ssd_ln_g[j],
                        ssd_ln_b[j], tm_ssd, n_chunks)
    return x
```

```python
import functools
import math

import jax
import jax.numpy as jnp
from jax import lax
from jax.experimental import pallas as pl
from jax.experimental.pallas import tpu as pltpu

D_MODEL = 1024
DEPTH = 2
DN_ALPHA = (2.0 * DEPTH) ** 0.25

RNN_WIDTH = 512
RNN_HEADS = 8
RNN_HEAD_DIM = RNN_WIDTH // RNN_HEADS
CONV_WIDTH = 4
RG_C = 8.0

MLA_HEADS = 8
MLA_NOPE = 64
MLA_ROPE = 32
MLA_V = 64
MLA_Q_RANK = 256
MLA_KV_RANK = 128
MLA_WIDTH = MLA_HEADS * MLA_V
ROPE_THETA = 10000.0
AB_WIDTH = RNN_WIDTH + MLA_WIDTH

SSD_INNER = 2048
SSD_HEAD_DIM = 64
SSD_HEADS = 32
SSD_GROUPS = 4
SSD_STATE = 128
SSD_CHUNK = 128
SSD_CONV_DIM = SSD_INNER + 2 * SSD_GROUPS * SSD_STATE
SSD_GROUP_WIDTH = SSD_INNER // SSD_GROUPS

LANES = 128
SUBLANES = 8
BF16_ROWS = 2 * SUBLANES
HEAD_PAD = LANES
MLA_VT_ROWS = MLA_V + BF16_ROWS
QK_PAD_WIDTH = MLA_HEADS * HEAD_PAD
VMEM_LIMIT = 56 * 1024 * 1024

BF16 = jnp.bfloat16
F32 = jnp.float32


NEG_LOG2E = -math.log2(math.e)


def _exp_neg(x):
    return jnp.exp2(x * NEG_LOG2E)


def _sigmoid(x):
    return 1.0 / (1.0 + _exp_neg(x))


def _silu(x):
    return x * _sigmoid(x)


def _log1p(y):
    u = 1.0 + y
    return jnp.where(u == 1.0, y, jnp.log(u) * (y / (u - 1.0)))


def _softplus(x):
    return jnp.maximum(x, 0.0) + _log1p(_exp_neg(jnp.abs(x)))


def _dot(a, b):
    return jnp.dot(a, b, preferred_element_type=F32)


def _const_spec(shape):
    zeros = (0,) * len(shape)
    return pl.BlockSpec(shape, lambda *_: zeros)


def _params(semantics):
    return pltpu.CompilerParams(dimension_semantics=semantics,
                                vmem_limit_bytes=VMEM_LIMIT)


def _ab_in_kernel(x_ref, pos_ref, w_in_ref, freq_ref, qn_ref, kvn_ref, wq_ref,
                  wk_ref, wvt_ref, vones_ref,
                  xr_ref, gate_ref, q_ref, k_ref, vt_ref):
    xb = x_ref[0].astype(BF16)
    proj = _dot(xb, w_in_ref[...])
    o_gate = RNN_WIDTH
    o_cq = o_gate + AB_WIDTH
    o_ckv = o_cq + MLA_Q_RANK
    o_kr = o_ckv + MLA_KV_RANK

    xr_ref[0] = proj[:, :o_gate]
    gate_ref[0] = _silu(proj[:, o_gate:o_cq]).astype(BF16)

    pos_rows = jnp.broadcast_to(pos_ref[0], (HEAD_PAD, pos_ref.shape[2])).T
    ang = pos_rows * freq_ref[...]
    cos = jnp.cos(ang)
    sin = jnp.sin(ang)
    to_rope_lanes = HEAD_PAD - MLA_ROPE

    c_q = proj[:, o_cq:o_ckv]
    c_q = c_q * lax.rsqrt(jnp.mean(c_q * c_q, axis=-1, keepdims=True) + 1e-6) * qn_ref[...]
    q_all = _dot(c_q.astype(BF16), wq_ref[...])
    scale = (MLA_NOPE + MLA_ROPE) ** -0.5 * math.log2(math.e)
    for h in range(MLA_HEADS):
        q_h = q_all[:, h * HEAD_PAD:(h + 1) * HEAD_PAD]
        q_h = q_h * cos + pltpu.roll(q_h, to_rope_lanes, axis=1) * sin
        q_ref[0, :, h * HEAD_PAD:(h + 1) * HEAD_PAD] = (q_h * scale).astype(BF16)

    c_kv = proj[:, o_ckv:o_kr]
    c_kv = c_kv * lax.rsqrt(jnp.mean(c_kv * c_kv, axis=-1, keepdims=True) + 1e-6) * kvn_ref[...]
    c_kvb = c_kv.astype(BF16)
    kr = proj[:, o_kr:o_kr + HEAD_PAD]
    lane = lax.broadcasted_iota(jnp.int32, kr.shape, 1)
    k_rope = jnp.where(lane < MLA_NOPE + MLA_ROPE,
                       kr * cos + pltpu.roll(kr, to_rope_lanes, axis=1) * sin, 0.0)
    k = _dot(c_kvb, wk_ref[...]) + jnp.concatenate([k_rope] * MLA_HEADS, axis=1)
    k_ref[0] = k.astype(BF16)
    vt = lax.dot_general(wvt_ref[...], c_kvb, (((1,), (1,)), ((), ())),
                         preferred_element_type=F32) + vones_ref[...]
    vt_ref[0] = vt.reshape(MLA_HEADS, MLA_VT_ROWS, vt.shape[1]).astype(BF16)


def _ab_in_call(x, pos_f, w_in_ext, freq, q_norm, kv_norm, wq, wk, wvt, vones, tm):
    bsz, seqlen, _ = x.shape
    n_in = w_in_ext.shape[1]
    grid = (bsz, seqlen // tm)
    tok = lambda width: pl.BlockSpec((1, tm, width), lambda b, i: (b, i, 0))
    vt_rows = MLA_HEADS * MLA_VT_ROWS
    return pl.pallas_call(
        _ab_in_kernel,
        grid=grid,
        in_specs=[tok(D_MODEL), pl.BlockSpec((1, 1, tm), lambda b, i: (b, 0, i)),
                  _const_spec((D_MODEL, n_in)), _const_spec((1, HEAD_PAD)),
                  _const_spec((1, MLA_Q_RANK)), _const_spec((1, MLA_KV_RANK)),
                  _const_spec((MLA_Q_RANK, QK_PAD_WIDTH)),
                  _const_spec((MLA_KV_RANK, QK_PAD_WIDTH)), _const_spec((vt_rows, MLA_KV_RANK)),
                  _const_spec((vt_rows, 1))],
        out_specs=[tok(RNN_WIDTH), tok(AB_WIDTH), tok(QK_PAD_WIDTH), tok(QK_PAD_WIDTH),
                   pl.BlockSpec((1, MLA_HEADS, MLA_VT_ROWS, tm), lambda b, i: (b, 0, 0, i))],
        out_shape=[jax.ShapeDtypeStruct((bsz, seqlen, RNN_WIDTH), F32),
                   jax.ShapeDtypeStruct((bsz, seqlen, AB_WIDTH), BF16),
                   jax.ShapeDtypeStruct((bsz, seqlen, QK_PAD_WIDTH), BF16),
                   jax.ShapeDtypeStruct((bsz, seqlen, QK_PAD_WIDTH), BF16),
                   jax.ShapeDtypeStruct((bsz, MLA_HEADS, MLA_VT_ROWS, seqlen), BF16)],
        compiler_params=_params(("parallel", "parallel")),
        name="ab_in_proj",
    )(x, pos_f, w_in_ext, freq, q_norm, kv_norm, wq, wk, wvt, vones)


def _layernorm(z, g, b):
    mu = jnp.mean(z, axis=-1, keepdims=True)
    zc = z - mu
    var = jnp.mean(zc * zc, axis=-1, keepdims=True)
    return zc * lax.rsqrt(var + 1e-5) * g + b


def _rglru_kernel(xr_ref, ya_ref, gate_ref, x_ref, conv_w_ref, conv_b_ref, w_gate_ref,
                  b_gate_ref, lam_ref, w_out_ref, ln_g_ref, ln_b_ref,
                  o_ref, tail_ref, carry_ref, a_ref, u_ref):
    bsz, ts, width = xr_ref.shape

    @pl.when(pl.program_id(0) == 0)
    def _():
        tail_ref[...] = jnp.zeros_like(tail_ref)
        carry_ref[...] = jnp.zeros_like(carry_ref)

    x_blk = jnp.swapaxes(xr_ref[...], 0, 1)
    x_ext = jnp.concatenate([tail_ref[...], x_blk], axis=0)
    tail_ref[...] = x_blk[ts - (CONV_WIDTH - 1):]
    xc = conv_b_ref[...][None]
    for k in range(CONV_WIDTH):
        xc = xc + x_ext[k:k + ts] * conv_w_ref[k:k + 1, :][None]

    xc2 = xc.reshape(ts * bsz, width)
    gates = _dot(xc2.astype(BF16), w_gate_ref[...]) + b_gate_ref[...]
    r = _sigmoid(gates[:, :width])
    i = _sigmoid(gates[:, width:])
    neg_c_softplus = -RG_C * _softplus(-lam_ref[...])
    log_a = r * neg_c_softplus
    a = jnp.exp2(r * (neg_c_softplus * -NEG_LOG2E))
    mult = jnp.sqrt(-jnp.tanh(log_a) * (a * a + 1.0))
    u = mult * (i * xc2)
    a_ref[...] = a.reshape(ts, bsz, width)
    u_ref[...] = u.reshape(ts, bsz, width)

    def step(t, h):
        h = a_ref[t] * h + u_ref[t]
        u_ref[t] = h
        return h

    carry_ref[...] = lax.fori_loop(0, ts, step, carry_ref[...], unroll=8)

    rows = bsz * ts
    h = jnp.swapaxes(u_ref[...], 0, 1).reshape(rows, width)
    gate = gate_ref[...].reshape(rows, AB_WIDTH).astype(F32)
    y_rnn = (h * gate[:, :RNN_WIDTH]).astype(BF16)
    y_mla = (ya_ref[...].reshape(rows, MLA_WIDTH).astype(F32) * gate[:, RNN_WIDTH:]).astype(BF16)
    y = _dot(y_rnn, w_out_ref[:RNN_WIDTH, :]) + _dot(y_mla, w_out_ref[RNN_WIDTH:, :])
    out = _layernorm(DN_ALPHA * x_ref[...].reshape(rows, D_MODEL) + y,
                     ln_g_ref[...], ln_b_ref[...])
    o_ref[...] = out.reshape(bsz, ts, D_MODEL)


def _rglru_call(xr, y_mla, gate, x, conv_w, conv_b, w_gate, b_gate, lam, w_out, ln_g, ln_b, ts):
    bsz, seqlen, width = xr.shape
    blk = lambda w: pl.BlockSpec((bsz, ts, w), lambda i: (0, i, 0))
    return pl.pallas_call(
        _rglru_kernel,
        grid=(seqlen // ts,),
        in_specs=[blk(width), blk(MLA_WIDTH), blk(AB_WIDTH), blk(D_MODEL),
                  _const_spec((CONV_WIDTH, width)), _const_spec((1, width)),
                  _const_spec((width, 2 * width)), _const_spec((1, 2 * width)),
                  _const_spec((1, width)), _const_spec((AB_WIDTH, D_MODEL)),
                  _const_spec((1, D_MODEL)), _const_spec((1, D_MODEL))],
        out_specs=blk(D_MODEL),
        out_shape=jax.ShapeDtypeStruct((bsz, seqlen, D_MODEL), F32),
        scratch_shapes=[pltpu.VMEM((CONV_WIDTH - 1, bsz, width), F32),
                        pltpu.VMEM((bsz, width), F32),
                        pltpu.VMEM((ts, bsz, width), F32),
                        pltpu.VMEM((ts, bsz, width), F32)],
        compiler_params=_params(("arbitrary",)),
        name="rglru_out_proj",
    )(xr, y_mla, gate, x, conv_w, conv_b, w_gate, b_gate, lam, w_out, ln_g, ln_b)


ATTN_ROWS = 256


def _mla_attn_tiles(q_ref, k_ref, vt_ref, o_ref, s_refs, tiles, tq):
    heads = q_ref.shape[2] // HEAD_PAD
    halves = tq // ATTN_ROWS
    units = [(h, r) for h in range(heads) for r in range(halves)]
    steps = [(qi, j) for qi in tiles for j in range(qi + 1)]

    def depth(qi, j, r):
        return (r + 1) * ATTN_ROWS if j == qi else tq

    def scores_into(dst_ref, qi, j):
        for u, (h, r) in enumerate(units):
            w = depth(qi, j, r)
            q_u = q_ref[0, qi * tq + r * ATTN_ROWS:qi * tq + (r + 1) * ATTN_ROWS,
                        h * HEAD_PAD:(h + 1) * HEAD_PAD]
            k_blk = k_ref[0, j * tq:j * tq + w, h * HEAD_PAD:(h + 1) * HEAD_PAD]
            dst_ref[u, :w, :] = lax.dot_general(k_blk, q_u, (((1,), (1,)), ((), ())),
                                                preferred_element_type=F32)

    scores_into(s_refs[0], *steps[0])
    ms, accs = None, None
    for i, (qi, j) in enumerate(steps):
        src_ref = s_refs[i % 2]
        if i + 1 < len(steps):
            scores_into(s_refs[(i + 1) % 2], *steps[i + 1])
        if j == 0:
            ms = [jnp.full((1, ATTN_ROWS), -1e30, F32) for _ in units]
            accs = [jnp.zeros((MLA_VT_ROWS, ATTN_ROWS), F32) for _ in units]
        for u, (h, r) in enumerate(units):
            w = depth(qi, j, r)
            if j == qi:
                key = lax.broadcasted_iota(jnp.int32, (ATTN_ROWS, ATTN_ROWS), 0)
                qry = lax.broadcasted_iota(jnp.int32, (ATTN_ROWS, ATTN_ROWS), 1)
                edge = slice(w - ATTN_ROWS, w)
                src_ref[u, edge, :] = jnp.where(key <= qry, src_ref[u, edge, :], -1e30)
            vt_blk = vt_ref[0, h, :, j * tq:j * tq + w]
            m_new = jnp.maximum(ms[u], jnp.max(src_ref[u, :w, :], axis=0, keepdims=True))
            p = jnp.exp2((src_ref[u, :w, :] - m_new).astype(BF16))
            accs[u] = jnp.exp2(ms[u] - m_new) * accs[u] + _dot(vt_blk, p)
            ms[u] = m_new
        if j == qi:
            outs = [acc[:MLA_V] / acc[MLA_V:MLA_V + 1] for acc in accs]
            for pair in range(heads // 2):
                for r in range(halves):
                    o_t = jnp.concatenate([outs[2 * pair * halves + r],
                                           outs[(2 * pair + 1) * halves + r]], axis=0)
                    o_ref[0, qi * tq + r * ATTN_ROWS:qi * tq + (r + 1) * ATTN_ROWS,
                          pair * HEAD_PAD:(pair + 1) * HEAD_PAD] = o_t.T.astype(o_ref.dtype)


def _mla_attn_kernel(q_ref, k_ref, vt_ref, o_ref, s0_ref, s1_ref, *, tq, groups):
    g = pl.program_id(2)
    for idx, tiles in enumerate(groups):
        @pl.when(g == idx)
        def _(tiles=tiles):
            _mla_attn_tiles(q_ref, k_ref, vt_ref, o_ref, (s0_ref, s1_ref), tiles, tq)


def _mla_attn_call(q, k, vt, tq):
    bsz, seqlen, _ = q.shape
    heads_per_step = 2
    width = heads_per_step * HEAD_PAD
    n_hsteps = MLA_HEADS // heads_per_step
    out_w = heads_per_step * MLA_V
    n_units = heads_per_step * (tq // ATTN_ROWS)
    n_q = seqlen // tq
    groups = tuple(tuple(sorted({i, n_q - 1 - i})) for i in range((n_q + 1) // 2))
    resident = lambda w: pl.BlockSpec((1, seqlen, w), lambda b, h, g: (b, 0, h))
    return pl.pallas_call(
        functools.partial(_mla_attn_kernel, tq=tq, groups=groups),
        grid=(bsz, n_hsteps, len(groups)),
        in_specs=[resident(width), resident(width),
                  pl.BlockSpec((1, heads_per_step, MLA_VT_ROWS, seqlen),
                               lambda b, h, g: (b, h, 0, 0))],
        out_specs=resident(out_w),
        out_shape=jax.ShapeDtypeStruct((bsz, seqlen, MLA_WIDTH), BF16),
        scratch_shapes=[pltpu.VMEM((n_units, tq, ATTN_ROWS), F32),
                        pltpu.VMEM((n_units, tq, ATTN_ROWS), F32)],
        compiler_params=_params(("parallel", "parallel", "arbitrary")),
        name="mla_attention",
    )(q, k, vt)


SSD_IN_COLS = 512


def _conv_silu_store(proj, tail, w, b, out_ref, cols):
    n_rows = proj.shape[0]
    taps = [w[k:k + 1, :] for k in range(CONV_WIDTH)]
    first = lax.broadcasted_iota(jnp.int32, tail.shape, 0) == 0
    prev = None
    done = []
    for i in range(-1, n_rows // SUBLANES):
        x = tail if i < 0 else proj[i * SUBLANES:(i + 1) * SUBLANES]
        acc = x * taps[0]
        rolled = []
        for k in range(1, CONV_WIDTH):
            r = pltpu.roll(acc, 1, axis=0)
            rolled.append(r)
            acc = (r if prev is None else jnp.where(first, prev[k - 1], r)) + x * taps[k]
        prev = rolled
        if i >= 0:
            done.append(acc + b)
        if len(done) == BF16_ROWS // SUBLANES:
            top = (i + 1) * SUBLANES
            out_ref[0, top - BF16_ROWS:top, cols] = _silu(
                jnp.concatenate(done, axis=0)).astype(BF16)
            done = []


SSD_PROJ_WIDTH = SSD_INNER + SSD_CONV_DIM + LANES


def _ssd_in_stage(step, x_ref, w_ref, dt_bias_ref, conv_w_ref, conv_b_ref,
                  z_ref, xbc_ref, dt_ref, new_ref, old_ref, project=True, finish=True):
    tm = x_ref.shape[1]
    x_off = SSD_INNER
    if finish:
        tails = []
        for c in range(SSD_CONV_DIM // SSD_IN_COLS):
            cols = slice(x_off + c * SSD_IN_COLS, x_off + (c + 1) * SSD_IN_COLS)
            tails.append(jnp.where(step == 1, 0.0, new_ref[tm - SUBLANES:, cols]))

    if project:
        xb = x_ref[0].astype(BF16)
        for c in range(0, SSD_PROJ_WIDTH, SSD_IN_COLS):
            cols = slice(c, min(c + SSD_IN_COLS, SSD_PROJ_WIDTH))
            new_ref[:, cols] = _dot(xb, w_ref[:, cols])
    if not finish:
        return

    for c in range(SSD_INNER // SSD_IN_COLS):
        cols = slice(c * SSD_IN_COLS, (c + 1) * SSD_IN_COLS)
        z_ref[0, :, cols] = _silu(old_ref[:, cols]).astype(BF16)
    for c in range(SSD_CONV_DIM // SSD_IN_COLS):
        cols = slice(c * SSD_IN_COLS, (c + 1) * SSD_IN_COLS)
        p_cols = slice(x_off + c * SSD_IN_COLS, x_off + (c + 1) * SSD_IN_COLS)
        _conv_silu_store(old_ref[:, p_cols], tails[c], conv_w_ref[:, cols], conv_b_ref[:, cols],
                         xbc_ref, cols)
    dt_ref[0] = _softplus(old_ref[:, SSD_INNER + SSD_CONV_DIM:] + dt_bias_ref[...])


def _ssd_in_kernel(x_ref, w_ref, dt_bias_ref, conv_w_ref, conv_b_ref,
                   z_ref, xbc_ref, dt_ref, p0_ref, p1_ref):
    step = pl.program_id(1)
    last = pl.num_programs(1) - 1
    args = (step, x_ref, w_ref, dt_bias_ref, conv_w_ref, conv_b_ref, z_ref, xbc_ref, dt_ref)
    bufs = (p0_ref, p1_ref)

    @pl.when(step == 0)
    def _():
        p1_ref[p1_ref.shape[0] - SUBLANES:, :] = jnp.zeros((SUBLANES, p1_ref.shape[1]), F32)
        _ssd_in_stage(*args, p0_ref, p1_ref, finish=False)

    for parity in range(2):
        @pl.when((step > 0) & (step < last) & (step % 2 == parity))
        def _(parity=parity):
            _ssd_in_stage(*args, bufs[parity], bufs[1 - parity])

        @pl.when((step == last) & (step % 2 == parity))
        def _(parity=parity):
            _ssd_in_stage(*args, bufs[parity], bufs[1 - parity], project=False)


def _ssd_in_call(x, w_all, dt_bias, conv_w, conv_b, tm):
    bsz, seqlen, _ = x.shape
    n_tiles = seqlen // tm
    tile_in = pl.BlockSpec((1, tm, D_MODEL), lambda b, i: (b, jnp.minimum(i, n_tiles - 1), 0))
    tile_out = lambda width: pl.BlockSpec((1, tm, width),
                                          lambda b, i: (b, jnp.maximum(i - 1, 0), 0))
    return pl.pallas_call(
        _ssd_in_kernel,
        grid=(bsz, n_tiles + 1),
        in_specs=[tile_in, _const_spec((D_MODEL, SSD_PROJ_WIDTH)),
                  _const_spec((1, LANES)), _const_spec((CONV_WIDTH, SSD_CONV_DIM)),
                  _const_spec((1, SSD_CONV_DIM))],
        out_specs=[tile_out(SSD_INNER), tile_out(SSD_CONV_DIM), tile_out(LANES)],
        out_shape=[jax.ShapeDtypeStruct((bsz, seqlen, SSD_INNER), BF16),
                   jax.ShapeDtypeStruct((bsz, seqlen, SSD_CONV_DIM), BF16),
                   jax.ShapeDtypeStruct((bsz, seqlen, LANES), F32)],
        scratch_shapes=[pltpu.VMEM((tm, SSD_PROJ_WIDTH), F32),
                        pltpu.VMEM((tm, SSD_PROJ_WIDTH), F32)],
        compiler_params=_params(("parallel", "arbitrary")),
        name="ssd_in_proj",
    )(x, w_all, dt_bias, conv_w, conv_b)


def _cumsum_rows(x):
    n = x.shape[0]
    row = lax.broadcasted_iota(jnp.int32, x.shape, 0)
    shift = 1
    while shift < n:
        x = x + jnp.where(row >= shift, pltpu.roll(x, shift, axis=0), 0.0)
        shift *= 2
    return x


LOG2E = math.log2(math.e)


def _ssd_chunk(xbc_ref, rows, dt, state_ref, a_neg2, d_skip_x):
    L = SSD_CHUNK
    gw = SSD_GROUPS * SSD_STATE

    cs = _cumsum_rows(dt * a_neg2)
    cs_t = cs.T
    dt_t = dt.T
    w_t = dt_t * jnp.exp2(cs_t[:, L - 1:L] - cs_t)

    row = lax.broadcasted_iota(jnp.int32, (L, L), 0)
    col = lax.broadcasted_iota(jnp.int32, (L, L), 1)
    tril = col <= row
    lane = lax.broadcasted_iota(jnp.int32, (L, LANES), 1)
    low = lane < SSD_HEAD_DIM
    keep_lo = low.astype(F32).astype(BF16)
    keep_hi = 1.0 - keep_lo

    hpg = SSD_HEADS // SSD_GROUPS
    y_parts = []
    for g in range(SSD_GROUPS):
        b_g = xbc_ref[0, rows, SSD_INNER + g * SSD_STATE:SSD_INNER + (g + 1) * SSD_STATE]
        c_g = xbc_ref[0, rows, SSD_INNER + gw + g * SSD_STATE:SSD_INNER + gw + (g + 1) * SSD_STATE]
        cb = lax.dot_general(c_g, b_g, (((1,), (1,)), ((), ())),
                             preferred_element_type=F32)
        cb = jnp.where(tril, cb, 0.0)
        b_t = b_g.astype(F32).T
        glanes = slice(g * SSD_GROUP_WIDTH, (g + 1) * SSD_GROUP_WIDTH)
        y_off = _dot(c_g, state_ref[:, glanes].astype(BF16))
        for pair in range(hpg // 2):
            h0 = g * hpg + 2 * pair
            plane = slice(h0 * SSD_HEAD_DIM, (h0 + 2) * SSD_HEAD_DIM)
            tops, bots, decays = [], [], []
            for h in (h0, h0 + 1):
                cs_col = jnp.broadcast_to(cs[:, h:h + 1], (L, LANES))
                seg = cs_col - cs_t[h:h + 1, :]
                m_h = jnp.exp2(jnp.minimum(seg, 0.0)) * (cb * dt_t[h:h + 1, :])
                tops.append(m_h.astype(BF16))
                bots.append((b_t * w_t[h:h + 1, :]).astype(BF16))
                decays.append(jnp.exp2(cs_col))
            lhs = jnp.concatenate([jnp.concatenate(tops, axis=1),
                                   jnp.concatenate(bots, axis=1)], axis=0)
            x_pair = xbc_ref[0, rows, plane]
            rhs = jnp.concatenate([x_pair * keep_lo, x_pair * keep_hi], axis=0)
            res = _dot(lhs, rhs)
            dec = jnp.where(low, decays[0], decays[1])
            off = pair * 2 * SSD_HEAD_DIM
            y_pair = (res[:L] + dec * y_off[:, off:off + 2 * SSD_HEAD_DIM]
                      + d_skip_x[:, plane] * x_pair.astype(F32))
            y_parts.append(y_pair)
            state_ref[:, plane] = state_ref[:, plane] * dec[L - 1:L, :] + res[L:]
    return jnp.concatenate(y_parts, axis=1)


def _ssd_stage(z_ref, xbc_ref, dt_ref, x_ref, a_log_ref, d_skip_ref, norm_ref, w_out_ref,
               g_ref, b_ref, o_ref, state_ref, new_ref, old_ref, n_chunks,
               scan=True, finish=True):
    L = SSD_CHUNK
    if scan:
        a_neg2 = -jnp.exp(a_log_ref[...]) * LOG2E
        for c in range(n_chunks):
            rows = slice(c * L, (c + 1) * L)
            new_ref[rows, :] = _ssd_chunk(xbc_ref, rows, dt_ref[0, rows, :], state_ref, a_neg2,
                                          d_skip_ref[...])
    if finish:
        for c in range(n_chunks):
            rows = slice(c * L, (c + 1) * L)
            y = old_ref[rows, :] * z_ref[0, rows, :].astype(F32)
            parts = []
            for g in range(SSD_GROUPS):
                yg = y[:, g * SSD_GROUP_WIDTH:(g + 1) * SSD_GROUP_WIDTH]
                parts.append(yg * lax.rsqrt(jnp.mean(yg * yg, axis=-1, keepdims=True) + 1e-6))
            yn = (jnp.concatenate(parts, axis=1) * norm_ref[...]).astype(BF16)
            out = _dot(yn, w_out_ref[...])
            o_ref[0, rows, :] = _layernorm(DN_ALPHA * x_ref[0, rows, :] + out,
                                           g_ref[...], b_ref[...])


def _ssd_kernel(*refs, n_chunks):
    io_refs, state_ref, bufs = refs[:-3], refs[-3], refs[-2:]
    step = pl.program_id(1)
    last = pl.num_programs(1) - 1

    @pl.when(step == 0)
    def _():
        state_ref[...] = jnp.zeros_like(state_ref)
        _ssd_stage(*io_refs, state_ref, bufs[0], bufs[1], n_chunks, finish=False)

    for parity in range(2):
        @pl.when((step > 0) & (step < last) & (step % 2 == parity))
        def _(parity=parity):
            _ssd_stage(*io_refs, state_ref, bufs[parity], bufs[1 - parity], n_chunks)

        @pl.when((step == last) & (step % 2 == parity))
        def _(parity=parity):
            _ssd_stage(*io_refs, state_ref, bufs[parity], bufs[1 - parity], n_chunks, scan=False)


def _ssd_call(z, xbc, dt, x, a_log, d_skip_x, norm_w, w_out, ln_g, ln_b, n_chunks):
    bsz, seqlen, _ = x.shape
    tm = n_chunks * SSD_CHUNK
    n_tiles = seqlen // tm
    cur = lambda width: pl.BlockSpec((1, tm, width),
                                     lambda b, i: (b, jnp.minimum(i, n_tiles - 1), 0))
    prev = lambda width: pl.BlockSpec((1, tm, width),
                                      lambda b, i: (b, jnp.maximum(i - 1, 0), 0))
    return pl.pallas_call(
        functools.partial(_ssd_kernel, n_chunks=n_chunks),
        grid=(bsz, n_tiles + 1),
        in_specs=[prev(SSD_INNER), cur(SSD_CONV_DIM), cur(LANES), prev(D_MODEL),
                  _const_spec((1, LANES)), _const_spec((1, SSD_INNER)),
                  _const_spec((1, SSD_INNER)), _const_spec((SSD_INNER, D_MODEL)),
                  _const_spec((1, D_MODEL)), _const_spec((1, D_MODEL))],
        out_specs=prev(D_MODEL),
        out_shape=jax.ShapeDtypeStruct((bsz, seqlen, D_MODEL), F32),
        scratch_shapes=[pltpu.VMEM((SSD_STATE, SSD_INNER), F32),
                        pltpu.VMEM((tm, SSD_INNER), F32), pltpu.VMEM((tm, SSD_INNER), F32)],
        compiler_params=_params(("parallel", "arbitrary")),
        name="ssd_scan_out",
    )(z, xbc, dt, x, a_log, d_skip_x, norm_w, w_out, ln_g, ln_b)


def _rot_cols(w):
    half = MLA_ROPE // 2
    return jnp.concatenate([-w[..., half:], w[..., :half]], axis=-1)


def _pad_heads(w_heads):
    r, h, c = w_heads.shape
    return jnp.pad(w_heads, ((0, 0), (0, 0), (0, HEAD_PAD - c))).reshape(r, h * HEAD_PAD)


def _block_diag(w):
    h, d, _ = w.shape
    eye = jnp.eye(h, dtype=w.dtype)
    return (eye[:, None, :, None] * w[:, :, None, :]).reshape(h * d, h * d)


def _layer0(x, pos_f, freq, w_in, conv_w, conv_b, gate_a_w, gate_a_b, gate_x_w, gate_x_b, lam,
            q_norm, kv_norm, w_uq, w_ukv, w_out, ln_g, ln_b, tm, ts, tq):
    o_kr = RNN_WIDTH + AB_WIDTH + MLA_Q_RANK + MLA_KV_RANK
    w_kr = w_in[:, o_kr:]
    w_in_ext = jnp.concatenate([w_in[:, :o_kr], jnp.zeros_like(w_in[:, :MLA_NOPE]), w_kr,
                                _rot_cols(w_kr)], axis=1).astype(BF16)

    uq = w_uq.reshape(MLA_Q_RANK, MLA_HEADS, MLA_NOPE + MLA_ROPE)
    wq = _pad_heads(jnp.concatenate([uq, _rot_cols(uq[..., MLA_NOPE:])], axis=-1)).astype(BF16)
    ukv = w_ukv.reshape(MLA_KV_RANK, MLA_HEADS, MLA_NOPE + MLA_V)
    wk = _pad_heads(ukv[..., :MLA_NOPE]).astype(BF16)
    wvt = jnp.pad(jnp.transpose(ukv[..., MLA_NOPE:], (1, 2, 0)),
                  ((0, 0), (0, MLA_VT_ROWS - MLA_V), (0, 0)))
    wvt = wvt.reshape(MLA_HEADS * MLA_VT_ROWS, MLA_KV_RANK).astype(BF16)
    vones = jnp.tile((jnp.arange(MLA_VT_ROWS) >= MLA_V).astype(F32), MLA_HEADS)[:, None]

    xr, gate, q, k, vt = _ab_in_call(x, pos_f, w_in_ext, freq, q_norm[None], kv_norm[None],
                                     wq, wk, wvt, vones, tm)

    w_gate = jnp.concatenate([_block_diag(gate_a_w), _block_diag(gate_x_w)], axis=1).astype(BF16)
    b_gate = jnp.concatenate([gate_a_b, gate_x_b])[None]
    y_mla = _mla_attn_call(q, k, vt, tq)
    return _rglru_call(xr, y_mla, gate, x, conv_w, conv_b[None], w_gate, b_gate, lam[None],
                       w_out.astype(BF16), ln_g[None], ln_b[None], ts)


def _layer1(x, w_in, conv_w, conv_b, dt_bias, a_log, d_skip, norm_w, w_out, ln_g, ln_b,
            tm, n_chunks):
    pad_h = lambda a: jnp.pad(a, ((0, 0), (0, LANES - SSD_HEADS)))
    w_all = pad_h(w_in).astype(BF16)
    z, xbc, dt = _ssd_in_call(x, w_all, pad_h(dt_bias[None]), conv_w, conv_b[None], tm)
    d_skip_x = jnp.repeat(d_skip, SSD_HEAD_DIM)[None]
    return _ssd_call(z, xbc, dt, x, pad_h(a_log[None]), d_skip_x, norm_w[None],
                     w_out.astype(BF16), ln_g[None], ln_b[None], n_chunks)


def kernel(x, positions, ab_w_in, ab_conv_w, ab_conv_b, ab_gate_a_w, ab_gate_a_b, ab_gate_x_w,
           ab_gate_x_b, ab_lambda, mla_q_norm, mla_kv_norm, mla_w_uq, mla_w_ukv, ab_w_out,
           ab_ln_g, ab_ln_b, ssd_w_in, ssd_conv_w, ssd_conv_b, ssd_dt_bias, ssd_a_log, ssd_d,
           ssd_norm, ssd_w_out, ssd_ln_g, ssd_ln_b):
    seqlen = x.shape[1]
    tm = min(1024, seqlen)
    tm_ssd = min(256, seqlen)
    ts = min(128, seqlen)
    tq = min(512, seqlen)
    n_chunks = min(4, seqlen // SSD_CHUNK)

    inv_freq = ROPE_THETA ** (-jnp.arange(0, MLA_ROPE, 2, dtype=F32) / MLA_ROPE)
    freq = jnp.concatenate([jnp.zeros((MLA_NOPE,), F32), inv_freq, inv_freq,
                            jnp.zeros((HEAD_PAD - MLA_NOPE - MLA_ROPE,), F32)])[None]
    pos_f = positions.astype(F32)[:, None, :]

    for layer in range(DEPTH):
        j = layer // 2
        if layer % 2 == 0:
            x = _layer0(x, pos_f, freq, ab_w_in[j], ab_conv_w[j], ab_conv_b[j], ab_gate_a_w[j],
                        ab_gate_a_b[j], ab_gate_x_w[j], ab_gate_x_b[j], ab_lambda[j],
                        mla_q_norm[j], mla_kv_norm[j], mla_w_uq[j], mla_w_ukv[j], ab_w_out[j],
                        ab_ln_g[j], ab_ln_b[j], tm, ts, tq)
        else:
            x = _layer1(x, ssd_w_in[j], ssd_conv_w[j], ssd_conv_b[j], ssd_dt_bias[j],
                        ssd_a_log[j], ssd_d[j], ssd_norm[j], ssd_w_out[j], ssd_ln_g[j],
                        ssd_ln_b[j], tm_ssd, n_chunks)
    return x
```

```python
import functools
import math

import jax
import jax.numpy as jnp
from jax import lax
from jax.experimental import pallas as pl
from jax.experimental.pallas import tpu as pltpu

D_MODEL = 1024
DEPTH = 2
DN_ALPHA = (2.0 * DEPTH) ** 0.25

RNN_WIDTH = 512
RNN_HEADS = 8
RNN_HEAD_DIM = RNN_WIDTH // RNN_HEADS
CONV_WIDTH = 4
RG_C = 8.0

MLA_HEADS = 8
MLA_NOPE = 64
MLA_ROPE = 32
MLA_V = 64
MLA_Q_RANK = 256
MLA_KV_RANK = 128
MLA_WIDTH = MLA_HEADS * MLA_V
ROPE_THETA = 10000.0
AB_WIDTH = RNN_WIDTH + MLA_WIDTH

SSD_INNER = 2048
SSD_HEAD_DIM = 64
SSD_HEADS = 32
SSD_GROUPS = 4
SSD_STATE = 128
SSD_CHUNK = 128
SSD_CONV_DIM = SSD_INNER + 2 * SSD_GROUPS * SSD_STATE
SSD_GROUP_WIDTH = SSD_INNER // SSD_GROUPS

LANES = 128
SUBLANES = 8
BF16_ROWS = 2 * SUBLANES
HEAD_PAD = LANES
MLA_VT_ROWS = MLA_V + BF16_ROWS
QK_PAD_WIDTH = MLA_HEADS * HEAD_PAD
VMEM_LIMIT = 56 * 1024 * 1024

BF16 = jnp.bfloat16
F32 = jnp.float32


NEG_LOG2E = -math.log2(math.e)


def _exp_neg(x):
    return jnp.exp2(x * NEG_LOG2E)


def _sigmoid(x):
    return 1.0 / (1.0 + _exp_neg(x))


def _silu(x):
    return x * _sigmoid(x)


def _log1p(y):
    u = 1.0 + y
    return jnp.where(u == 1.0, y, jnp.log(u) * (y / (u - 1.0)))


def _softplus(x):
    return jnp.maximum(x, 0.0) + _log1p(_exp_neg(jnp.abs(x)))


def _dot(a, b):
    return jnp.dot(a, b, preferred_element_type=F32)


def _const_spec(shape):
    zeros = (0,) * len(shape)
    return pl.BlockSpec(shape, lambda *_: zeros)


def _params(semantics):
    return pltpu.CompilerParams(dimension_semantics=semantics,
                                vmem_limit_bytes=VMEM_LIMIT)


def _ab_in_kernel(x_ref, pos_ref, w_in_ref, freq_ref, qn_ref, kvn_ref, wq_ref,
                  wk_ref, wvt_ref, vones_ref,
                  xr_ref, gate_ref, q_ref, k_ref, vt_ref):
    xb = x_ref[0].astype(BF16)
    proj = _dot(xb, w_in_ref[...])
    o_gate = RNN_WIDTH
    o_cq = o_gate + AB_WIDTH
    o_ckv = o_cq + MLA_Q_RANK
    o_kr = o_ckv + MLA_KV_RANK

    xr_ref[0] = proj[:, :o_gate]
    gate_ref[0] = _silu(proj[:, o_gate:o_cq]).astype(BF16)

    pos_rows = jnp.broadcast_to(pos_ref[0], (HEAD_PAD, pos_ref.shape[2])).T
    ang = pos_rows * freq_ref[...]
    cos = jnp.cos(ang)
    sin = jnp.sin(ang)
    to_rope_lanes = HEAD_PAD - MLA_ROPE

    c_q = proj[:, o_cq:o_ckv]
    c_q = c_q * lax.rsqrt(jnp.mean(c_q * c_q, axis=-1, keepdims=True) + 1e-6) * qn_ref[...]
    q_all = _dot(c_q.astype(BF16), wq_ref[...])
    scale = (MLA_NOPE + MLA_ROPE) ** -0.5 * math.log2(math.e)
    for h in range(MLA_HEADS):
        q_h = q_all[:, h * HEAD_PAD:(h + 1) * HEAD_PAD]
        q_h = q_h * cos + pltpu.roll(q_h, to_rope_lanes, axis=1) * sin
        q_ref[0, :, h * HEAD_PAD:(h + 1) * HEAD_PAD] = (q_h * scale).astype(BF16)

    c_kv = proj[:, o_ckv:o_kr]
    c_kv = c_kv * lax.rsqrt(jnp.mean(c_kv * c_kv, axis=-1, keepdims=True) + 1e-6) * kvn_ref[...]
    c_kvb = c_kv.astype(BF16)
    kr = proj[:, o_kr:o_kr + HEAD_PAD]
    lane = lax.broadcasted_iota(jnp.int32, kr.shape, 1)
    k_rope = jnp.where(lane < MLA_NOPE + MLA_ROPE,
                       kr * cos + pltpu.roll(kr, to_rope_lanes, axis=1) * sin, 0.0)
    k = _dot(c_kvb, wk_ref[...]) + jnp.concatenate([k_rope] * MLA_HEADS, axis=1)
    k_ref[0] = k.astype(BF16)
    vt = lax.dot_general(wvt_ref[...], c_kvb, (((1,), (1,)), ((), ())),
                         preferred_element_type=F32) + vones_ref[...]
    vt_ref[0] = vt.reshape(MLA_HEADS, MLA_VT_ROWS, vt.shape[1]).astype(BF16)


def _ab_in_call(x, pos_f, w_in_ext, freq, q_norm, kv_norm, wq, wk, wvt, vones, tm):
    bsz, seqlen, _ = x.shape
    n_in = w_in_ext.shape[1]
    grid = (bsz, seqlen // tm)
    tok = lambda width: pl.BlockSpec((1, tm, width), lambda b, i: (b, i, 0))
    vt_rows = MLA_HEADS * MLA_VT_ROWS
    return pl.pallas_call(
        _ab_in_kernel,
        grid=grid,
        in_specs=[tok(D_MODEL), pl.BlockSpec((1, 1, tm), lambda b, i: (b, 0, i)),
                  _const_spec((D_MODEL, n_in)), _const_spec((1, HEAD_PAD)),
                  _const_spec((1, MLA_Q_RANK)), _const_spec((1, MLA_KV_RANK)),
                  _const_spec((MLA_Q_RANK, QK_PAD_WIDTH)),
                  _const_spec((MLA_KV_RANK, QK_PAD_WIDTH)), _const_spec((vt_rows, MLA_KV_RANK)),
                  _const_spec((vt_rows, 1))],
        out_specs=[tok(RNN_WIDTH), tok(AB_WIDTH), tok(QK_PAD_WIDTH), tok(QK_PAD_WIDTH),
                   pl.BlockSpec((1, MLA_HEADS, MLA_VT_ROWS, tm), lambda b, i: (b, 0, 0, i))],
        out_shape=[jax.ShapeDtypeStruct((bsz, seqlen, RNN_WIDTH), F32),
                   jax.ShapeDtypeStruct((bsz, seqlen, AB_WIDTH), BF16),
                   jax.ShapeDtypeStruct((bsz, seqlen, QK_PAD_WIDTH), BF16),
                   jax.ShapeDtypeStruct((bsz, seqlen, QK_PAD_WIDTH), BF16),
                   jax.ShapeDtypeStruct((bsz, MLA_HEADS, MLA_VT_ROWS, seqlen), BF16)],
        compiler_params=_params(("parallel", "parallel")),
        name="ab_in_proj",
    )(x, pos_f, w_in_ext, freq, q_norm, kv_norm, wq, wk, wvt, vones)


def _layernorm(z, g, b):
    mu = jnp.mean(z, axis=-1, keepdims=True)
    zc = z - mu
    var = jnp.mean(zc * zc, axis=-1, keepdims=True)
    return zc * lax.rsqrt(var + 1e-5) * g + b


def _rglru_kernel(xr_ref, ya_ref, gate_ref, x_ref, conv_w_ref, conv_b_ref, w_gate_ref,
                  b_gate_ref, lam_ref, w_out_ref, ln_g_ref, ln_b_ref,
                  o_ref, tail_ref, carry_ref, a_ref, u_ref):
    bsz, ts, width = xr_ref.shape

    @pl.when(pl.program_id(0) == 0)
    def _():
        tail_ref[...] = jnp.zeros_like(tail_ref)
        carry_ref[...] = jnp.zeros_like(carry_ref)

    x_blk = jnp.swapaxes(xr_ref[...], 0, 1)
    x_ext = jnp.concatenate([tail_ref[...], x_blk], axis=0)
    tail_ref[...] = x_blk[ts - (CONV_WIDTH - 1):]
    xc = conv_b_ref[...][None]
    for k in range(CONV_WIDTH):
        xc = xc + x_ext[k:k + ts] * conv_w_ref[k:k + 1, :][None]

    xc2 = xc.reshape(ts * bsz, width)
    gates = _dot(xc2.astype(BF16), w_gate_ref[...]) + b_gate_ref[...]
    r = _sigmoid(gates[:, :width])
    i = _sigmoid(gates[:, width:])
    neg_c_softplus = -RG_C * _softplus(-lam_ref[...])
    log_a = r * neg_c_softplus
    a = jnp.exp2(r * (neg_c_softplus * -NEG_LOG2E))
    mult = jnp.sqrt(-jnp.tanh(log_a) * (a * a + 1.0))
    u = mult * (i * xc2)
    a_ref[...] = a.reshape(ts, bsz, width)
    u_ref[...] = u.reshape(ts, bsz, width)

    def step(t, h):
        h = a_ref[t] * h + u_ref[t]
        u_ref[t] = h
        return h

    carry_ref[...] = lax.fori_loop(0, ts, step, carry_ref[...], unroll=8)

    rows = bsz * ts
    h = jnp.swapaxes(u_ref[...], 0, 1).reshape(rows, width)
    gate = gate_ref[...].reshape(rows, AB_WIDTH).astype(F32)
    y_rnn = (h * gate[:, :RNN_WIDTH]).astype(BF16)
    y_mla = (ya_ref[...].reshape(rows, MLA_WIDTH).astype(F32) * gate[:, RNN_WIDTH:]).astype(BF16)
    y = _dot(y_rnn, w_out_ref[:RNN_WIDTH, :]) + _dot(y_mla, w_out_ref[RNN_WIDTH:, :])
    out = _layernorm(DN_ALPHA * x_ref[...].reshape(rows, D_MODEL) + y,
                     ln_g_ref[...], ln_b_ref[...])
    o_ref[...] = out.reshape(bsz, ts, D_MODEL)


def _rglru_call(xr, y_mla, gate, x, conv_w, conv_b, w_gate, b_gate, lam, w_out, ln_g, ln_b, ts):
    bsz, seqlen, width = xr.shape
    blk = lambda w: pl.BlockSpec((bsz, ts, w), lambda i: (0, i, 0))
    return pl.pallas_call(
        _rglru_kernel,
        grid=(seqlen // ts,),
        in_specs=[blk(width), blk(MLA_WIDTH), blk(AB_WIDTH), blk(D_MODEL),
                  _const_spec((CONV_WIDTH, width)), _const_spec((1, width)),
                  _const_spec((width, 2 * width)), _const_spec((1, 2 * width)),
                  _const_spec((1, width)), _const_spec((AB_WIDTH, D_MODEL)),
                  _const_spec((1, D_MODEL)), _const_spec((1, D_MODEL))],
        out_specs=blk(D_MODEL),
        out_shape=jax.ShapeDtypeStruct((bsz, seqlen, D_MODEL), F32),
        scratch_shapes=[pltpu.VMEM((CONV_WIDTH - 1, bsz, width), F32),
                        pltpu.VMEM((bsz, width), F32),
                        pltpu.VMEM((ts, bsz, width), F32),
                        pltpu.VMEM((ts, bsz, width), F32)],
        compiler_params=_params(("arbitrary",)),
        name="rglru_out_proj",
    )(xr, y_mla, gate, x, conv_w, conv_b, w_gate, b_gate, lam, w_out, ln_g, ln_b)


ATTN_ROWS = 256


def _mla_attn_tiles(q_ref, k_ref, vt_ref, o_ref, s_refs, tiles, tq):
    heads = q_ref.shape[2] // HEAD_PAD
    halves = tq // ATTN_ROWS
    units = [(h, r) for h in range(heads) for r in range(halves)]
    steps = [(qi, j) for qi in tiles for j in range(qi + 1)]

    def depth(qi, j, r):
        return (r + 1) * ATTN_ROWS if j == qi else tq

    def scores_into(dst_ref, qi, j):
        for u, (h, r) in enumerate(units):
            w = depth(qi, j, r)
            q_u = q_ref[0, qi * tq + r * ATTN_ROWS:qi * tq + (r + 1) * ATTN_ROWS,
                        h * HEAD_PAD:(h + 1) * HEAD_PAD]
            k_blk = k_ref[0, j * tq:j * tq + w, h * HEAD_PAD:(h + 1) * HEAD_PAD]
            dst_ref[u, :w, :] = lax.dot_general(k_blk, q_u, (((1,), (1,)), ((), ())),
                                                preferred_element_type=F32)

    scores_into(s_refs[0], *steps[0])
    ms, accs = None, None
    for i, (qi, j) in enumerate(steps):
        src_ref = s_refs[i % 2]
        if i + 1 < len(steps):
            scores_into(s_refs[(i + 1) % 2], *steps[i + 1])
        if j == 0:
            ms = [jnp.full((1, ATTN_ROWS), -1e30, F32) for _ in units]
            accs = [jnp.zeros((MLA_VT_ROWS, ATTN_ROWS), F32) for _ in units]
        for u, (h, r) in enumerate(units):
            w = depth(qi, j, r)
            if j == qi:
                key = lax.broadcasted_iota(jnp.int32, (ATTN_ROWS, ATTN_ROWS), 0)
                qry = lax.broadcasted_iota(jnp.int32, (ATTN_ROWS, ATTN_ROWS), 1)
                edge = slice(w - ATTN_ROWS, w)
                src_ref[u, edge, :] = jnp.where(key <= qry, src_ref[u, edge, :], -1e30)
            vt_blk = vt_ref[0, h, :, j * tq:j * tq + w]
            m_new = jnp.maximum(ms[u], jnp.max(src_ref[u, :w, :], axis=0, keepdims=True))
            p = jnp.exp2((src_ref[u, :w, :] - m_new).astype(BF16))
            accs[u] = jnp.exp2(ms[u] - m_new) * accs[u] + _dot(vt_blk, p)
            ms[u] = m_new
        if j == qi:
            outs = [acc[:MLA_V] / acc[MLA_V:MLA_V + 1] for acc in accs]
            for pair in range(heads // 2):
                for r in range(halves):
                    o_t = jnp.concatenate([outs[2 * pair * halves + r],
                                           outs[(2 * pair + 1) * halves + r]], axis=0)
                    o_ref[0, qi * tq + r * ATTN_ROWS:qi * tq + (r + 1) * ATTN_ROWS,
                          pair * HEAD_PAD:(pair + 1) * HEAD_PAD] = o_t.T.astype(o_ref.dtype)


def _mla_attn_kernel(q_ref, k_ref, vt_ref, o_ref, s0_ref, s1_ref, *, tq, groups):
    g = pl.program_id(2)
    for idx, tiles in enumerate(groups):
        @pl.when(g == idx)
        def _(tiles=tiles):
            _mla_attn_tiles(q_ref, k_ref, vt_ref, o_ref, (s0_ref, s1_ref), tiles, tq)


def _mla_attn_call(q, k, vt, tq):
    bsz, seqlen, _ = q.shape
    heads_per_step = 2
    width = heads_per_step * HEAD_PAD
    n_hsteps = MLA_HEADS // heads_per_step
    out_w = heads_per_step * MLA_V
    n_units = heads_per_step * (tq // ATTN_ROWS)
    n_q = seqlen // tq
    groups = tuple(tuple(sorted({i, n_q - 1 - i})) for i in range((n_q + 1) // 2))
    resident = lambda w: pl.BlockSpec((1, seqlen, w), lambda b, h, g: (b, 0, h))
    return pl.pallas_call(
        functools.partial(_mla_attn_kernel, tq=tq, groups=groups),
        grid=(bsz, n_hsteps, len(groups)),
        in_specs=[resident(width), resident(width),
                  pl.BlockSpec((1, heads_per_step, MLA_VT_ROWS, seqlen),
                               lambda b, h, g: (b, h, 0, 0))],
        out_specs=resident(out_w),
        out_shape=jax.ShapeDtypeStruct((bsz, seqlen, MLA_WIDTH), BF16),
        scratch_shapes=[pltpu.VMEM((n_units, tq, ATTN_ROWS), F32),
                        pltpu.VMEM((n_units, tq, ATTN_ROWS), F32)],
        compiler_params=_params(("parallel", "parallel", "arbitrary")),
        name="mla_attention",
    )(q, k, vt)


SSD_IN_COLS = 512


def _conv_silu_store(proj, tail, w, b, out_ref, cols):
    n_rows = proj.shape[0]
    taps = [w[k:k + 1, :] for k in range(CONV_WIDTH)]
    first = lax.broadcasted_iota(jnp.int32, tail.shape, 0) == 0
    prev = None
    done = []
    for i in range(-1, n_rows // SUBLANES):
        x = tail if i < 0 else proj[i * SUBLANES:(i + 1) * SUBLANES]
        acc = x * taps[0]
        rolled = []
        for k in range(1, CONV_WIDTH):
            r = pltpu.roll(acc, 1, axis=0)
            rolled.append(r)
            acc = (r if prev is None else jnp.where(first, prev[k - 1], r)) + x * taps[k]
        prev = rolled
        if i >= 0:
            done.append(acc + b)
        if len(done) == BF16_ROWS // SUBLANES:
            top = (i + 1) * SUBLANES
            out_ref[0, top - BF16_ROWS:top, cols] = _silu(
                jnp.concatenate(done, axis=0)).astype(BF16)
            done = []


SSD_PROJ_WIDTH = SSD_INNER + SSD_CONV_DIM + LANES


def _ssd_in_stage(step, x_ref, w_ref, dt_bias_ref, conv_w_ref, conv_b_ref,
                  z_ref, xbc_ref, dt_ref, new_ref, old_ref, project=True, finish=True):
    tm = x_ref.shape[1]
    x_off = SSD_INNER
    if finish:
        tails = []
        for c in range(SSD_CONV_DIM // SSD_IN_COLS):
            cols = slice(x_off + c * SSD_IN_COLS, x_off + (c + 1) * SSD_IN_COLS)
            tails.append(jnp.where(step == 1, 0.0, new_ref[tm - SUBLANES:, cols]))

    if project:
        xb = x_ref[0].astype(BF16)
        for c in range(0, SSD_PROJ_WIDTH, SSD_IN_COLS):
            cols = slice(c, min(c + SSD_IN_COLS, SSD_PROJ_WIDTH))
            new_ref[:, cols] = _dot(xb, w_ref[:, cols])
    if not finish:
        return

    for c in range(SSD_INNER // SSD_IN_COLS):
        cols = slice(c * SSD_IN_COLS, (c + 1) * SSD_IN_COLS)
        z_ref[0, :, cols] = _silu(old_ref[:, cols]).astype(BF16)
    for c in range(SSD_CONV_DIM // SSD_IN_COLS):
        cols = slice(c * SSD_IN_COLS, (c + 1) * SSD_IN_COLS)
        p_cols = slice(x_off + c * SSD_IN_COLS, x_off + (c + 1) * SSD_IN_COLS)
        _conv_silu_store(old_ref[:, p_cols], tails[c], conv_w_ref[:, cols], conv_b_ref[:, cols],
                         xbc_ref, cols)
    dt_ref[0] = _softplus(old_ref[:, SSD_INNER + SSD_CONV_DIM:] + dt_bias_ref[...])


def _ssd_in_kernel(x_ref, w_ref, dt_bias_ref, conv_w_ref, conv_b_ref,
                   z_ref, xbc_ref, dt_ref, p0_ref, p1_ref):
    step = pl.program_id(1)
    last = pl.num_programs(1) - 1
    args = (step, x_ref, w_ref, dt_bias_ref, conv_w_ref, conv_b_ref, z_ref, xbc_ref, dt_ref)
    bufs = (p0_ref, p1_ref)

    @pl.when(step == 0)
    def _():
        p1_ref[p1_ref.shape[0] - SUBLANES:, :] = jnp.zeros((SUBLANES, p1_ref.shape[1]), F32)
        _ssd_in_stage(*args, p0_ref, p1_ref, finish=False)

    for parity in range(2):
        @pl.when((step > 0) & (step < last) & (step % 2 == parity))
        def _(parity=parity):
            _ssd_in_stage(*args, bufs[parity], bufs[1 - parity])

        @pl.when((step == last) & (step % 2 == parity))
        def _(parity=parity):
            _ssd_in_stage(*args, bufs[parity], bufs[1 - parity], project=False)


def _ssd_in_call(x, w_all, dt_bias, conv_w, conv_b, tm):
    bsz, seqlen, _ = x.shape
    n_tiles = seqlen // tm
    tile_in = pl.BlockSpec((1, tm, D_MODEL), lambda b, i: (b, jnp.minimum(i, n_tiles - 1), 0))
    tile_out = lambda width: pl.BlockSpec((1, tm, width),
                                          lambda b, i: (b, jnp.maximum(i - 1, 0), 0))
    return pl.pallas_call(
        _ssd_in_kernel,
        grid=(bsz, n_tiles + 1),
        in_specs=[tile_in, _const_spec((D_MODEL, SSD_PROJ_WIDTH)),
                  _const_spec((1, LANES)), _const_spec((CONV_WIDTH, SSD_CONV_DIM)),
                  _const_spec((1, SSD_CONV_DIM))],
        out_specs=[tile_out(SSD_INNER), tile_out(SSD_CONV_DIM), tile_out(LANES)],
        out_shape=[jax.ShapeDtypeStruct((bsz, seqlen, SSD_INNER), BF16),
                   jax.ShapeDtypeStruct((bsz, seqlen, SSD_CONV_DIM), BF16),
                   jax.ShapeDtypeStruct((bsz, seqlen, LANES), F32)],
        scratch_shapes=[pltpu.VMEM((tm, SSD_PROJ_WIDTH), F32),
                        pltpu.VMEM((tm, SSD_PROJ_WIDTH), F32)],
        compiler_params=_params(("parallel", "arbitrary")),
        name="ssd_in_proj",
    )(x, w_all, dt_bias, conv_w, conv_b)


def _cumsum_rows(x):
    n = x.shape[0]
    row = lax.broadcasted_iota(jnp.int32, x.shape, 0)
    shift = 1
    while shift < n:
        x = x + jnp.where(row >= shift, pltpu.roll(x, shift, axis=0), 0.0)
        shift *= 2
    return x


LOG2E = math.log2(math.e)


def _ssd_chunk(xbc_ref, rows, dt, state_ref, a_neg2, d_skip_x):
    L = SSD_CHUNK
    gw = SSD_GROUPS * SSD_STATE

    cs = _cumsum_rows(dt * a_neg2)
    cs_t = cs.T
    dt_t = dt.T
    w_t = dt_t * jnp.exp2(cs_t[:, L - 1:L] - cs_t)

    row = lax.broadcasted_iota(jnp.int32, (L, L), 0)
    col = lax.broadcasted_iota(jnp.int32, (L, L), 1)
    tril = col <= row
    lane = lax.broadcasted_iota(jnp.int32, (L, LANES), 1)
    low = lane < SSD_HEAD_DIM
    keep_lo = low.astype(F32).astype(BF16)
    keep_hi = 1.0 - keep_lo

    hpg = SSD_HEADS // SSD_GROUPS
    y_parts = []
    for g in range(SSD_GROUPS):
        b_g = xbc_ref[0, rows, SSD_INNER + g * SSD_STATE:SSD_INNER + (g + 1) * SSD_STATE]
        c_g = xbc_ref[0, rows, SSD_INNER + gw + g * SSD_STATE:SSD_INNER + gw + (g + 1) * SSD_STATE]
        cb = lax.dot_general(c_g, b_g, (((1,), (1,)), ((), ())),
                             preferred_element_type=F32)
        cb = jnp.where(tril, cb, 0.0)
        b_t = b_g.astype(F32).T
        glanes = slice(g * SSD_GROUP_WIDTH, (g + 1) * SSD_GROUP_WIDTH)
        y_off = _dot(c_g, state_ref[:, glanes].astype(BF16))
        for pair in range(hpg // 2):
            h0 = g * hpg + 2 * pair
            plane = slice(h0 * SSD_HEAD_DIM, (h0 + 2) * SSD_HEAD_DIM)
            tops, bots, decays = [], [], []
            for h in (h0, h0 + 1):
                cs_col = jnp.broadcast_to(cs[:, h:h + 1], (L, LANES))
                seg = cs_col - cs_t[h:h + 1, :]
                m_h = jnp.exp2(jnp.minimum(seg, 0.0)) * (cb * dt_t[h:h + 1, :])
                tops.append(m_h.astype(BF16))
                bots.append((b_t * w_t[h:h + 1, :]).astype(BF16))
                decays.append(jnp.exp2(cs_col))
            lhs = jnp.concatenate([jnp.concatenate(tops, axis=1),
                                   jnp.concatenate(bots, axis=1)], axis=0)
            x_pair = xbc_ref[0, rows, plane]
            rhs = jnp.concatenate([x_pair * keep_lo, x_pair * keep_hi], axis=0)
            res = _dot(lhs, rhs)
            dec = jnp.where(low, decays[0], decays[1])
            off = pair * 2 * SSD_HEAD_DIM
            y_pair = (res[:L] + dec * y_off[:, off:off + 2 * SSD_HEAD_DIM]
                      + d_skip_x[:, plane] * x_pair.astype(F32))
            y_parts.append(y_pair)
            state_ref[:, plane] = state_ref[:, plane] * dec[L - 1:L, :] + res[L:]
    return jnp.concatenate(y_parts, axis=1)


def _ssd_stage(z_ref, xbc_ref, dt_ref, x_ref, a_log_ref, d_skip_ref, norm_ref, w_out_ref,
               g_ref, b_ref, o_ref, state_ref, new_ref, old_ref, n_chunks,
               scan=True, finish=True):
    L = SSD_CHUNK
    if scan:
        a_neg2 = -jnp.exp(a_log_ref[...]) * LOG2E
        for c in range(n_chunks):
            rows = slice(c * L, (c + 1) * L)
            new_ref[rows, :] = _ssd_chunk(xbc_ref, rows, dt_ref[0, rows, :], state_ref, a_neg2,
                                          d_skip_ref[...])
    if finish:
        y = old_ref[...] * z_ref[0].astype(F32)
        parts = []
        for g in range(SSD_GROUPS):
            yg = y[:, g * SSD_GROUP_WIDTH:(g + 1) * SSD_GROUP_WIDTH]
            parts.append(yg * lax.rsqrt(jnp.mean(yg * yg, axis=-1, keepdims=True) + 1e-6))
        yn = (jnp.concatenate(parts, axis=1) * norm_ref[...]).astype(BF16)
        out = _dot(yn, w_out_ref[...])
        o_ref[0] = _layernorm(DN_ALPHA * x_ref[0] + out, g_ref[...], b_ref[...])


def _ssd_kernel(*refs, n_chunks):
    io_refs, state_ref, bufs = refs[:-3], refs[-3], refs[-2:]
    step = pl.program_id(1)
    last = pl.num_programs(1) - 1

    @pl.when(step == 0)
    def _():
        state_ref[...] = jnp.zeros_like(state_ref)
        _ssd_stage(*io_refs, state_ref, bufs[0], bufs[1], n_chunks, finish=False)

    for parity in range(2):
        @pl.when((step > 0) & (step < last) & (step % 2 == parity))
        def _(parity=parity):
            _ssd_stage(*io_refs, state_ref, bufs[parity], bufs[1 - parity], n_chunks)

        @pl.when((step == last) & (step % 2 == parity))
        def _(parity=parity):
            _ssd_stage(*io_refs, state_ref, bufs[parity], bufs[1 - parity], n_chunks, scan=False)


def _ssd_call(z, xbc, dt, x, a_log, d_skip_x, norm_w, w_out, ln_g, ln_b, n_chunks):
    bsz, seqlen, _ = x.shape
    tm = n_chunks * SSD_CHUNK
    n_tiles = seqlen // tm
    cur = lambda width: pl.BlockSpec((1, tm, width),
                                     lambda b, i: (b, jnp.minimum(i, n_tiles - 1), 0))
    prev = lambda width: pl.BlockSpec((1, tm, width),
                                      lambda b, i: (b, jnp.maximum(i - 1, 0), 0))
    return pl.pallas_call(
        functools.partial(_ssd_kernel, n_chunks=n_chunks),
        grid=(bsz, n_tiles + 1),
        in_specs=[prev(SSD_INNER), cur(SSD_CONV_DIM), cur(LANES), prev(D_MODEL),
                  _const_spec((1, LANES)), _const_spec((1, SSD_INNER)),
                  _const_spec((1, SSD_INNER)), _const_spec((SSD_INNER, D_MODEL)),
                  _const_spec((1, D_MODEL)), _const_spec((1, D_MODEL))],
        out_specs=prev(D_MODEL),
        out_shape=jax.ShapeDtypeStruct((bsz, seqlen, D_MODEL), F32),
        scratch_shapes=[pltpu.VMEM((SSD_STATE, SSD_INNER), F32),
                        pltpu.VMEM((tm, SSD_INNER), F32), pltpu.VMEM((tm, SSD_INNER), F32)],
        compiler_params=_params(("parallel", "arbitrary")),
        name="ssd_scan_out",
    )(z, xbc, dt, x, a_log, d_skip_x, norm_w, w_out, ln_g, ln_b)


def _rot_cols(w):
    half = MLA_ROPE // 2
    return jnp.concatenate([-w[..., half:], w[..., :half]], axis=-1)


def _pad_heads(w_heads):
    r, h, c = w_heads.shape
    return jnp.pad(w_heads, ((0, 0), (0, 0), (0, HEAD_PAD - c))).reshape(r, h * HEAD_PAD)


def _block_diag(w):
    h, d, _ = w.shape
    eye = jnp.eye(h, dtype=w.dtype)
    return (eye[:, None, :, None] * w[:, :, None, :]).reshape(h * d, h * d)


def _layer0(x, pos_f, freq, w_in, conv_w, conv_b, gate_a_w, gate_a_b, gate_x_w, gate_x_b, lam,
            q_norm, kv_norm, w_uq, w_ukv, w_out, ln_g, ln_b, tm, ts, tq):
    o_kr = RNN_WIDTH + AB_WIDTH + MLA_Q_RANK + MLA_KV_RANK
    w_kr = w_in[:, o_kr:]
    w_in_ext = jnp.concatenate([w_in[:, :o_kr], jnp.zeros_like(w_in[:, :MLA_NOPE]), w_kr,
                                _rot_cols(w_kr)], axis=1).astype(BF16)

    uq = w_uq.reshape(MLA_Q_RANK, MLA_HEADS, MLA_NOPE + MLA_ROPE)
    wq = _pad_heads(jnp.concatenate([uq, _rot_cols(uq[..., MLA_NOPE:])], axis=-1)).astype(BF16)
    ukv = w_ukv.reshape(MLA_KV_RANK, MLA_HEADS, MLA_NOPE + MLA_V)
    wk = _pad_heads(ukv[..., :MLA_NOPE]).astype(BF16)
    wvt = jnp.pad(jnp.transpose(ukv[..., MLA_NOPE:], (1, 2, 0)),
                  ((0, 0), (0, MLA_VT_ROWS - MLA_V), (0, 0)))
    wvt = wvt.reshape(MLA_HEADS * MLA_VT_ROWS, MLA_KV_RANK).astype(BF16)
    vones = jnp.tile((jnp.arange(MLA_VT_ROWS) >= MLA_V).astype(F32), MLA_HEADS)[:, None]

    xr, gate, q, k, vt = _ab_in_call(x, pos_f, w_in_ext, freq, q_norm[None], kv_norm[None],
                                     wq, wk, wvt, vones, tm)

    w_gate = jnp.concatenate([_block_diag(gate_a_w), _block_diag(gate_x_w)], axis=1).astype(BF16)
    b_gate = jnp.concatenate([gate_a_b, gate_x_b])[None]
    y_mla = _mla_attn_call(q, k, vt, tq)
    return _rglru_call(xr, y_mla, gate, x, conv_w, conv_b[None], w_gate, b_gate, lam[None],
                       w_out.astype(BF16), ln_g[None], ln_b[None], ts)


def _layer1(x, w_in, conv_w, conv_b, dt_bias, a_log, d_skip, norm_w, w_out, ln_g, ln_b,
            tm, n_chunks):
    pad_h = lambda a: jnp.pad(a, ((0, 0), (0, LANES - SSD_HEADS)))
    w_all = pad_h(w_in).astype(BF16)
    z, xbc, dt = _ssd_in_call(x, w_all, pad_h(dt_bias[None]), conv_w, conv_b[None], tm)
    d_skip_x = jnp.repeat(d_skip, SSD_HEAD_DIM)[None]
    return _ssd_call(z, xbc, dt, x, pad_h(a_log[None]), d_skip_x, norm_w[None],
                     w_out.astype(BF16), ln_g[None], ln_b[None], n_chunks)


def kernel(x, positions, ab_w_in, ab_conv_w, ab_conv_b, ab_gate_a_w, ab_gate_a_b, ab_gate_x_w,
           ab_gate_x_b, ab_lambda, mla_q_norm, mla_kv_norm, mla_w_uq, mla_w_ukv, ab_w_out,
           ab_ln_g, ab_ln_b, ssd_w_in, ssd_conv_w, ssd_conv_b, ssd_dt_bias, ssd_a_log, ssd_d,
           ssd_norm, ssd_w_out, ssd_ln_g, ssd_ln_b):
    seqlen = x.shape[1]
    tm = min(1024, seqlen)
    tm_ssd = min(256, seqlen)
    ts = min(128, seqlen)
    tq = min(512, seqlen)
    n_chunks = min(4, seqlen // SSD_CHUNK)

    inv_freq = ROPE_THETA ** (-jnp.arange(0, MLA_ROPE, 2, dtype=F32) / MLA_ROPE)
    freq = jnp.concatenate([jnp.zeros((MLA_NOPE,), F32), inv_freq, inv_freq,
                            jnp.zeros((HEAD_PAD - MLA_NOPE - MLA_ROPE,), F32)])[None]
    pos_f = positions.astype(F32)[:, None, :]

    for layer in range(DEPTH):
        j = layer // 2
        if layer % 2 == 0:
            x = _layer0(x, pos_f, freq, ab_w_in[j], ab_conv_w[j], ab_conv_b[j], ab_gate_a_w[j],
                        ab_gate_a_b[j], ab_gate_x_w[j], ab_gate_x_b[j], ab_lambda[j],
                        mla_q_norm[j], mla_kv_norm[j], mla_w_uq[j], mla_w_ukv[j], ab_w_out[j],
                        ab_ln_g[j], ab_ln_b[j], tm, ts, tq)
        else:
            x = _layer1(x, ssd_w_in[j], ssd_conv_w[j], ssd_conv_b[j], ssd_dt_bias[j],
                        ssd_a_log[j], ssd_d[j], ssd_norm[j], ssd_w_out[j], ssd_ln_g[j],
                        ssd_ln_b[j], tm_ssd, n_chunks)
    return x
```

```python
import functools
import math

import jax
import jax.numpy as jnp
from jax import lax
from jax.experimental import pallas as pl
from jax.experimental.pallas import tpu as pltpu

D_MODEL = 1024
DEPTH = 2
DN_ALPHA = (2.0 * DEPTH) ** 0.25

RNN_WIDTH = 512
RNN_HEADS = 8
RNN_HEAD_DIM = RNN_WIDTH // RNN_HEADS
CONV_WIDTH = 4
RG_C = 8.0

MLA_HEADS = 8
MLA_NOPE = 64
MLA_ROPE = 32
MLA_V = 64
MLA_Q_RANK = 256
MLA_KV_RANK = 128
MLA_WIDTH = MLA_HEADS * MLA_V
ROPE_THETA = 10000.0
AB_WIDTH = RNN_WIDTH + MLA_WIDTH

SSD_INNER = 2048
SSD_HEAD_DIM = 64
SSD_HEADS = 32
SSD_GROUPS = 4
SSD_STATE = 128
SSD_CHUNK = 128
SSD_CONV_DIM = SSD_INNER + 2 * SSD_GROUPS * SSD_STATE
SSD_GROUP_WIDTH = SSD_INNER // SSD_GROUPS

LANES = 128
SUBLANES = 8
BF16_ROWS = 2 * SUBLANES
HEAD_PAD = LANES
MLA_VT_ROWS = MLA_V + BF16_ROWS
QK_PAD_WIDTH = MLA_HEADS * HEAD_PAD
VMEM_LIMIT = 56 * 1024 * 1024

BF16 = jnp.bfloat16
F32 = jnp.float32


NEG_LOG2E = -math.log2(math.e)


def _exp_neg(x):
    return jnp.exp2(x * NEG_LOG2E)


def _sigmoid(x):
    return 1.0 / (1.0 + _exp_neg(x))


def _silu(x):
    return x * _sigmoid(x)


def _log1p(y):
    u = 1.0 + y
    return jnp.where(u == 1.0, y, jnp.log(u) * (y / (u - 1.0)))


def _softplus(x):
    return jnp.maximum(x, 0.0) + _log1p(_exp_neg(jnp.abs(x)))


def _dot(a, b):
    return jnp.dot(a, b, preferred_element_type=F32)


def _const_spec(shape):
    zeros = (0,) * len(shape)
    return pl.BlockSpec(shape, lambda *_: zeros)


def _params(semantics):
    return pltpu.CompilerParams(dimension_semantics=semantics,
                                vmem_limit_bytes=VMEM_LIMIT)


def _ab_in_kernel(x_ref, pos_ref, w_in_ref, freq_ref, qn_ref, kvn_ref, wq_ref,
                  wk_ref, wvt_ref, vones_ref,
                  xr_ref, gate_ref, q_ref, k_ref, vt_ref):
    xb = x_ref[0].astype(BF16)
    proj = _dot(xb, w_in_ref[...])
    o_gate = RNN_WIDTH
    o_cq = o_gate + AB_WIDTH
    o_ckv = o_cq + MLA_Q_RANK
    o_kr = o_ckv + MLA_KV_RANK

    xr_ref[0] = proj[:, :o_gate]
    gate_ref[0] = _silu(proj[:, o_gate:o_cq]).astype(BF16)

    pos_rows = jnp.broadcast_to(pos_ref[0], (HEAD_PAD, pos_ref.shape[2])).T
    ang = pos_rows * freq_ref[...]
    cos = jnp.cos(ang)
    sin = jnp.sin(ang)
    to_rope_lanes = HEAD_PAD - MLA_ROPE

    c_q = proj[:, o_cq:o_ckv]
    c_q = c_q * lax.rsqrt(jnp.mean(c_q * c_q, axis=-1, keepdims=True) + 1e-6) * qn_ref[...]
    q_all = _dot(c_q.astype(BF16), wq_ref[...])
    scale = (MLA_NOPE + MLA_ROPE) ** -0.5 * math.log2(math.e)
    for h in range(MLA_HEADS):
        q_h = q_all[:, h * HEAD_PAD:(h + 1) * HEAD_PAD]
        q_h = q_h * cos + pltpu.roll(q_h, to_rope_lanes, axis=1) * sin
        q_ref[0, :, h * HEAD_PAD:(h + 1) * HEAD_PAD] = (q_h * scale).astype(BF16)

    c_kv = proj[:, o_ckv:o_kr]
    c_kv = c_kv * lax.rsqrt(jnp.mean(c_kv * c_kv, axis=-1, keepdims=True) + 1e-6) * kvn_ref[...]
    c_kvb = c_kv.astype(BF16)
    kr = proj[:, o_kr:o_kr + HEAD_PAD]
    lane = lax.broadcasted_iota(jnp.int32, kr.shape, 1)
    k_rope = jnp.where(lane < MLA_NOPE + MLA_ROPE,
                       kr * cos + pltpu.roll(kr, to_rope_lanes, axis=1) * sin, 0.0)
    k = _dot(c_kvb, wk_ref[...]) + jnp.concatenate([k_rope] * MLA_HEADS, axis=1)
    k_ref[0] = k.astype(BF16)
    vt = lax.dot_general(wvt_ref[...], c_kvb, (((1,), (1,)), ((), ())),
                         preferred_element_type=F32) + vones_ref[...]
    vt_ref[0] = vt.reshape(MLA_HEADS, MLA_VT_ROWS, vt.shape[1]).astype(BF16)


def _ab_in_call(x, pos_f, w_in_ext, freq, q_norm, kv_norm, wq, wk, wvt, vones, tm):
    bsz, seqlen, _ = x.shape
    n_in = w_in_ext.shape[1]
    grid = (bsz, seqlen // tm)
    tok = lambda width: pl.BlockSpec((1, tm, width), lambda b, i: (b, i, 0))
    vt_rows = MLA_HEADS * MLA_VT_ROWS
    return pl.pallas_call(
        _ab_in_kernel,
        grid=grid,
        in_specs=[tok(D_MODEL), pl.BlockSpec((1, 1, tm), lambda b, i: (b, 0, i)),
                  _const_spec((D_MODEL, n_in)), _const_spec((1, HEAD_PAD)),
                  _const_spec((1, MLA_Q_RANK)), _const_spec((1, MLA_KV_RANK)),
                  _const_spec((MLA_Q_RANK, QK_PAD_WIDTH)),
                  _const_spec((MLA_KV_RANK, QK_PAD_WIDTH)), _const_spec((vt_rows, MLA_KV_RANK)),
                  _const_spec((vt_rows, 1))],
        out_specs=[tok(RNN_WIDTH), tok(AB_WIDTH), tok(QK_PAD_WIDTH), tok(QK_PAD_WIDTH),
                   pl.BlockSpec((1, MLA_HEADS, MLA_VT_ROWS, tm), lambda b, i: (b, 0, 0, i))],
        out_shape=[jax.ShapeDtypeStruct((bsz, seqlen, RNN_WIDTH), F32),
                   jax.ShapeDtypeStruct((bsz, seqlen, AB_WIDTH), BF16),
                   jax.ShapeDtypeStruct((bsz, seqlen, QK_PAD_WIDTH), BF16),
                   jax.ShapeDtypeStruct((bsz, seqlen, QK_PAD_WIDTH), BF16),
                   jax.ShapeDtypeStruct((bsz, MLA_HEADS, MLA_VT_ROWS, seqlen), BF16)],
        compiler_params=_params(("parallel", "parallel")),
        name="ab_in_proj",
    )(x, pos_f, w_in_ext, freq, q_norm, kv_norm, wq, wk, wvt, vones)


def _layernorm(z, g, b):
    mu = jnp.mean(z, axis=-1, keepdims=True)
    zc = z - mu
    var = jnp.mean(zc * zc, axis=-1, keepdims=True)
    return zc * lax.rsqrt(var + 1e-5) * g + b


def _rglru_kernel(xr_ref, ya_ref, gate_ref, x_ref, conv_w_ref, conv_b_ref, w_gate_ref,
                  b_gate_ref, lam_ref, w_out_ref, ln_g_ref, ln_b_ref,
                  o_ref, tail_ref, carry_ref, a_ref, u_ref):
    bsz, ts, width = xr_ref.shape

    @pl.when(pl.program_id(0) == 0)
    def _():
        tail_ref[...] = jnp.zeros_like(tail_ref)
        carry_ref[...] = jnp.zeros_like(carry_ref)

    x_blk = jnp.swapaxes(xr_ref[...], 0, 1)
    x_ext = jnp.concatenate([tail_ref[...], x_blk], axis=0)
    tail_ref[...] = x_blk[ts - (CONV_WIDTH - 1):]
    xc = conv_b_ref[...][None]
    for k in range(CONV_WIDTH):
        xc = xc + x_ext[k:k + ts] * conv_w_ref[k:k + 1, :][None]

    xc2 = xc.reshape(ts * bsz, width)
    gates = _dot(xc2.astype(BF16), w_gate_ref[...]) + b_gate_ref[...]
    r = _sigmoid(gates[:, :width])
    i = _sigmoid(gates[:, width:])
    neg_c_softplus = -RG_C * _softplus(-lam_ref[...])
    log_a = r * neg_c_softplus
    a = jnp.exp2(r * (neg_c_softplus * -NEG_LOG2E))
    mult = jnp.sqrt(-jnp.tanh(log_a) * (a * a + 1.0))
    u = mult * (i * xc2)
    a_ref[...] = a.reshape(ts, bsz, width)
    u_ref[...] = u.reshape(ts, bsz, width)

    def step(t, h):
        h = a_ref[t] * h + u_ref[t]
        u_ref[t] = h
        return h

    carry_ref[...] = lax.fori_loop(0, ts, step, carry_ref[...], unroll=8)

    rows = bsz * ts
    h = jnp.swapaxes(u_ref[...], 0, 1).reshape(rows, width)
    gate = gate_ref[...].reshape(rows, AB_WIDTH).astype(F32)
    y_rnn = (h * gate[:, :RNN_WIDTH]).astype(BF16)
    y_mla = (ya_ref[...].reshape(rows, MLA_WIDTH).astype(F32) * gate[:, RNN_WIDTH:]).astype(BF16)
    y = _dot(y_rnn, w_out_ref[:RNN_WIDTH, :]) + _dot(y_mla, w_out_ref[RNN_WIDTH:, :])
    out = _layernorm(DN_ALPHA * x_ref[...].reshape(rows, D_MODEL) + y,
                     ln_g_ref[...], ln_b_ref[...])
    o_ref[...] = out.reshape(bsz, ts, D_MODEL)


def _rglru_call(xr, y_mla, gate, x, conv_w, conv_b, w_gate, b_gate, lam, w_out, ln_g, ln_b, ts):
    bsz, seqlen, width = xr.shape
    blk = lambda w: pl.BlockSpec((bsz, ts, w), lambda i: (0, i, 0))
    return pl.pallas_call(
        _rglru_kernel,
        grid=(seqlen // ts,),
        in_specs=[blk(width), blk(MLA_WIDTH), blk(AB_WIDTH), blk(D_MODEL),
                  _const_spec((CONV_WIDTH, width)), _const_spec((1, width)),
                  _const_spec((width, 2 * width)), _const_spec((1, 2 * width)),
                  _const_spec((1, width)), _const_spec((AB_WIDTH, D_MODEL)),
                  _const_spec((1, D_MODEL)), _const_spec((1, D_MODEL))],
        out_specs=blk(D_MODEL),
        out_shape=jax.ShapeDtypeStruct((bsz, seqlen, D_MODEL), F32),
        scratch_shapes=[pltpu.VMEM((CONV_WIDTH - 1, bsz, width), F32),
                        pltpu.VMEM((bsz, width), F32),
                        pltpu.VMEM((ts, bsz, width), F32),
                        pltpu.VMEM((ts, bsz, width), F32)],
        compiler_params=_params(("arbitrary",)),
        name="rglru_out_proj",
    )(xr, y_mla, gate, x, conv_w, conv_b, w_gate, b_gate, lam, w_out, ln_g, ln_b)


ATTN_ROWS = 256


def _mla_attn_tiles(q_ref, k_ref, vt_ref, o_ref, s_refs, tiles, tq):
    heads = q_ref.shape[2] // HEAD_PAD
    halves = tq // ATTN_ROWS
    units = [(h, r) for h in range(heads) for r in range(halves)]
    steps = [(qi, j) for qi in tiles for j in range(qi + 1)]

    def depth(qi, j, r):
        return (r + 1) * ATTN_ROWS if j == qi else tq

    def scores_into(dst_ref, qi, j):
        for u, (h, r) in enumerate(units):
            w = depth(qi, j, r)
            q_u = q_ref[0, qi * tq + r * ATTN_ROWS:qi * tq + (r + 1) * ATTN_ROWS,
                        h * HEAD_PAD:(h + 1) * HEAD_PAD]
            k_blk = k_ref[0, j * tq:j * tq + w, h * HEAD_PAD:(h + 1) * HEAD_PAD]
            dst_ref[u, :w, :] = lax.dot_general(k_blk, q_u, (((1,), (1,)), ((), ())),
                                                preferred_element_type=F32)

    scores_into(s_refs[0], *steps[0])
    ms, accs = None, None
    for i, (qi, j) in enumerate(steps):
        src_ref = s_refs[i % 2]
        if i + 1 < len(steps):
            scores_into(s_refs[(i + 1) % 2], *steps[i + 1])
        if j == 0:
            ms = [jnp.full((1, ATTN_ROWS), -1e30, F32) for _ in units]
            accs = [jnp.zeros((MLA_VT_ROWS, ATTN_ROWS), F32) for _ in units]
        for u, (h, r) in enumerate(units):
            w = depth(qi, j, r)
            if j == qi:
                key = lax.broadcasted_iota(jnp.int32, (ATTN_ROWS, ATTN_ROWS), 0)
                qry = lax.broadcasted_iota(jnp.int32, (ATTN_ROWS, ATTN_ROWS), 1)
                edge = slice(w - ATTN_ROWS, w)
                src_ref[u, edge, :] = jnp.where(key <= qry, src_ref[u, edge, :], -1e30)
            vt_blk = vt_ref[0, h, :, j * tq:j * tq + w]
            m_new = jnp.maximum(ms[u], jnp.max(src_ref[u, :w, :], axis=0, keepdims=True))
            p = jnp.exp2((src_ref[u, :w, :] - m_new).astype(BF16))
            accs[u] = jnp.exp2(ms[u] - m_new) * accs[u] + _dot(vt_blk, p)
            ms[u] = m_new
        if j == qi:
            outs = [acc[:MLA_V] / acc[MLA_V:MLA_V + 1] for acc in accs]
            for pair in range(heads // 2):
                for r in range(halves):
                    o_t = jnp.concatenate([outs[2 * pair * halves + r],
                                           outs[(2 * pair + 1) * halves + r]], axis=0)
                    o_ref[0, qi * tq + r * ATTN_ROWS:qi * tq + (r + 1) * ATTN_ROWS,
                          pair * HEAD_PAD:(pair + 1) * HEAD_PAD] = o_t.T.astype(o_ref.dtype)


def _mla_attn_kernel(q_ref, k_ref, vt_ref, o_ref, s0_ref, s1_ref, *, tq, groups):
    g = pl.program_id(2)
    for idx, tiles in enumerate(groups):
        @pl.when(g == idx)
        def _(tiles=tiles):
            _mla_attn_tiles(q_ref, k_ref, vt_ref, o_ref, (s0_ref, s1_ref), tiles, tq)


def _mla_attn_call(q, k, vt, tq):
    bsz, seqlen, _ = q.shape
    heads_per_step = 2
    width = heads_per_step * HEAD_PAD
    n_hsteps = MLA_HEADS // heads_per_step
    out_w = heads_per_step * MLA_V
    n_units = heads_per_step * (tq // ATTN_ROWS)
    n_q = seqlen // tq
    groups = tuple(tuple(sorted({i, n_q - 1 - i})) for i in range((n_q + 1) // 2))
    resident = lambda w: pl.BlockSpec((1, seqlen, w), lambda b, h, g: (b, 0, h))
    return pl.pallas_call(
        functools.partial(_mla_attn_kernel, tq=tq, groups=groups),
        grid=(bsz, n_hsteps, len(groups)),
        in_specs=[resident(width), resident(width),
                  pl.BlockSpec((1, heads_per_step, MLA_VT_ROWS, seqlen),
                               lambda b, h, g: (b, h, 0, 0))],
        out_specs=resident(out_w),
        out_shape=jax.ShapeDtypeStruct((bsz, seqlen, MLA_WIDTH), BF16),
        scratch_shapes=[pltpu.VMEM((n_units, tq, ATTN_ROWS), F32),
                        pltpu.VMEM((n_units, tq, ATTN_ROWS), F32)],
        compiler_params=_params(("parallel", "parallel", "arbitrary")),
        name="mla_attention",
    )(q, k, vt)


SSD_IN_COLS = 512


def _conv_silu_store(proj, tail, w, b, out_ref, cols):
    n_rows = proj.shape[0]
    taps = [w[k:k + 1, :] for k in range(CONV_WIDTH)]
    first = lax.broadcasted_iota(jnp.int32, tail.shape, 0) == 0
    prev = None
    done = []
    for i in range(-1, n_rows // SUBLANES):
        x = tail if i < 0 else proj[i * SUBLANES:(i + 1) * SUBLANES]
        acc = x * taps[0]
        rolled = []
        for k in range(1, CONV_WIDTH):
            r = pltpu.roll(acc, 1, axis=0)
            rolled.append(r)
            acc = (r if prev is None else jnp.where(first, prev[k - 1], r)) + x * taps[k]
        prev = rolled
        if i >= 0:
            done.append(acc + b)
        if len(done) == BF16_ROWS // SUBLANES:
            top = (i + 1) * SUBLANES
            out_ref[0, top - BF16_ROWS:top, cols] = _silu(
                jnp.concatenate(done, axis=0)).astype(BF16)
            done = []


SSD_PROJ_WIDTH = SSD_INNER + SSD_CONV_DIM + LANES


def _ssd_in_stage(step, x_ref, w_ref, dt_bias_ref, conv_w_ref, conv_b_ref,
                  z_ref, xbc_ref, dt_ref, new_ref, old_ref, project=True, finish=True):
    tm = x_ref.shape[1]
    x_off = SSD_INNER
    if finish:
        tails = []
        for c in range(SSD_CONV_DIM // SSD_IN_COLS):
            cols = slice(x_off + c * SSD_IN_COLS, x_off + (c + 1) * SSD_IN_COLS)
            tails.append(jnp.where(step == 1, 0.0, new_ref[tm - SUBLANES:, cols]))

    if project:
        xb = x_ref[0].astype(BF16)
        for c in range(0, SSD_PROJ_WIDTH, SSD_IN_COLS):
            cols = slice(c, min(c + SSD_IN_COLS, SSD_PROJ_WIDTH))
            new_ref[:, cols] = _dot(xb, w_ref[:, cols])
    if not finish:
        return

    for c in range(SSD_INNER // SSD_IN_COLS):
        cols = slice(c * SSD_IN_COLS, (c + 1) * SSD_IN_COLS)
        z_ref[0, :, cols] = _silu(old_ref[:, cols]).astype(BF16)
    for c in range(SSD_CONV_DIM // SSD_IN_COLS):
        cols = slice(c * SSD_IN_COLS, (c + 1) * SSD_IN_COLS)
        p_cols = slice(x_off + c * SSD_IN_COLS, x_off + (c + 1) * SSD_IN_COLS)
        _conv_silu_store(old_ref[:, p_cols], tails[c], conv_w_ref[:, cols], conv_b_ref[:, cols],
                         xbc_ref, cols)
    dt_ref[0] = _softplus(old_ref[:, SSD_INNER + SSD_CONV_DIM:] + dt_bias_ref[...])


def _ssd_in_kernel(x_ref, w_ref, dt_bias_ref, conv_w_ref, conv_b_ref,
                   z_ref, xbc_ref, dt_ref, p0_ref, p1_ref):
    step = pl.program_id(1)
    last = pl.num_programs(1) - 1
    args = (step, x_ref, w_ref, dt_bias_ref, conv_w_ref, conv_b_ref, z_ref, xbc_ref, dt_ref)
    bufs = (p0_ref, p1_ref)

    @pl.when(step == 0)
    def _():
        p1_ref[p1_ref.shape[0] - SUBLANES:, :] = jnp.zeros((SUBLANES, p1_ref.shape[1]), F32)
        _ssd_in_stage(*args, p0_ref, p1_ref, finish=False)

    for parity in range(2):
        @pl.when((step > 0) & (step < last) & (step % 2 == parity))
        def _(parity=parity):
            _ssd_in_stage(*args, bufs[parity], bufs[1 - parity])

        @pl.when((step == last) & (step % 2 == parity))
        def _(parity=parity):
            _ssd_in_stage(*args, bufs[parity], bufs[1 - parity], project=False)


def _ssd_in_call(x, w_all, dt_bias, conv_w, conv_b, tm):
    bsz, seqlen, _ = x.shape
    n_tiles = seqlen // tm
    tile_in = pl.BlockSpec((1, tm, D_MODEL), lambda b, i: (b, jnp.minimum(i, n_tiles - 1), 0))
    tile_out = lambda width: pl.BlockSpec((1, tm, width),
                                          lambda b, i: (b, jnp.maximum(i - 1, 0), 0))
    return pl.pallas_call(
        _ssd_in_kernel,
        grid=(bsz, n_tiles + 1),
        in_specs=[tile_in, _const_spec((D_MODEL, SSD_PROJ_WIDTH)),
                  _const_spec((1, LANES)), _const_spec((CONV_WIDTH, SSD_CONV_DIM)),
                  _const_spec((1, SSD_CONV_DIM))],
        out_specs=[tile_out(SSD_INNER), tile_out(SSD_CONV_DIM), tile_out(LANES)],
        out_shape=[jax.ShapeDtypeStruct((bsz, seqlen, SSD_INNER), BF16),
                   jax.ShapeDtypeStruct((bsz, seqlen, SSD_CONV_DIM), BF16),
                   jax.ShapeDtypeStruct((bsz, seqlen, LANES), F32)],
        scratch_shapes=[pltpu.VMEM((tm, SSD_PROJ_WIDTH), F32),
                        pltpu.VMEM((tm, SSD_PROJ_WIDTH), F32)],
        compiler_params=_params(("parallel", "arbitrary")),
        name="ssd_in_proj",
    )(x, w_all, dt_bias, conv_w, conv_b)


def _cumsum_rows(x):
    n = x.shape[0]
    row = lax.broadcasted_iota(jnp.int32, x.shape, 0)
    shift = 1
    while shift < n:
        x = x + jnp.where(row >= shift, pltpu.roll(x, shift, axis=0), 0.0)
        shift *= 2
    return x


LOG2E = math.log2(math.e)


def _ssd_chunk(xbc_ref, rows, dt, state_ref, a_neg2, d_skip_x):
    L = SSD_CHUNK
    gw = SSD_GROUPS * SSD_STATE

    cs = _cumsum_rows(dt * a_neg2)
    cs_t = cs.T
    dt_t = dt.T
    w_t = dt_t * jnp.exp2(cs_t[:, L - 1:L] - cs_t)

    row = lax.broadcasted_iota(jnp.int32, (L, L), 0)
    col = lax.broadcasted_iota(jnp.int32, (L, L), 1)
    tril = col <= row
    lane = lax.broadcasted_iota(jnp.int32, (L, LANES), 1)
    low = lane < SSD_HEAD_DIM
    keep_lo = low.astype(F32).astype(BF16)
    keep_hi = 1.0 - keep_lo

    hpg = SSD_HEADS // SSD_GROUPS
    y_parts = []
    for g in range(SSD_GROUPS):
        b_g = xbc_ref[0, rows, SSD_INNER + g * SSD_STATE:SSD_INNER + (g + 1) * SSD_STATE]
        c_g = xbc_ref[0, rows, SSD_INNER + gw + g * SSD_STATE:SSD_INNER + gw + (g + 1) * SSD_STATE]
        cb = lax.dot_general(c_g, b_g, (((1,), (1,)), ((), ())),
                             preferred_element_type=F32)
        cb = jnp.where(tril, cb, 0.0)
        b_t = b_g.astype(F32).T
        glanes = slice(g * SSD_GROUP_WIDTH, (g + 1) * SSD_GROUP_WIDTH)
        y_off = _dot(c_g, state_ref[:, glanes].astype(BF16))
        for pair in range(hpg // 2):
            h0 = g * hpg + 2 * pair
            plane = slice(h0 * SSD_HEAD_DIM, (h0 + 2) * SSD_HEAD_DIM)
            tops, bots, decays = [], [], []
            for h in (h0, h0 + 1):
                cs_col = jnp.broadcast_to(cs[:, h:h + 1], (L, LANES))
                seg = cs_col - cs_t[h:h + 1, :]
                m_h = jnp.exp2(jnp.minimum(seg, 0.0)) * (cb * dt_t[h:h + 1, :])
                tops.append(m_h.astype(BF16))
                bots.append((b_t * w_t[h:h + 1, :]).astype(BF16))
                decays.append(jnp.exp2(cs_col))
            lhs = jnp.concatenate([jnp.concatenate(tops, axis=1),
                                   jnp.concatenate(bots, axis=1)], axis=0)
            x_pair = xbc_ref[0, rows, plane]
            rhs = jnp.concatenate([x_pair * keep_lo, x_pair * keep_hi], axis=0)
            res = _dot(lhs, rhs)
            dec = jnp.where(low, decays[0], decays[1])
            off = pair * 2 * SSD_HEAD_DIM
            y_pair = (res[:L] + dec * y_off[:, off:off + 2 * SSD_HEAD_DIM]
                      + d_skip_x[:, plane] * x_pair.astype(F32))
            y_parts.append(y_pair)
            state_ref[:, plane] = state_ref[:, plane] * dec[L - 1:L, :] + res[L:]
    return jnp.concatenate(y_parts, axis=1)


def _ssd_stage(z_ref, xbc_ref, dt_ref, x_ref, a_log_ref, d_skip_ref, norm_ref, w_out_ref,
               g_ref, b_ref, o_ref, state_ref, new_ref, old_ref, n_chunks,
               scan=True, finish=True):
    L = SSD_CHUNK
    if scan:
        a_neg2 = -jnp.exp(a_log_ref[...]) * LOG2E
        for c in range(n_chunks):
            rows = slice(c * L, (c + 1) * L)
            new_ref[rows, :] = _ssd_chunk(xbc_ref, rows, dt_ref[0, rows, :], state_ref, a_neg2,
                                          d_skip_ref[...])
    if finish:
        y = old_ref[...] * z_ref[0].astype(F32)
        parts = []
        for g in range(SSD_GROUPS):
            yg = y[:, g * SSD_GROUP_WIDTH:(g + 1) * SSD_GROUP_WIDTH]
            parts.append(yg * lax.rsqrt(jnp.mean(yg * yg, axis=-1, keepdims=True) + 1e-6))
        yn = (jnp.concatenate(parts, axis=1) * norm_ref[...]).astype(BF16)
        out = _dot(yn, w_out_ref[...])
        o_ref[0] = _layernorm(DN_ALPHA * x_ref[0] + out, g_ref[...], b_ref[...])


def _ssd_kernel(*refs, n_chunks):
    io_refs, state_ref, bufs = refs[:-3], refs[-3], refs[-2:]
    step = pl.program_id(1)
    last = pl.num_programs(1) - 1

    @pl.when(step == 0)
    def _():
        state_ref[...] = jnp.zeros_like(state_ref)
        _ssd_stage(*io_refs, state_ref, bufs[0], bufs[1], n_chunks, finish=False)

    for parity in range(2):
        @pl.when((step > 0) & (step < last) & (step % 2 == parity))
        def _(parity=parity):
            _ssd_stage(*io_refs, state_ref, bufs[parity], bufs[1 - parity], n_chunks)

        @pl.when((step == last) & (step % 2 == parity))
        def _(parity=parity):
            _ssd_stage(*io_refs, state_ref, bufs[parity], bufs[1 - parity], n_chunks, scan=False)


def _ssd_call(z, xbc, dt, x, a_log, d_skip_x, norm_w, w_out, ln_g, ln_b, n_chunks):
    bsz, seqlen, _ = x.shape
    tm = n_chunks * SSD_CHUNK
    n_tiles = seqlen // tm
    cur = lambda width: pl.BlockSpec((1, tm, width),
                                     lambda b, i: (b, jnp.minimum(i, n_tiles - 1), 0))
    prev = lambda width: pl.BlockSpec((1, tm, width),
                                      lambda b, i: (b, jnp.maximum(i - 1, 0), 0))
    return pl.pallas_call(
        functools.partial(_ssd_kernel, n_chunks=n_chunks),
        grid=(bsz, n_tiles + 1),
        in_specs=[prev(SSD_INNER), cur(SSD_CONV_DIM), cur(LANES), prev(D_MODEL),
                  _const_spec((1, LANES)), _const_spec((1, SSD_INNER)),
                  _const_spec((1, SSD_INNER)), _const_spec((SSD_INNER, D_MODEL)),
                  _const_spec((1, D_MODEL)), _const_spec((1, D_MODEL))],
        out_specs=prev(D_MODEL),
        out_shape=jax.ShapeDtypeStruct((bsz, seqlen, D_MODEL), F32),
        scratch_shapes=[pltpu.VMEM((SSD_STATE, SSD_INNER), F32),
                        pltpu.VMEM((tm, SSD_INNER), F32), pltpu.VMEM((tm, SSD_INNER), F32)],
        compiler_params=_params(("parallel", "arbitrary")),
        name="ssd_scan_out",
    )(z, xbc, dt, x, a_log, d_skip_x, norm_w, w_out, ln_g, ln_b)


def _rot_cols(w):
    half = MLA_ROPE // 2
    return jnp.concatenate([-w[..., half:], w[..., :half]], axis=-1)


def _pad_heads(w_heads):
    r, h, c = w_heads.shape
    return jnp.pad(w_heads, ((0, 0), (0, 0), (0, HEAD_PAD - c))).reshape(r, h * HEAD_PAD)


def _block_diag(w):
    h, d, _ = w.shape
    eye = jnp.eye(h, dtype=w.dtype)
    return (eye[:, None, :, None] * w[:, :, None, :]).reshape(h * d, h * d)


def _layer0(x, pos_f, freq, w_in, conv_w, conv_b, gate_a_w, gate_a_b, gate_x_w, gate_x_b, lam,
            q_norm, kv_norm, w_uq, w_ukv, w_out, ln_g, ln_b, tm, ts, tq):
    o_kr = RNN_WIDTH + AB_WIDTH + MLA_Q_RANK + MLA_KV_RANK
    w_kr = w_in[:, o_kr:]
    w_in_ext = jnp.concatenate([w_in[:, :o_kr], jnp.zeros_like(w_in[:, :MLA_NOPE]), w_kr,
                                _rot_cols(w_kr)], axis=1).astype(BF16)

    uq = w_uq.reshape(MLA_Q_RANK, MLA_HEADS, MLA_NOPE + MLA_ROPE)
    wq = _pad_heads(jnp.concatenate([uq, _rot_cols(uq[..., MLA_NOPE:])], axis=-1)).astype(BF16)
    ukv = w_ukv.reshape(MLA_KV_RANK, MLA_HEADS, MLA_NOPE + MLA_V)
    wk = _pad_heads(ukv[..., :MLA_NOPE]).astype(BF16)
    wvt = jnp.pad(jnp.transpose(ukv[..., MLA_NOPE:], (1, 2, 0)),
                  ((0, 0), (0, MLA_VT_ROWS - MLA_V), (0, 0)))
    wvt = wvt.reshape(MLA_HEADS * MLA_VT_ROWS, MLA_KV_RANK).astype(BF16)
    vones = jnp.tile((jnp.arange(MLA_VT_ROWS) >= MLA_V).astype(F32), MLA_HEADS)[:, None]

    xr, gate, q, k, vt = _ab_in_call(x, pos_f, w_in_ext, freq, q_norm[None], kv_norm[None],
                                     wq, wk, wvt, vones, tm)

    w_gate = jnp.concatenate([_block_diag(gate_a_w), _block_diag(gate_x_w)], axis=1).astype(BF16)
    b_gate = jnp.concatenate([gate_a_b, gate_x_b])[None]
    y_mla = _mla_attn_call(q, k, vt, tq)
    return _rglru_call(xr, y_mla, gate, x, conv_w, conv_b[None], w_gate, b_gate, lam[None],
                       w_out.astype(BF16), ln_g[None], ln_b[None], ts)


def _layer1(x, w_in, conv_w, conv_b, dt_bias, a_log, d_skip, norm_w, w_out, ln_g, ln_b,
            tm, n_chunks):
    pad_h = lambda a: jnp.pad(a, ((0, 0), (0, LANES - SSD_HEADS)))
    w_all = pad_h(w_in).astype(BF16)
    z, xbc, dt = _ssd_in_call(x, w_all, pad_h(dt_bias[None]), conv_w, conv_b[None], tm)
    d_skip_x = jnp.repeat(d_skip, SSD_HEAD_DIM)[None]
    return _ssd_call(z, xbc, dt, x, pad_h(a_log[None]), d_skip_x, norm_w[None],
                     w_out.astype(BF16), ln_g[None], ln_b[None], n_chunks)


def kernel(x, positions, ab_w_in, ab_conv_w, ab_conv_b, ab_gate_a_w, ab_gate_a_b, ab_gate_x_w,
           ab_gate_x_b, ab_lambda, mla_q_norm, mla_kv_norm, mla_w_uq, mla_w_ukv, ab_w_out,
           ab_ln_g, ab_ln_b, ssd_w_in, ssd_conv_w, ssd_conv_b, ssd_dt_bias, ssd_a_log, ssd_d,
           ssd_norm, ssd_w_out, ssd_ln_g, ssd_ln_b):
    seqlen = x.shape[1]
    tm = min(1024, seqlen)
    tm_ssd = min(512, seqlen)
    ts = min(128, seqlen)
    tq = min(512, seqlen)
    n_chunks = min(4, seqlen // SSD_CHUNK)

    inv_freq = ROPE_THETA ** (-jnp.arange(0, MLA_ROPE, 2, dtype=F32) / MLA_ROPE)
    freq = jnp.concatenate([jnp.zeros((MLA_NOPE,), F32), inv_freq, inv_freq,
                            jnp.zeros((HEAD_PAD - MLA_NOPE - MLA_ROPE,), F32)])[None]
    pos_f = positions.astype(F32)[:, None, :]

    for layer in range(DEPTH):
        j = layer // 2
        if layer % 2 == 0:
            x = _layer0(x, pos_f, freq, ab_w_in[j], ab_conv_w[j], ab_conv_b[j], ab_gate_a_w[j],
                        ab_gate_a_b[j], ab_gate_x_w[j], ab_gate_x_b[j], ab_lambda[j],
                        mla_q_norm[j], mla_kv_norm[j], mla_w_uq[j], mla_w_ukv[j], ab_w_out[j],
                        ab_ln_g[j], ab_ln_b[j], tm, ts, tq)
        else:
            x = _layer1(x, ssd_w_in[j], ssd_conv_w[j], ssd_conv_b[j], ssd_dt_bias[j],
                        ssd_a_log[j], ssd_d[j], ssd_norm[j], ssd_w_out[j], ssd_ln_g[j],
                        ssd_ln_b[j], tm_ssd, n_chunks)
    return x
```
